```python
import math
import jax, jax.numpy as jnp
from jax import lax
import numpy as np

D_MODEL = 1024
BATCH = 8
SEQ = 2048
DEPTH = 4
DEC_BATCH = 128
DEC_SEQ = 4
PAST_LEN = 2048
PAGE_SIZE = 128

HEAD_DIM = 64
H_A = 8
KV_A = 2
H_IDX = 8
D_IDX = 64
TOPK_MAX = 256
H_B = 8
ROT_DIM = HEAD_DIM // 4
ROPE_THETA = 500000.0
QBLOCK = 128
CONV_WIDTH = 31
D_CONV = D_MODEL
D_FF = ((-(-8 * D_MODEL // 3) + 255) // 256) * 256
N_ATT = (DEPTH + 1) // 2
N_CONV = DEPTH // 2
EPS = 1e-6
SPLITS = (H_A * HEAD_DIM, KV_A * HEAD_DIM, KV_A * HEAD_DIM, H_IDX * D_IDX, D_IDX, H_IDX,
          H_B * HEAD_DIM, H_B * HEAD_DIM, H_B * HEAD_DIM, H_B)
D_IN_ATT = sum(SPLITS)
D_MIX = H_A * HEAD_DIM + H_B * HEAD_DIM

kernel_name = 'hybrid_dsa_fox_conformer_adaln_step'

F32 = jnp.float32


def rms_norm(x, g):
    x32 = x.astype(F32)
    y = x32 * lax.rsqrt(jnp.mean(x32 * x32, axis=-1, keepdims=True) + EPS)
    return (y * g.astype(F32)).astype(x.dtype)


def modulate(x, g, shift, scale):
    return rms_norm(x, g) * (1 + scale[:, None, :]) + shift[:, None, :]


def rope(x, pos):
    half = ROT_DIM // 2
    inv = ROPE_THETA ** (-jnp.arange(half, dtype=F32) * 2.0 / ROT_DIM)
    ang = pos[:, None] * inv[None, :]
    cos = jnp.cos(ang)[None, :, None, :].astype(x.dtype)
    sin = jnp.sin(ang)[None, :, None, :].astype(x.dtype)
    x1 = x[..., :half]
    x2 = x[..., half:ROT_DIM]
    return jnp.concatenate([x1 * cos - x2 * sin, x2 * cos + x1 * sin, x[..., ROT_DIM:]], axis=-1)


def even_projections(h, w_in, b_f, pos):
    B, T, _ = h.shape
    u = h @ w_in
    parts = []
    off = 0
    for w in SPLITS:
        parts.append(u[..., off:off + w])
        off += w
    qa, ka, va, qi, ki, wi, qb, kb, vb, fb = parts
    qa = rope(qa.reshape(B, T, H_A, HEAD_DIM), pos)
    ka = rope(ka.reshape(B, T, KV_A, HEAD_DIM), pos)
    va = va.reshape(B, T, KV_A, HEAD_DIM)
    qi = rope(qi.reshape(B, T, H_IDX, D_IDX), pos)
    ki = rope(ki.reshape(B, T, 1, D_IDX), pos)[:, :, 0]
    qb = qb.reshape(B, T, H_B, HEAD_DIM)
    kb = kb.reshape(B, T, H_B, HEAD_DIM)
    vb = vb.reshape(B, T, H_B, HEAD_DIM)
    logf = jax.nn.log_sigmoid(fb.astype(F32) + b_f.astype(F32))
    return qa, ka, va, qi, ki, wi, qb, kb, vb, logf


def index_select(qi, wi, ki, qpos, kpos, k_top):
    s = jnp.einsum('bthd,bsd->bths', qi.astype(F32), ki.astype(F32)) * (D_IDX ** -0.5)
    score = jnp.einsum('bths,bth->bts', jax.nn.relu(s), wi.astype(F32)) * (H_IDX ** -0.5)
    causal = kpos[None, None, :] <= qpos[None, :, None]
    score = jnp.where(causal, score, -jnp.inf)
    _, idx = lax.top_k(score, k_top)
    valid = idx <= qpos[None, :, None]
    return idx, valid


def sparse_attend(qa, kg, vg, valid):
    B, T = qa.shape[:2]
    q = qa.reshape(B, T, KV_A, H_A // KV_A, HEAD_DIM)
    logits = jnp.einsum('btjgd,btnjd->btjgn', q, kg).astype(F32) * (HEAD_DIM ** -0.5)
    logits = jnp.where(valid[:, :, None, None, :], logits, -jnp.inf)
    p = jax.nn.softmax(logits, axis=-1).astype(vg.dtype)
    o = jnp.einsum('btjgn,btnjd->btjgd', p, vg)
    return o.reshape(B, T, H_A * HEAD_DIM)


def fox_attend(qb, kb, vb, Fq, Fk, qpos, kpos):
    B, T = qb.shape[:2]
    logits = jnp.einsum('bthd,bshd->bhts', qb, kb).astype(F32) * (HEAD_DIM ** -0.5)
    logits = logits + jnp.transpose(Fq, (0, 2, 1))[..., None] - jnp.transpose(Fk, (0, 2, 1))[:, :, None, :]
    causal = kpos[None, :] <= qpos[:, None]
    logits = jnp.where(causal[None, None], logits, -jnp.inf)
    p = jax.nn.softmax(logits, axis=-1).astype(vb.dtype)
    o = jnp.einsum('bhts,bshd->bthd', p, vb)
    return o.reshape(B, T, H_B * HEAD_DIM)


def to_blocks(a, nb):
    return jnp.swapaxes(a.reshape(a.shape[0], nb, QBLOCK, *a.shape[2:]), 0, 1)


def from_blocks(a):
    a = jnp.swapaxes(a, 0, 1)
    return a.reshape(a.shape[0], a.shape[1] * a.shape[2], *a.shape[3:])


def take_rows(rows, idx):
    return jax.vmap(lambda r, i: r[i])(rows, idx)


def gather_pages(pool, page_table):
    g = pool[page_table]
    return g.reshape(g.shape[0], g.shape[1] * g.shape[2], *g.shape[3:])


def gather_selected(pool, page_table, new_rows, idx, past):
    P, PS = pool.shape[:2]
    flat = pool.reshape(P * PS, *pool.shape[2:])
    DB, T, K = idx.shape
    in_past = idx < past
    pidx = jnp.minimum(idx, past - 1)
    phys = jnp.take_along_axis(page_table, (pidx // PS).reshape(DB, T * K), axis=1).reshape(DB, T, K)
    from_past = flat[phys * PS + pidx % PS]
    from_new = take_rows(new_rows, jnp.clip(idx - past, 0, new_rows.shape[1] - 1))
    mask = in_past.reshape(in_past.shape + (1,) * (from_past.ndim - 3))
    return jnp.where(mask, from_past, from_new)


def even_mixer_prompt(h, w_in, b_f, w_out):
    B, S, _ = h.shape
    pos_i = jnp.arange(S, dtype=jnp.int32)
    qa, ka, va, qi, ki, wi, qb, kb, vb, logf = even_projections(h, w_in, b_f, pos_i.astype(F32))
    F = jnp.cumsum(logf, axis=1)
    nb = S // QBLOCK
    k_top = min(TOPK_MAX, S // 4)
    qpos_blocks = pos_i.reshape(nb, QBLOCK)

    def a_block(args):
        q_blk, qi_blk, wi_blk, qpos = args
        idx, valid = index_select(qi_blk, wi_blk, ki, qpos, pos_i, k_top)
        return sparse_attend(q_blk, take_rows(ka, idx), take_rows(va, idx), valid)

    def b_block(args):
        q_blk, fq_blk, qpos = args
        return fox_attend(q_blk, kb, vb, fq_blk, F, qpos, pos_i)

    out_a = from_blocks(lax.map(a_block, (to_blocks(qa, nb), to_blocks(qi, nb), to_blocks(wi, nb), qpos_blocks)))
    out_b = from_blocks(lax.map(b_block, (to_blocks(qb, nb), to_blocks(F, nb), qpos_blocks)))
    y = jnp.concatenate([out_a, out_b], axis=-1) @ w_out
    return y, (ka, va, ki, kb, vb, logf)


def even_mixer_sample(h, w_in, b_f, w_out, pool_ka, pool_va, pool_ki, pool_kb, pool_vb, pool_lf, page_table):
    DB, T, _ = h.shape
    past = page_table.shape[1] * PAGE_SIZE
    L = past + T
    kpos = jnp.arange(L, dtype=jnp.int32)
    qpos = past + jnp.arange(T, dtype=jnp.int32)
    qa, ka, va, qi, ki, wi, qb, kb, vb, logf = even_projections(h, w_in, b_f, qpos.astype(F32))
    k_top = min(TOPK_MAX, L // 4)
    ki_all = jnp.concatenate([gather_pages(pool_ki, page_table), ki], axis=1)
    idx, valid = index_select(qi, wi, ki_all, qpos, kpos, k_top)
    kg = gather_selected(pool_ka, page_table, ka, idx, past)
    vg = gather_selected(pool_va, page_table, va, idx, past)
    out_a = sparse_attend(qa, kg, vg, valid)
    kb_all = jnp.concatenate([gather_pages(pool_kb, page_table), kb], axis=1)
    vb_all = jnp.concatenate([gather_pages(pool_vb, page_table), vb], axis=1)
    lf_all = jnp.concatenate([gather_pages(pool_lf, page_table).astype(F32), logf], axis=1)
    F = jnp.cumsum(lf_all, axis=1)
    out_b = fox_attend(qb, kb_all, vb_all, F[:, past:], F, qpos, kpos)
    y = jnp.concatenate([out_a, out_b], axis=-1) @ w_out
    return y, (ka, va, ki, kb, vb, logf)


def conv_module(h, buf, w_pw1, b_pw1, w_dw, b_dw, ln_g, ln_b, w_pw2, b_pw2):
    a, g = jnp.split(h @ w_pw1 + b_pw1, 2, axis=-1)
    u = a * jax.nn.sigmoid(g)
    full = jnp.concatenate([buf.astype(u.dtype), u], axis=1)
    z = lax.conv_general_dilated(full, w_dw[:, None, :], window_strides=(1,), padding='VALID',
                                 dimension_numbers=('NWC', 'WIO', 'NWC'), feature_group_count=D_CONV)
    z = z + b_dw
    z32 = z.astype(F32)
    mu = jnp.mean(z32, axis=-1, keepdims=True)
    var = jnp.mean(jnp.square(z32 - mu), axis=-1, keepdims=True)
    zn = ((z32 - mu) * lax.rsqrt(var + EPS) * ln_g.astype(F32) + ln_b.astype(F32)).astype(z.dtype)
    out = jax.nn.silu(zn) @ w_pw2 + b_pw2
    return out, full[:, full.shape[1] - (CONV_WIDTH - 1):]


def swiglu(h, w_in, w_out):
    g, u = jnp.split(h @ w_in, 2, axis=-1)
    return (jax.nn.silu(g) * u) @ w_out


def run_trunk(x, c, even_fn, odd_fn, w_ada, b_ada, norm_mix, norm_ffn, w_ffn_in, w_ffn_out, norm_final):
    att_states = []
    conv_states = []
    for i in range(DEPTH):
        mod = jax.nn.silu(c) @ w_ada[i] + b_ada[i]
        sh1, sc1, g1, sh2, sc2, g2 = jnp.split(mod, 6, axis=-1)
        h = modulate(x, norm_mix[i], sh1, sc1)
        if i % 2 == 0:
            y, st = even_fn(i // 2, h)
            att_states.append(st)
        else:
            y, st = odd_fn(i // 2, h)
            conv_states.append(st)
        x = x + g1[:, None, :] * y
        h = modulate(x, norm_ffn[i], sh2, sc2)
        x = x + g2[:, None, :] * swiglu(h, w_ffn_in[i], w_ffn_out[i])
    att = [jnp.stack([s[j] for s in att_states]) for j in range(6)]
    conv = jnp.stack(conv_states)
    return rms_norm(x, norm_final), att, conv


def setup_inputs(seed: int = 0) -> dict:
    key = jax.random.key(seed)
    ks = iter(jax.random.split(key, 40))
    n_pages = PAST_LEN // PAGE_SIZE
    n_pool = (DEC_BATCH * n_pages * 5) // 4
    nrm = lambda shape, s=1.0: jax.random.normal(next(ks), shape, F32) * s
    perm = jax.random.permutation(next(ks), n_pool)[:DEC_BATCH * n_pages]
    page_table = perm.reshape(DEC_BATCH, n_pages).astype(jnp.int32)
    d = D_MODEL
    return {
        'x_prompt': nrm((BATCH, SEQ, d)),
        'x_sample': nrm((DEC_BATCH, DEC_SEQ, d)),
        'cache_a_k': nrm((N_ATT, n_pool, PAGE_SIZE, KV_A, HEAD_DIM)),
        'cache_a_v': nrm((N_ATT, n_pool, PAGE_SIZE, KV_A, HEAD_DIM)),
        'cache_a_idx_k': nrm((N_ATT, n_pool, PAGE_SIZE, D_IDX)),
        'cache_b_k': nrm((N_ATT, n_pool, PAGE_SIZE, H_B, HEAD_DIM)),
        'cache_b_v': nrm((N_ATT, n_pool, PAGE_SIZE, H_B, HEAD_DIM)),
        'cache_b_logf': jax.nn.log_sigmoid(nrm((N_ATT, n_pool, PAGE_SIZE, H_B)) + 2.0),
        'state_conv': nrm((N_CONV, DEC_BATCH, CONV_WIDTH - 1, D_CONV), 0.5),
        'page_table': page_table,
        'c_prompt': nrm((BATCH, d)),
        'c_sample': nrm((DEC_BATCH, d)),
        'w_in_att': nrm((N_ATT, d, D_IN_ATT), d ** -0.5),
        'b_fgate': jax.random.uniform(next(ks), (N_ATT, H_B), F32, 1.0, 3.0),
        'w_out_att': nrm((N_ATT, D_MIX, d), D_MIX ** -0.5),
        'w_pw1': nrm((N_CONV, d, 2 * D_CONV), d ** -0.5),
        'b_pw1': nrm((N_CONV, 2 * D_CONV), 0.01),
        'w_dw': nrm((N_CONV, CONV_WIDTH, D_CONV), CONV_WIDTH ** -0.5),
        'b_dw': nrm((N_CONV, D_CONV), 0.01),
        'ln_conv_g': 1.0 + nrm((N_CONV, D_CONV), 0.01),
        'ln_conv_b': nrm((N_CONV, D_CONV), 0.01),
        'w_pw2': nrm((N_CONV, D_CONV, d), D_CONV ** -0.5),
        'b_pw2': nrm((N_CONV, d), 0.01),
        'w_ada': nrm((DEPTH, d, 6 * d), 0.5 * d ** -0.5),
        'b_ada': nrm((DEPTH, 6 * d), 0.01),
        'norm_mix': 1.0 + nrm((DEPTH, d), 0.01),
        'norm_ffn': 1.0 + nrm((DEPTH, d), 0.01),
        'w_ffn_in': nrm((DEPTH, d, 2 * D_FF), d ** -0.5),
        'w_ffn_out': nrm((DEPTH, D_FF, d), D_FF ** -0.5),
        'norm_final': 1.0 + nrm((d,), 0.01),
    }


def reference(x_prompt, x_sample, cache_a_k, cache_a_v, cache_a_idx_k, cache_b_k, cache_b_v, cache_b_logf,
              state_conv, page_table, c_prompt, c_sample, w_in_att, b_fgate, w_out_att, w_pw1, b_pw1, w_dw,
              b_dw, ln_conv_g, ln_conv_b, w_pw2, b_pw2, w_ada, b_ada, norm_mix, norm_ffn, w_ffn_in,
              w_ffn_out, norm_final):
    def even_p(l, h):
        return even_mixer_prompt(h, w_in_att[l], b_fgate[l], w_out_att[l])

    def odd_p(l, h):
        buf = jnp.zeros((h.shape[0], CONV_WIDTH - 1, D_CONV), h.dtype)
        return conv_module(h, buf, w_pw1[l], b_pw1[l], w_dw[l], b_dw[l], ln_conv_g[l], ln_conv_b[l],
                           w_pw2[l], b_pw2[l])

    def even_s(l, h):
        return even_mixer_sample(h, w_in_att[l], b_fgate[l], w_out_att[l], cache_a_k[l], cache_a_v[l],
                                 cache_a_idx_k[l], cache_b_k[l], cache_b_v[l], cache_b_logf[l], page_table)

    def odd_s(l, h):
        return conv_module(h, state_conv[l], w_pw1[l], b_pw1[l], w_dw[l], b_dw[l], ln_conv_g[l],
                           ln_conv_b[l], w_pw2[l], b_pw2[l])

    y_prompt, att_p, conv_p = run_trunk(x_prompt, c_prompt, even_p, odd_p, w_ada, b_ada, norm_mix, norm_ffn,
                                        w_ffn_in, w_ffn_out, norm_final)
    y_sample, att_s, conv_s = run_trunk(x_sample, c_sample, even_s, odd_s, w_ada, b_ada, norm_mix, norm_ffn,
                                        w_ffn_in, w_ffn_out, norm_final)
    p_a_k, p_a_v, p_a_ik, p_b_k, p_b_v, p_b_lf = att_p
    s_a_k, s_a_v, s_a_ik, s_b_k, s_b_v, s_b_lf = att_s
    return (y_prompt, y_sample, p_a_k, p_a_v, p_a_ik, p_b_k, p_b_v, p_b_lf, conv_p,
            s_a_k, s_a_v, s_a_ik, s_b_k, s_b_v, s_b_lf, conv_s)
```

```python
import functools

import jax
import jax.numpy as jnp
from jax import lax
from jax.experimental import pallas as pl
from jax.experimental.pallas import tpu as pltpu

F32 = jnp.float32
I32 = jnp.int32
_MM = jnp.bfloat16

HEAD_DIM = 64
H_A = 8
KV_A = 2
H_IDX = 8
D_IDX = 64
H_B = 8
ROT_DIM = HEAD_DIM // 4
ROPE_THETA = 500000.0
TOPK_MAX = 256
CONV_WIDTH = 31
EPS = 1e-6
LANES = 128
INT_MIN = -(2 ** 31)
NEG_INF = float("-inf")
Q_SCALE = HEAD_DIM ** -0.5
VMEM_LIMIT = 56 * 1024 * 1024

_SPLITS = (H_A * HEAD_DIM, KV_A * HEAD_DIM, KV_A * HEAD_DIM, H_IDX * D_IDX, D_IDX, H_IDX,
           H_B * HEAD_DIM, H_B * HEAD_DIM, H_B * HEAD_DIM, H_B)
_NAMES = ("qa", "ka", "va", "qi", "ki", "wi", "qb", "kb", "vb", "fb")
_OFF = {}
_o = 0
for _n, _w in zip(_NAMES, _SPLITS):
    _OFF[_n] = (_o, _w)
    _o += _w
MISC_WI = D_IDX
MISC_FB = D_IDX + H_IDX

SDS = jax.ShapeDtypeStruct
BS = pl.BlockSpec


def _cp(*sem):
    return pltpu.CompilerParams(dimension_semantics=sem, vmem_limit_bytes=VMEM_LIMIT)


def _dot(a, b):
    return jnp.dot(a, b, preferred_element_type=F32)


def _dot_nt(a, b):
    return lax.dot_general(a, b, (((1,), (1,)), ((), ())), preferred_element_type=F32)


def _sigmoid(x):
    return 1.0 / (1.0 + jnp.exp(-x))


def _silu(x):
    return x * _sigmoid(x)


def _rms(x, g):
    return x * lax.rsqrt(jnp.mean(x * x, axis=-1, keepdims=True) + EPS) * g


def _modulate(x, g, shift, scale):
    return _rms(x, g) * (1.0 + scale) + shift


def _ada_kernel(c_ref, w_ref, b_ref, o_ref):
    a = _silu(c_ref[...]).astype(_MM)
    o_ref[0] = _dot(a, w_ref[0].astype(_MM)) + b_ref[0]


def _ada(c_all, w_ada, b_ada):
    depth, d, d6 = w_ada.shape
    r = c_all.shape[0]
    tn = d6 // 4
    return pl.pallas_call(
        _ada_kernel,
        out_shape=SDS((depth, r, d6), F32),
        grid=(depth, d6 // tn),
        in_specs=[BS((r, d), lambda l, j: (0, 0)),
                  BS((1, d, tn), lambda l, j: (l, 0, j)),
                  BS((1, 1, tn), lambda l, j: (l, 0, j))],
        out_specs=BS((1, r, tn), lambda l, j: (l, 0, j)),
        compiler_params=_cp("arbitrary", "arbitrary"),
        name="ada_mod",
    )(c_all, w_ada, b_ada.reshape(depth, 1, d6))


class _Group:
    def __init__(self, nb, t, d, mod, per_row):
        self.nb, self.t, self.d = nb, t, d
        self.n = nb * t
        self.mod = mod
        self.per_row = per_row

    def mod_arg(self, layer, tm, chunk, extra_grid=0):
        d = self.d
        if self.per_row:
            arr = self.mod[layer].reshape(self.n // tm, tm, 6 * d)
            imap = (lambda i, *_: (i, 0, chunk))
            return arr, BS((1, tm, d), imap)
        tpb = self.t // tm
        imap = (lambda i, *_: (i // tpb, 0, chunk))
        return self.mod[layer], BS((1, 1, d), imap)


def _rope_tables(pos):
    half = ROT_DIM // 2
    inv = ROPE_THETA ** (-jnp.arange(half, dtype=F32) * 2.0 / ROT_DIM)
    ang = pos[:, None] * inv[None, :]
    cos, sin = jnp.cos(ang), jnp.sin(ang)
    n = pos.shape[0]
    one = jnp.ones((n, HEAD_DIM - ROT_DIM), F32)
    zero = jnp.zeros((n, HEAD_DIM - ROT_DIM), F32)
    z8 = jnp.zeros((n, half), F32)
    c = jnp.concatenate([cos, cos, one], axis=1)
    s1 = jnp.concatenate([-sin, z8, zero], axis=1)
    s2 = jnp.concatenate([z8, sin, zero], axis=1)
    rep = LANES // HEAD_DIM
    return jnp.tile(c, (1, rep)), jnp.tile(s1, (1, rep)), jnp.tile(s2, (1, rep))


def _rope(y, c, s1, s2):
    w = y.shape[1]
    rep = w // LANES
    if rep > 1:
        c, s1, s2 = (jnp.concatenate([t] * rep, axis=1) for t in (c, s1, s2))
    half = ROT_DIM // 2
    return y * c + pltpu.roll(y, w - half, 1) * s1 + pltpu.roll(y, half, 1) * s2


def _log_sigmoid(x):
    return jnp.minimum(x, 0.0) - jnp.log(1.0 + jnp.exp(-jnp.abs(x)))


def _inproj_kernel(x_ref, g_ref, sh_ref, sc_ref, w_ref, c_ref, s1_ref, s2_ref, bf_ref, *out_refs, groups):
    h = _modulate(x_ref[...], g_ref[...], sh_ref[0], sc_ref[0]).astype(_MM)
    c, s1, s2 = c_ref[...], s1_ref[...], s2_ref[...]
    off = 0
    for (name, width, rope, scale), o_ref in zip(groups, out_refs):
        y = _dot(h, w_ref[:, off:off + width])
        off += width
        if name == "misc":
            lane = lax.broadcasted_iota(I32, y.shape, 1)
            yr = _rope(y, c, s1, s2)
            lf = _log_sigmoid(y + bf_ref[...])
            y = jnp.where(lane < MISC_WI, yr, jnp.where((lane >= MISC_FB) & (lane < MISC_FB + H_B), lf, y))
        elif rope:
            y = _rope(y, c, s1, s2)
        if scale != 1.0:
            y = y * scale
        o_ref[...] = y.astype(o_ref.dtype)


def _pack_w_in(w_in, slots):
    d = w_in.shape[0]

    def cols(name):
        o, w = _OFF[name]
        return w_in[:, o:o + w]

    def slot(name, place):
        src = cols(name).reshape(d, -1, HEAD_DIM)
        nh = src.shape[1]
        out = jnp.zeros((d, nh, 2, HEAD_DIM), w_in.dtype)
        for hh in range(nh):
            out = out.at[:, hh, place(hh), :].set(src[:, hh, :])
        return out.reshape(d, nh * LANES)

    misc = jnp.concatenate([cols("ki"), cols("wi"), cols("fb"),
                            jnp.zeros((d, LANES - D_IDX - H_IDX - H_B), w_in.dtype)], axis=1)
    if slots:
        qa = slot("qa", lambda hh: hh // (H_A // KV_A))
        qi = slot("qi", lambda hh: 0)
        qb = slot("qb", lambda hh: hh % 2)
    else:
        qa, qi, qb = cols("qa"), cols("qi"), cols("qb")
    parts = [qa, qi, qb, cols("kb"), cols("vb"), cols("ka"), cols("va"), misc]
    groups = (("qa", qa.shape[1], True, Q_SCALE), ("qi", qi.shape[1], True, D_IDX ** -0.5),
              ("qb", qb.shape[1], False, Q_SCALE), ("kb", H_B * HEAD_DIM, False, 1.0),
              ("vb", H_B * HEAD_DIM, False, 1.0), ("ka", KV_A * HEAD_DIM, True, 1.0),
              ("va", KV_A * HEAD_DIM, False, 1.0), ("misc", LANES, False, 1.0))
    return jnp.concatenate(parts, axis=1).astype(_MM), groups


def _inproj(grp, layer, x, norm_g, w_packed, groups, b_f, tables, tm):
    n, d = x.shape
    nt = n // tm
    sh_arr, sh_spec = grp.mod_arg(layer, tm, 0)
    sc_arr, sc_spec = grp.mod_arg(layer, tm, 1)
    c, s1, s2 = tables
    tr = c.shape[0] // tm
    tspec = BS((tm, LANES), lambda i: (i % tr, 0))
    bf = jnp.zeros((1, LANES), F32).at[0, MISC_FB:MISC_FB + H_B].set(b_f)
    out_dtypes = {"qa": _MM, "qi": _MM, "qb": _MM}
    out_shape = [SDS((n, w), out_dtypes.get(name, F32)) for name, w, _, _ in groups]
    out_specs = [BS((tm, w), lambda i: (i, 0)) for _, w, _, _ in groups]
    nc = w_packed.shape[1]
    return pl.pallas_call(
        functools.partial(_inproj_kernel, groups=groups),
        out_shape=out_shape,
        grid=(nt,),
        in_specs=[BS((tm, d), lambda i: (i, 0)), BS((1, d), lambda i: (0, 0)), sh_spec, sc_spec,
                  BS((d, nc), lambda i: (0, 0)), tspec, tspec, tspec, BS((1, LANES), lambda i: (0, 0))],
        out_specs=out_specs,
        compiler_params=_cp("arbitrary"),
        name="even_inproj",
    )(x, norm_g.reshape(1, d), sh_arr, sc_arr, w_packed, c, s1, s2, bf)


def _tri_lower(n):
    r = lax.broadcasted_iota(I32, (n, n), 0)
    c = lax.broadcasted_iota(I32, (n, n), 1)
    return jnp.where(r >= c, 1.0, 0.0).astype(F32)


def _dot_f32(a, b):
    return jnp.dot(a, b, preferred_element_type=F32, precision=lax.Precision.HIGHEST)


def _cumsum_p_kernel(lf_ref, f_ref, *, tc):
    t = lf_ref.shape[1]
    tri = _tri_lower(tc)
    carry = jnp.zeros((1, lf_ref.shape[2]), F32)
    for c in range(t // tc):
        fc = _dot_f32(tri, lf_ref[0, c * tc:(c + 1) * tc, :]) + carry
        f_ref[0, c * tc:(c + 1) * tc, :] = fc
        carry = fc[tc - 1:tc, :]


def _cumsum_prompt(logf):
    b, t, hb = logf.shape
    tc = min(256, t)
    return pl.pallas_call(
        functools.partial(_cumsum_p_kernel, tc=tc),
        out_shape=SDS((b, t, hb), F32),
        grid=(b,),
        in_specs=[BS((1, t, hb), lambda i: (i, 0, 0))],
        out_specs=BS((1, t, hb), lambda i: (i, 0, 0)),
        compiler_params=_cp("arbitrary"),
        name="cumsum_prompt",
    )(logf)


def _fox_p_kernel(q_ref, k_ref, v_ref, fq_ref, fk_ref, o_ref, m_sc, l_sc, acc_sc, *, tq, tk):
    i = pl.program_id(1)
    j = pl.program_id(2)
    nk = pl.num_programs(2)

    @pl.when(j == 0)
    def _():
        m_sc[...] = jnp.full(m_sc.shape, NEG_INF, F32)
        l_sc[...] = jnp.zeros(l_sc.shape, F32)
        acc_sc[...] = jnp.zeros(acc_sc.shape, F32)

    @pl.when(j * tk < (i + 1) * tq)
    def _():
        k = k_ref[0].astype(_MM)
        v = v_ref[0].astype(_MM)
        fq = fq_ref[0]
        fk = fk_ref[0]
        rows = i * tq + lax.broadcasted_iota(I32, (tq, tk), 0)
        cols = j * tk + lax.broadcasted_iota(I32, (tq, tk), 1)
        causal = cols <= rows
        lane = lax.broadcasted_iota(I32, (tq, LANES), 1)
        low = lane < HEAD_DIM
        for p in range(H_B // 2):
            kp = k[:, p * LANES:(p + 1) * LANES]
            vp = v[:, p * LANES:(p + 1) * LANES]
            alphas, pvs = [], []
            for e in range(2):
                hh = 2 * p + e
                s = _dot_nt(q_ref[0, :, hh * LANES:(hh + 1) * LANES], kp)
                s = s + fq[:, hh:hh + 1] - fk[hh:hh + 1, :]
                s = jnp.where(causal, s, NEG_INF)
                m_prev = m_sc[hh]
                m_new = jnp.maximum(m_prev, jnp.max(s, axis=1, keepdims=True))
                alpha = jnp.exp(m_prev - m_new)
                pe = jnp.exp(s - m_new)
                l_sc[hh] = alpha * l_sc[hh] + jnp.sum(pe, axis=1, keepdims=True)
                m_sc[hh] = m_new
                alphas.append(alpha)
                pvs.append(_dot(pe.astype(_MM), vp))
            acc_sc[p] = jnp.where(low, alphas[0], alphas[1]) * acc_sc[p] + jnp.where(low, pvs[0], pvs[1])

    @pl.when(j == nk - 1)
    def _():
        lane = lax.broadcasted_iota(I32, (tq, LANES), 1)
        low = lane < HEAD_DIM
        for p in range(H_B // 2):
            linv = jnp.where(low, 1.0 / l_sc[2 * p], 1.0 / l_sc[2 * p + 1])
            o_ref[0, :, p * LANES:(p + 1) * LANES] = (acc_sc[p] * linv).astype(o_ref.dtype)


def _fox_prompt(qb_s, kb, vb, f, ft, b, t):
    tq = tk = min(256, t)
    nq, nk = t // tq, t // tk
    dq = qb_s.shape[1]
    dk = kb.shape[1]
    kmap = lambda bb, i, j: (bb, jnp.minimum(j, ((i + 1) * tq - 1) // tk), 0)
    return pl.pallas_call(
        functools.partial(_fox_p_kernel, tq=tq, tk=tk),
        out_shape=SDS((b, t, dk), _MM),
        grid=(b, nq, nk),
        in_specs=[BS((1, tq, dq), lambda bb, i, j: (bb, i, 0)),
                  BS((1, tk, dk), kmap), BS((1, tk, dk), kmap),
                  BS((1, tq, H_B), lambda bb, i, j: (bb, i, 0)),
                  BS((1, H_B, tk), lambda bb, i, j: (bb, 0, jnp.minimum(j, ((i + 1) * tq - 1) // tk)))],
        out_specs=BS((1, tq, dk), lambda bb, i, j: (bb, i, 0)),
        scratch_shapes=[pltpu.VMEM((H_B, tq, 1), F32), pltpu.VMEM((H_B, tq, 1), F32),
                        pltpu.VMEM((H_B // 2, tq, LANES), F32)],
        compiler_params=_cp("arbitrary", "arbitrary", "arbitrary"),
        name="fox_prompt",
    )(qb_s.reshape(b, t, dq), kb.reshape(b, t, dk), vb.reshape(b, t, dk), f, ft)


def _order_key(score):
    bits = pltpu.bitcast(score + 0.0, I32)
    return jnp.where(bits < 0, bits ^ jnp.int32(0x7FFFFFFF), bits)


def _kth_largest(load, nchunks, k_f):
    rows = load(0).shape[0]

    def count_ge(cand):
        tot = None
        for c in range(nchunks):
            x = jnp.where(load(c) >= cand, 1.0, 0.0)
            tot = x if tot is None else tot + x
        return jnp.sum(tot, axis=1, keepdims=True)

    def body(it, tau):
        cand = tau ^ jnp.left_shift(jnp.int32(1), 31 - it)
        return jnp.where(count_ge(cand) >= k_f, cand, tau)

    tau = lax.fori_loop(0, 32, body, jnp.full((rows, 1), INT_MIN, I32))
    return tau, count_ge(tau)


def _select_with_ties(load, store, nchunks, width, tau, k_f):
    n_gt = None
    for c in range(nchunks):
        x = jnp.sum(jnp.where(load(c) > tau, 1.0, 0.0), axis=1, keepdims=True)
        n_gt = x if n_gt is None else n_gt + x
    need = k_f - n_gt
    r = lax.broadcasted_iota(I32, (width, width), 0)
    cc = lax.broadcasted_iota(I32, (width, width), 1)
    upper = jnp.where(r <= cc, 1.0, 0.0).astype(_MM)
    carry = jnp.zeros_like(need)
    for c in range(nchunks):
        key = load(c)
        eq = key == tau
        prefix = _dot(jnp.where(eq, 1.0, 0.0).astype(_MM), upper) + carry
        store(c, (key > tau) | (eq & (prefix <= need)))
        carry = prefix[:, width - 1:width]


def _dsa_p_kernel(qi_ref, qa_ref, mq_ref, mk_ref, ka_ref, va_ref, o_ref, key_sc, bias_sc,
                  *, tq, lk, q0, k_top, cw):
    i = pl.program_id(1)
    nch = lk // cw
    qpos = q0 + i * tq + lax.broadcasted_iota(I32, (tq, 1), 0)
    wi = mq_ref[0][:, MISC_WI:MISC_WI + H_IDX] * (H_IDX ** -0.5)
    for c in range(nch):
        kmat = mk_ref[0, c * cw:(c + 1) * cw, :].astype(_MM)
        score = jnp.zeros((tq, cw), F32)
        for hh in range(H_IDX):
            s = _dot_nt(qi_ref[0, :, hh * LANES:(hh + 1) * LANES], kmat)
            score = score + jnp.maximum(s, 0.0) * wi[:, hh:hh + 1]
        kpos = c * cw + lax.broadcasted_iota(I32, (tq, cw), 1)
        key_sc[:, c * cw:(c + 1) * cw] = jnp.where(kpos <= qpos, _order_key(score), INT_MIN)

    load = lambda c: key_sc[:, c * cw:(c + 1) * cw]
    k_f = jnp.minimum(k_top, qpos + 1).astype(F32)
    tau, cnt = _kth_largest(load, nch, k_f)
    has_tie = jnp.max(cnt - k_f) > 0.0

    def store(c, sel):
        bias_sc[:, c * cw:(c + 1) * cw] = jnp.where(sel, 0.0, NEG_INF)

    @pl.when(jnp.logical_not(has_tie))
    def _():
        for c in range(nch):
            store(c, load(c) >= tau)

    @pl.when(has_tie)
    def _():
        _select_with_ties(load, store, nch, cw, tau, k_f)

    ka = ka_ref[0].astype(_MM)
    va = va_ref[0].astype(_MM)
    bias = bias_sc[...]
    lane = lax.broadcasted_iota(I32, (tq, LANES), 1)
    for hh in range(H_A):
        grp = hh // (H_A // KV_A)
        lg = _dot_nt(qa_ref[0, :, hh * LANES:(hh + 1) * LANES], ka) + bias
        m = jnp.max(lg, axis=1, keepdims=True)
        pe = jnp.exp(lg - m)
        l = jnp.sum(pe, axis=1, keepdims=True)
        o = _dot(pe.astype(_MM), va) * (1.0 / l)
        o = jnp.where((lane >= grp * HEAD_DIM) & (lane < (grp + 1) * HEAD_DIM), o, 0.0)
        o_ref[0, :, hh * LANES:(hh + 1) * LANES] = o.astype(o_ref.dtype)


def _dsa_prompt(qi_s, qa_s, misc, ka, va, b, t, k_top):
    tq = min(128, t)
    nq = t // tq
    cw = min(512, t)
    dq = qi_s.shape[1]
    qmap = lambda bb, i: (bb, i, 0)
    kmap = lambda bb, i: (bb, 0, 0)
    return pl.pallas_call(
        functools.partial(_dsa_p_kernel, tq=tq, lk=t, q0=0, k_top=k_top, cw=cw),
        out_shape=SDS((b, t, dq), _MM),
        grid=(b, nq),
        in_specs=[BS((1, tq, dq), qmap), BS((1, tq, dq), qmap), BS((1, tq, LANES), qmap),
                  BS((1, t, LANES), kmap), BS((1, t, LANES), kmap), BS((1, t, LANES), kmap)],
        out_specs=BS((1, tq, dq), qmap),
        scratch_shapes=[pltpu.VMEM((tq, t), I32), pltpu.VMEM((tq, t), F32)],
        compiler_params=_cp("arbitrary", "arbitrary"),
        name="dsa_prompt",
    )(qi_s.reshape(b, t, dq), qa_s.reshape(b, t, dq), misc.reshape(b, t, LANES), misc.reshape(b, t, LANES),
      ka.reshape(b, t, LANES), va.reshape(b, t, LANES))


def _outproj_kernel(oa_ref, ob_ref, wa_ref, wb_ref, x_ref, gate_ref, o_ref):
    y = _dot(oa_ref[...], wa_ref[...]) + _dot(ob_ref[...], wb_ref[...])
    o_ref[...] = x_ref[...] + gate_ref[0] * y


def _outproj(grp, layer, oa, ob, wa, wb, x, tm):
    n, d = x.shape
    g_arr, g_spec = grp.mod_arg(layer, tm, 2)
    da, db = oa.shape[1], ob.shape[1]
    return pl.pallas_call(
        _outproj_kernel,
        out_shape=SDS((n, d), F32),
        grid=(n // tm,),
        in_specs=[BS((tm, da), lambda i: (i, 0)), BS((tm, db), lambda i: (i, 0)),
                  BS((da, d), lambda i: (0, 0)), BS((db, d), lambda i: (0, 0)),
                  BS((tm, d), lambda i: (i, 0)), g_spec],
        out_specs=BS((tm, d), lambda i: (i, 0)),
        compiler_params=_cp("arbitrary"),
        name="even_outproj",
    )(oa, ob, wa, wb, x, g_arr)


def _ffn_kernel(x_ref, g_ref, sh_ref, sc_ref, gate_ref, wg_ref, wu_ref, wo_ref, gf_ref, o_ref, h_sc, acc_sc,
                *, final):
    f = pl.program_id(1)

    @pl.when(f == 0)
    def _():
        h_sc[...] = _modulate(x_ref[...], g_ref[...], sh_ref[0], sc_ref[0]).astype(_MM)
        acc_sc[...] = jnp.zeros(acc_sc.shape, F32)

    h = h_sc[...]
    a = (_silu(_dot(h, wg_ref[...])) * _dot(h, wu_ref[...])).astype(_MM)
    acc_sc[...] += _dot(a, wo_ref[...])

    @pl.when(f == pl.num_programs(1) - 1)
    def _():
        y = x_ref[...] + gate_ref[0] * acc_sc[...]
        if final:
            y = _rms(y, gf_ref[...])
        o_ref[...] = y


def _ffn(grp, layer, x, norm_g, w_in, w_out, norm_final, final, tm, tf):
    n, d = x.shape
    ff = w_out.shape[0]
    nf = ff // tf
    sh_arr, sh_spec = grp.mod_arg(layer, tm, 3)
    sc_arr, sc_spec = grp.mod_arg(layer, tm, 4)
    g_arr, g_spec = grp.mod_arg(layer, tm, 5)
    return pl.pallas_call(
        functools.partial(_ffn_kernel, final=final),
        out_shape=SDS((n, d), F32),
        grid=(n // tm, nf),
        in_specs=[BS((tm, d), lambda i, f: (i, 0)), BS((1, d), lambda i, f: (0, 0)), sh_spec, sc_spec, g_spec,
                  BS((d, tf), lambda i, f: (0, f)), BS((d, tf), lambda i, f: (0, f + nf)),
                  BS((tf, d), lambda i, f: (f, 0)), BS((1, d), lambda i, f: (0, 0))],
        out_specs=BS((tm, d), lambda i, f: (i, 0)),
        scratch_shapes=[pltpu.VMEM((tm, d), _MM), pltpu.VMEM((tm, d), F32)],
        compiler_params=_cp("arbitrary", "arbitrary"),
        name="ffn",
    )(x, norm_g.reshape(1, d), sh_arr, sc_arr, g_arr, w_in, w_in, w_out, norm_final.reshape(1, d))


def _pw1_kernel(x_ref, g_ref, sh_ref, sc_ref, w_ref, b_ref, u_ref):
    h = _modulate(x_ref[...], g_ref[...], sh_ref[0], sc_ref[0]).astype(_MM)
    y = _dot(h, w_ref[...]) + b_ref[...]
    dc = y.shape[1] // 2
    u_ref[...] = y[:, :dc] * _sigmoid(y[:, dc:])


def _pw1(grp, layer, x, norm_g, w, bias, tm):
    n, d = x.shape
    dc2 = w.shape[1]
    sh_arr, sh_spec = grp.mod_arg(layer, tm, 0)
    sc_arr, sc_spec = grp.mod_arg(layer, tm, 1)
    return pl.pallas_call(
        _pw1_kernel,
        out_shape=SDS((n, dc2 // 2), F32),
        grid=(n // tm,),
        in_specs=[BS((tm, d), lambda i: (i, 0)), BS((1, d), lambda i: (0, 0)), sh_spec, sc_spec,
                  BS((d, dc2), lambda i: (0, 0)), BS((1, dc2), lambda i: (0, 0))],
        out_specs=BS((tm, dc2 // 2), lambda i: (i, 0)),
        compiler_params=_cp("arbitrary"),
        name="conv_pw1_glu",
    )(x, norm_g.reshape(1, d), sh_arr, sc_arr, w, bias.reshape(1, dc2))


def _ln_swish_pw2(z, lng, lnb, w2, b2):
    mu = jnp.mean(z, axis=-1, keepdims=True)
    zc = z - mu
    var = jnp.mean(zc * zc, axis=-1, keepdims=True)
    zn = zc * lax.rsqrt(var + EPS) * lng + lnb
    return _dot(_silu(zn).astype(_MM), w2) + b2


HALO = 32


def _conv_p_kernel(ucur_ref, uhalo_ref, wdw_ref, bdw_ref, lng_ref, lnb_ref, w2_ref, b2_ref, x_ref, gate_ref,
                   o_ref, full_sc, *, tm):
    i = pl.program_id(1)
    full_sc[0:HALO, :] = jnp.where(i > 0, uhalo_ref[0], 0.0)
    full_sc[HALO:HALO + tm, :] = ucur_ref[0]
    z = jnp.zeros((tm, full_sc.shape[1]), F32) + bdw_ref[...]
    base = HALO - (CONV_WIDTH - 1)
    for w in range(CONV_WIDTH):
        z = z + full_sc[base + w:base + w + tm, :] * wdw_ref[w:w + 1, :]
    y = _ln_swish_pw2(z, lng_ref[...], lnb_ref[...], w2_ref[...], b2_ref[...])
    o_ref[0] = x_ref[0] + gate_ref[0] * y


def _conv_prompt(grp, layer, u, x, wdw, bdw, lng, lnb, w2, b2, tm):
    b, t, d = grp.nb, grp.t, grp.d
    dc = u.shape[1]
    g_arr, _ = grp.mod_arg(layer, tm, 2)
    hb = tm // HALO
    wpad = jnp.zeros((HALO, dc), F32).at[:CONV_WIDTH].set(wdw)
    vec = lambda bb, i: (0, 0)
    return pl.pallas_call(
        functools.partial(_conv_p_kernel, tm=tm),
        out_shape=SDS((b, t, d), F32),
        grid=(b, t // tm),
        in_specs=[BS((1, tm, dc), lambda bb, i: (bb, i, 0)),
                  BS((1, HALO, dc), lambda bb, i: (bb, jnp.maximum(i * hb - 1, 0), 0)),
                  BS((HALO, dc), vec), BS((1, dc), vec), BS((1, dc), vec), BS((1, dc), vec),
                  BS((dc, d), vec), BS((1, d), vec),
                  BS((1, tm, d), lambda bb, i: (bb, i, 0)),
                  BS((1, 1, d), lambda bb, i: (bb, 0, 2))],
        out_specs=BS((1, tm, d), lambda bb, i: (bb, i, 0)),
        scratch_shapes=[pltpu.VMEM((HALO + tm, dc), F32)],
        compiler_params=_cp("arbitrary", "arbitrary"),
        name="conv_prompt",
    )(u.reshape(b, t, dc), u.reshape(b, t, dc), wpad, bdw.reshape(1, dc), lng.reshape(1, dc), lnb.reshape(1, dc),
      w2, b2.reshape(1, d), x.reshape(b, t, d), g_arr).reshape(b * t, d)


def _conv_s_kernel(buf_ref, u_ref, wdw_ref, bdw_ref, lng_ref, lnb_ref, w2_ref, b2_ref, x_ref, gate_ref, o_ref):
    nbuf = buf_ref.shape[0]
    t = u_ref.shape[0]
    zs = []
    for tt in range(t):
        z = jnp.zeros(u_ref.shape[1:], F32) + bdw_ref[...]
        for w in range(CONV_WIDTH):
            src = tt + w
            row = buf_ref[src] if src < nbuf else u_ref[src - nbuf]
            z = z + row * wdw_ref[w:w + 1, :]
        zs.append(z)
    y = _ln_swish_pw2(jnp.concatenate(zs, axis=0), lng_ref[...], lnb_ref[...], w2_ref[...], b2_ref[...])
    bb = u_ref.shape[1]
    for tt in range(t):
        o_ref[tt] = x_ref[tt] + gate_ref[...] * y[tt * bb:(tt + 1) * bb, :]


def _conv_sample(buf_t, u_t, x_t, gate, wdw, bdw, lng, lnb, w2, b2):
    nbuf, db, dc = buf_t.shape
    t, _, d = x_t.shape
    bb = min(32, db)
    wpad = jnp.zeros((HALO, dc), F32).at[:CONV_WIDTH].set(wdw)
    vec = lambda j: (0, 0)
    return pl.pallas_call(
        _conv_s_kernel,
        out_shape=SDS((t, db, d), F32),
        grid=(db // bb,),
        in_specs=[BS((nbuf, bb, dc), lambda j: (0, j, 0)), BS((t, bb, dc), lambda j: (0, j, 0)),
                  BS((HALO, dc), vec), BS((1, dc), vec), BS((1, dc), vec), BS((1, dc), vec),
                  BS((dc, d), vec), BS((1, d), vec),
                  BS((t, bb, d), lambda j: (0, j, 0)), BS((bb, d), lambda j: (j, 2))],
        out_specs=BS((t, bb, d), lambda j: (0, j, 0)),
        compiler_params=_cp("arbitrary"),
        name="conv_sample",
    )(buf_t, u_t, wpad, bdw.reshape(1, dc), lng.reshape(1, dc), lnb.reshape(1, dc), w2, b2.reshape(1, d), x_t, gate)


def _cumsum_s_kernel(pt_ref, lf_ref, lfn_ref, f_ref, fn_ref, carry_sc):
    p = pl.program_id(1)
    ps = lf_ref.shape[2]

    @pl.when(p == 0)
    def _():
        carry_sc[...] = jnp.zeros(carry_sc.shape, F32)

    fc = _dot_f32(_tri_lower(ps), lf_ref[0, 0].astype(F32)) + carry_sc[...]
    f_ref[0] = fc
    carry_sc[...] = fc[ps - 1:ps, :]

    @pl.when(p == pl.num_programs(1) - 1)
    def _():
        run = carry_sc[...]
        rows = []
        for tt in range(lfn_ref.shape[1]):
            run = run + lfn_ref[0, tt:tt + 1, :]
            rows.append(run)
        fn_ref[0] = jnp.concatenate(rows, axis=0)


def _cumsum_sample(pool_lf, layer, page_table, lf_new):
    db, npg = page_table.shape
    ps, hb = pool_lf.shape[2], pool_lf.shape[3]
    t = lf_new.shape[1]
    return pl.pallas_call(
        _cumsum_s_kernel,
        out_shape=[SDS((db, npg * ps, hb), F32), SDS((db, t, hb), F32)],
        grid_spec=pltpu.PrefetchScalarGridSpec(
            num_scalar_prefetch=1,
            grid=(db, npg),
            in_specs=[BS((1, 1, ps, hb), lambda b, p, pt: (layer, pt[b, p], 0, 0)),
                      BS((1, t, hb), lambda b, p, pt: (b, 0, 0))],
            out_specs=[BS((1, ps, hb), lambda b, p, pt: (b, p, 0)),
                       BS((1, t, hb), lambda b, p, pt: (b, 0, 0))],
            scratch_shapes=[pltpu.VMEM((1, hb), F32)]),
        compiler_params=_cp("arbitrary", "arbitrary"),
        name="cumsum_sample",
    )(page_table, pool_lf, lf_new)


def _online_update(lt, v2, m_sc, l_sc, acc_sc):
    m_prev = m_sc[...]
    m_new = jnp.maximum(m_prev, jnp.max(lt, axis=1, keepdims=True))
    m_safe = jnp.where(m_new == NEG_INF, 0.0, m_new)
    alpha = jnp.exp(m_prev - m_safe)
    pe = jnp.exp(lt - m_safe)
    l_sc[...] = alpha * l_sc[...] + jnp.sum(pe, axis=1, keepdims=True)
    acc_sc[...] = alpha * acc_sc[...] + _dot(pe.astype(_MM), v2)
    m_sc[...] = m_new


def _attn_s_kernel(pt_ref, qb_ref, qa_ref, qi_ref, wi_ref, fq_ref, kb_ref, vb_ref, ka_ref, va_ref, ki_ref,
                   fk_ref, kbn_ref, vbn_ref, kan_ref, van_ref, kin_ref, fkn_ref, oa_ref, ob_ref,
                   m_sc, l_sc, acc_sc, ka_sc, va_sc, key_sc, sel_sc, *, past, k_top, t):
    p = pl.program_id(1)
    npg = pl.num_programs(1)
    ps = ki_ref.shape[2]
    rows = H_B * t
    wb = ps * H_B
    wa = ps * KV_A

    @pl.when(p == 0)
    def _():
        m_sc[...] = jnp.full(m_sc.shape, NEG_INF, F32)
        l_sc[...] = jnp.zeros(l_sc.shape, F32)
        acc_sc[...] = jnp.zeros(acc_sc.shape, F32)

    k2 = kb_ref[0, 0].reshape(wb, HEAD_DIM).astype(_MM)
    v2 = vb_ref[0, 0].reshape(wb, HEAD_DIM).astype(_MM)
    lt = _dot_nt(qb_ref[0], k2) + fq_ref[0] - fk_ref[0]
    r_b = lax.broadcasted_iota(I32, (rows, wb), 0)
    c_b = lax.broadcasted_iota(I32, (rows, wb), 1)
    lt = jnp.where((c_b % H_B) == (r_b // t), lt, NEG_INF)
    _online_update(lt, v2, m_sc, l_sc, acc_sc)

    def dup_scores(ki2):
        s = _dot_nt(qi_ref[0], ki2)
        s = jnp.maximum(s, 0.0) * (wi_ref[0] * (H_IDX ** -0.5))
        sc = s[0:t]
        for hh in range(1, H_IDX):
            sc = sc + s[hh * t:(hh + 1) * t]
        return jnp.concatenate([sc] * (8 // t), axis=0)

    r_e = lax.broadcasted_iota(I32, (wa, ps), 0)
    c_e = lax.broadcasted_iota(I32, (wa, ps), 1)
    expand = jnp.where((r_e // KV_A) == c_e, 1.0, 0.0).astype(_MM)
    ki2 = _dot(expand, ki_ref[0, 0].astype(_MM)).astype(_MM)
    key_sc[p] = _order_key(dup_scores(ki2))
    ka_sc[p] = ka_ref[0, 0].reshape(wa, HEAD_DIM).astype(_MM)
    va_sc[p] = va_ref[0, 0].reshape(wa, HEAD_DIM).astype(_MM)

    @pl.when(p == npg - 1)
    def _():
        nn = kbn_ref.shape[1]
        ltn = _dot_nt(qb_ref[0], kbn_ref[0].astype(_MM)) + fq_ref[0] - fkn_ref[0]
        r_n = lax.broadcasted_iota(I32, (rows, nn), 0)
        c_n = lax.broadcasted_iota(I32, (rows, nn), 1)
        ok = ((c_n % H_B) == (r_n // t)) & ((c_n // H_B) <= (r_n % t))
        _online_update(jnp.where(ok, ltn, NEG_INF), vbn_ref[0].astype(_MM), m_sc, l_sc, acc_sc)
        ob_ref[0] = acc_sc[...] * (1.0 / l_sc[...])

        nch = key_sc.shape[0]
        r8 = lax.broadcasted_iota(I32, (8, wa), 0)
        c8 = lax.broadcasted_iota(I32, (8, wa), 1)
        ok_new = (c8 // KV_A) <= (r8 % t)
        key_new = _order_key(dup_scores(kin_ref[0].astype(_MM)))
        key_sc[nch - 1] = jnp.where(ok_new, key_new, INT_MIN)
        ka_sc[nch - 1] = kan_ref[0].astype(_MM)
        va_sc[nch - 1] = van_ref[0].astype(_MM)

        qpos = past + lax.broadcasted_iota(I32, (8, 1), 0) % t
        k_f = (KV_A * jnp.minimum(k_top, qpos + 1)).astype(F32)
        load = lambda c: key_sc[c]
        tau, cnt = _kth_largest(load, nch, k_f)
        has_tie = jnp.max(cnt - k_f) > 0.0

        def store(c, sel):
            sel_sc[c] = jnp.where(sel, 0.0, NEG_INF)

        @pl.when(jnp.logical_not(has_tie))
        def _():
            for c in range(nch):
                store(c, load(c) >= tau)

        @pl.when(has_tie)
        def _():
            _select_with_ties(load, store, nch, wa, tau, k_f)

        m_sc[...] = jnp.full(m_sc.shape, NEG_INF, F32)
        l_sc[...] = jnp.zeros(l_sc.shape, F32)
        acc_sc[...] = jnp.zeros(acc_sc.shape, F32)
        r_a = lax.broadcasted_iota(I32, (rows, wa), 0)
        c_a = lax.broadcasted_iota(I32, (rows, wa), 1)
        grp_ok = (c_a % KV_A) == (r_a // (t * (H_A // KV_A)))
        for c in range(nch):
            bias = jnp.concatenate([sel_sc[c]] * (rows // 8), axis=0)
            lg = _dot_nt(qa_ref[0], ka_sc[c]) + bias
            _online_update(jnp.where(grp_ok, lg, NEG_INF), va_sc[c], m_sc, l_sc, acc_sc)
        oa_ref[0] = acc_sc[...] * (1.0 / l_sc[...])


def _attn_sample(layer, page_table, caches, qb32, qa32, qi32, wi32, fq32, fk_row, kbn, vbn, kan, van, kin, fkn_row,
                 k_top, t):
    cache_a_k, cache_a_v, cache_a_idx_k, cache_b_k, cache_b_v = caches
    db, npg = page_table.shape
    ps = cache_a_idx_k.shape[2]
    rows = H_B * t
    wb, wa = ps * H_B, ps * KV_A
    past = npg * ps
    qspec = BS((1, rows, HEAD_DIM), lambda b, p, pt: (b, 0, 0))
    cspec = BS((1, rows, 1), lambda b, p, pt: (b, 0, 0))
    pool5 = lambda nh: BS((1, 1, ps, nh, HEAD_DIM), lambda b, p, pt: (layer, pt[b, p], 0, 0, 0))
    newspec = lambda r: BS((1, r, HEAD_DIM), lambda b, p, pt: (b, 0, 0))
    return pl.pallas_call(
        functools.partial(_attn_s_kernel, past=past, k_top=k_top, t=t),
        out_shape=[SDS((db, rows, HEAD_DIM), F32), SDS((db, rows, HEAD_DIM), F32)],
        grid_spec=pltpu.PrefetchScalarGridSpec(
            num_scalar_prefetch=1,
            grid=(db, npg),
            in_specs=[qspec, qspec, qspec, cspec, cspec,
                      pool5(H_B), pool5(H_B), pool5(KV_A), pool5(KV_A),
                      BS((1, 1, ps, D_IDX), lambda b, p, pt: (layer, pt[b, p], 0, 0)),
                      BS((1, 1, wb), lambda b, p, pt: (b, 0, p)),
                      newspec(rows), newspec(rows), newspec(wa), newspec(wa), newspec(wa),
                      BS((1, 1, rows), lambda b, p, pt: (b, 0, 0))],
            out_specs=[BS((1, rows, HEAD_DIM), lambda b, p, pt: (b, 0, 0)),
                       BS((1, rows, HEAD_DIM), lambda b, p, pt: (b, 0, 0))],
            scratch_shapes=[pltpu.VMEM((rows, 1), F32), pltpu.VMEM((rows, 1), F32),
                            pltpu.VMEM((rows, HEAD_DIM), F32),
                            pltpu.VMEM((npg + 1, wa, HEAD_DIM), _MM), pltpu.VMEM((npg + 1, wa, HEAD_DIM), _MM),
                            pltpu.VMEM((npg + 1, 8, wa), I32), pltpu.VMEM((npg + 1, 8, wa), F32)]),
        compiler_params=_cp("arbitrary", "arbitrary"),
        name="attn_sample",
    )(page_table, qb32, qa32, qi32, wi32, fq32, cache_b_k, cache_b_v, cache_a_k, cache_a_v, cache_a_idx_k,
      fk_row, kbn, vbn, kan, van, kin, fkn_row)


def _row_tile(n, pref):
    tm = min(pref, n)
    assert n % tm == 0, (n, tm)
    return tm


def _even_prompt(grp, layer, x, w, st):
    b, t = grp.nb, grp.t
    tm = _row_tile(t, 256)
    outs = _inproj(grp, layer, x, w["norm_mix"], w["w_in_slots"], w["groups_slots"], w["b_f"], st["rope"], tm)
    qa_s, qi_s, qb_s, kb, vb, ka, va, misc = outs
    logf = misc[:, MISC_FB:MISC_FB + H_B]
    f = _cumsum_prompt(logf.reshape(b, t, H_B))
    o_b = _fox_prompt(qb_s, kb, vb, f, jnp.swapaxes(f, 1, 2), b, t)
    o_a = _dsa_prompt(qi_s, qa_s, misc, ka, va, b, t, min(TOPK_MAX, t // 4))
    x = _outproj(grp, layer, o_a.reshape(b * t, -1), o_b.reshape(b * t, -1), w["w_out_a_slots"], w["w_out_b"], x,
                 _row_tile(grp.n, 512))
    return x, (ka, va, misc[:, :D_IDX], kb, vb, logf)


def _even_sample(grp, layer, x, w, st):
    db, t = grp.nb, grp.t
    n = grp.n
    tm = _row_tile(n, 256)
    outs = _inproj(grp, layer, x, w["norm_mix"], w["w_in_plain"], w["groups_plain"], w["b_f"], st["rope"], tm)
    qa, qi, qb, kb, vb, ka, va, misc = outs
    ki, wi, logf = misc[:, :D_IDX], misc[:, MISC_WI:MISC_WI + H_IDX], misc[:, MISC_FB:MISC_FB + H_B]
    page_table = st["page_table"]
    npg = page_table.shape[1]
    ps = st["caches"][2].shape[2]
    att = st["att_layer"]
    f_past, f_new = _cumsum_sample(st["cache_b_logf"], att, page_table, logf.reshape(db, t, H_B))

    def head_rows(a, nh):
        return a.reshape(db, t, nh, HEAD_DIM).transpose(0, 2, 1, 3).reshape(db, nh * t, HEAD_DIM)

    def pad_page(a, nh):
        a = a.reshape(db, t * nh, HEAD_DIM)
        return jnp.pad(a, ((0, 0), (0, ps * nh - t * nh), (0, 0)))

    wi32 = wi.reshape(db, t, H_IDX).transpose(0, 2, 1).reshape(db, H_IDX * t, 1)
    fq32 = f_new.transpose(0, 2, 1).reshape(db, H_B * t, 1)
    kin = jnp.repeat(ki.reshape(db, t, 1, D_IDX), KV_A, axis=2).reshape(db * t, KV_A * D_IDX)
    o_a, o_b = _attn_sample(
        att, page_table, st["caches"], head_rows(qb, H_B), head_rows(qa, H_A), head_rows(qi, H_IDX), wi32, fq32,
        f_past.reshape(db, 1, npg * ps * H_B), kb.reshape(db, t * H_B, HEAD_DIM), vb.reshape(db, t * H_B, HEAD_DIM),
        pad_page(ka, KV_A), pad_page(va, KV_A), pad_page(kin, KV_A), f_new.reshape(db, 1, t * H_B),
        min(TOPK_MAX, (npg * ps + t) // 4), t)

    def token_rows(o, nh):
        return o.reshape(db, nh, t, HEAD_DIM).transpose(0, 2, 1, 3).reshape(n, nh * HEAD_DIM).astype(_MM)

    x = _outproj(grp, layer, token_rows(o_a, H_A), token_rows(o_b, H_B), w["w_out_a"], w["w_out_b"], x,
                 _row_tile(n, 512))
    return x, (ka, va, ki, kb, vb, logf)


def _odd_prompt(grp, layer, x, w, st):
    tm = _row_tile(grp.t, 256)
    u = _pw1(grp, layer, x, w["norm_mix"], w["w_pw1"], w["b_pw1"], _row_tile(grp.n, 256))
    x = _conv_prompt(grp, layer, u, x, w["w_dw"], w["b_dw"], w["ln_g"], w["ln_b"], w["w_pw2"], w["b_pw2"], tm)
    nb, t = grp.nb, grp.t
    state = u.reshape(nb, t, -1)[:, t - (CONV_WIDTH - 1):]
    return x, state


def _odd_sample(grp, layer, x, w, st):
    db, t, d = grp.nb, grp.t, grp.d
    u = _pw1(grp, layer, x, w["norm_mix"], w["w_pw1"], w["b_pw1"], _row_tile(grp.n, 256))
    buf = st["state_conv"][st["conv_layer"]]
    u3 = u.reshape(db, t, -1)
    x_t = _conv_sample(jnp.swapaxes(buf, 0, 1), jnp.swapaxes(u3, 0, 1), jnp.swapaxes(x.reshape(db, t, d), 0, 1),
                       st["mod_batch"][layer], w["w_dw"], w["b_dw"], w["ln_g"], w["ln_b"], w["w_pw2"], w["b_pw2"])
    x = jnp.swapaxes(x_t, 0, 1).reshape(db * t, d)
    state = jnp.concatenate([buf.astype(F32), u3], axis=1)[:, t:]
    return x, state


def _trunk(grp, x, even_fn, odd_fn, layer_w, st, norm_final):
    depth = len(layer_w)
    att, conv = [], []
    for i in range(depth):
        w = layer_w[i]
        if i % 2 == 0:
            x, s = even_fn(grp, i, x, w, dict(st, att_layer=i // 2))
            att.append(s)
        else:
            x, s = odd_fn(grp, i, x, w, dict(st, conv_layer=i // 2))
            conv.append(s)
        tm = _row_tile(grp.n, 512)
        x = _ffn(grp, i, x, w["norm_ffn"], w["w_ffn_in"], w["w_ffn_out"], norm_final, i == depth - 1, tm,
                 w["tf"])
    return x, att, conv


def kernel(x_prompt, x_sample, cache_a_k, cache_a_v, cache_a_idx_k, cache_b_k, cache_b_v, cache_b_logf,
           state_conv, page_table, c_prompt, c_sample, w_in_att, b_fgate, w_out_att, w_pw1, b_pw1, w_dw,
           b_dw, ln_conv_g, ln_conv_b, w_pw2, b_pw2, w_ada, b_ada, norm_mix, norm_ffn, w_ffn_in,
           w_ffn_out, norm_final):
    bp, sp, d = x_prompt.shape
    db, ts, _ = x_sample.shape
    depth = w_ada.shape[0]
    npg, ps = page_table.shape[1], cache_a_idx_k.shape[2]
    past = npg * ps

    mod = _ada(jnp.concatenate([c_prompt, c_sample], axis=0), w_ada, b_ada)
    mod_p = mod[:, :bp].reshape(depth, bp, 1, 6 * d)
    mod_b = mod[:, bp:]
    mod_s = jnp.broadcast_to(mod_b[:, :, None, :], (depth, db, ts, 6 * d)).reshape(depth, db * ts, 6 * d)
    grp_p = _Group(bp, sp, d, mod_p, per_row=False)
    grp_s = _Group(db, ts, d, mod_s, per_row=True)

    ff = w_ffn_out.shape[1]
    tf = 256 if ff % 256 == 0 else ff
    half_a = H_A * HEAD_DIM
    layer_w = []
    for i in range(depth):
        w = {"norm_mix": norm_mix[i], "norm_ffn": norm_ffn[i], "w_ffn_in": w_ffn_in[i].astype(_MM),
             "w_ffn_out": w_ffn_out[i].astype(_MM), "tf": tf}
        l = i // 2
        if i % 2 == 0:
            w["w_in_slots"], w["groups_slots"] = _pack_w_in(w_in_att[l], True)
            w["w_in_plain"], w["groups_plain"] = _pack_w_in(w_in_att[l], False)
            w["b_f"] = b_fgate[l]
            wo = w_out_att[l]
            wa = wo[:half_a].reshape(H_A, HEAD_DIM, d)
            wa_slots = jnp.zeros((H_A, 2, HEAD_DIM, d), wo.dtype)
            for hh in range(H_A):
                wa_slots = wa_slots.at[hh, hh // (H_A // KV_A)].set(wa[hh])
            w["w_out_a_slots"] = wa_slots.reshape(H_A * LANES, d).astype(_MM)
            w["w_out_a"] = wo[:half_a].astype(_MM)
            w["w_out_b"] = wo[half_a:].astype(_MM)
        else:
            w.update(w_pw1=w_pw1[l].astype(_MM), b_pw1=b_pw1[l], w_dw=w_dw[l], b_dw=b_dw[l], ln_g=ln_conv_g[l],
                     ln_b=ln_conv_b[l], w_pw2=w_pw2[l].astype(_MM), b_pw2=b_pw2[l])
        layer_w.append(w)

    st_p = {"rope": _rope_tables(jnp.arange(sp, dtype=I32).astype(F32))}
    tm_s = _row_tile(db * ts, 256)
    pos_s = (past + (jnp.arange(tm_s, dtype=I32) % ts)).astype(F32)
    st_s = {"rope": _rope_tables(pos_s), "page_table": page_table, "cache_b_logf": cache_b_logf,
            "caches": (cache_a_k, cache_a_v, cache_a_idx_k, cache_b_k, cache_b_v), "state_conv": state_conv,
            "mod_batch": mod_b}

    y_p, att_p, conv_p = _trunk(grp_p, x_prompt.reshape(bp * sp, d), _even_prompt, _odd_prompt, layer_w, st_p,
                                norm_final)
    y_s, att_s, conv_s = _trunk(grp_s, x_sample.reshape(db * ts, d), _even_sample, _odd_sample, layer_w, st_s,
                                norm_final)

    def stack_att(att, nb, t):
        shapes = ((KV_A, HEAD_DIM), (KV_A, HEAD_DIM), (D_IDX,), (H_B, HEAD_DIM), (H_B, HEAD_DIM), (H_B,))
        return [jnp.stack([s[j].reshape(nb, t, *shapes[j]) for s in att]) for j in range(6)]

    out_p = stack_att(att_p, bp, sp)
    out_s = stack_att(att_s, db, ts)
    return (y_p.reshape(bp, sp, d), y_s.reshape(db, ts, d), *out_p, jnp.stack(conv_p),
            *out_s, jnp.stack(conv_s))
```

```python
import functools

import jax
import jax.numpy as jnp
from jax import lax
from jax.experimental import pallas as pl
from jax.experimental.pallas import tpu as pltpu

F32 = jnp.float32
I32 = jnp.int32
_MM = jnp.bfloat16

HEAD_DIM = 64
H_A = 8
KV_A = 2
H_IDX = 8
D_IDX = 64
H_B = 8
ROT_DIM = HEAD_DIM // 4
ROPE_THETA = 500000.0
TOPK_MAX = 256
CONV_WIDTH = 31
EPS = 1e-6
LANES = 128
INT_MIN = -(2 ** 31)
NEG_INF = float("-inf")
Q_SCALE = HEAD_DIM ** -0.5
VMEM_LIMIT = 56 * 1024 * 1024

_SPLITS = (H_A * HEAD_DIM, KV_A * HEAD_DIM, KV_A * HEAD_DIM, H_IDX * D_IDX, D_IDX, H_IDX,
           H_B * HEAD_DIM, H_B * HEAD_DIM, H_B * HEAD_DIM, H_B)
_NAMES = ("qa", "ka", "va", "qi", "ki", "wi", "qb", "kb", "vb", "fb")
_OFF = {}
_o = 0
for _n, _w in zip(_NAMES, _SPLITS):
    _OFF[_n] = (_o, _w)
    _o += _w
MISC_WI = D_IDX
MISC_FB = D_IDX + H_IDX
MISC_CQ = MISC_FB + H_B

SDS = jax.ShapeDtypeStruct
BS = pl.BlockSpec


def _cp(*sem):
    return pltpu.CompilerParams(dimension_semantics=sem, vmem_limit_bytes=VMEM_LIMIT)


def _dot(a, b):
    return jnp.dot(a, b, preferred_element_type=F32)


def _dot_nt(a, b):
    return lax.dot_general(a, b, (((1,), (1,)), ((), ())), preferred_element_type=F32)


def _sigmoid(x):
    return 1.0 / (1.0 + jnp.exp(-x))


def _silu(x):
    return x * _sigmoid(x)


def _rms(x, g):
    return x * lax.rsqrt(jnp.mean(x * x, axis=-1, keepdims=True) + EPS) * g


def _modulate(x, g, shift, scale):
    return _rms(x, g) * (1.0 + scale) + shift


def _ada_kernel(c_ref, w_ref, b_ref, o_ref):
    a = _silu(c_ref[...]).astype(_MM)
    o_ref[0] = _dot(a, w_ref[0].astype(_MM)) + b_ref[0]


def _ada(c_all, w_ada, b_ada):
    depth, d, d6 = w_ada.shape
    r = c_all.shape[0]
    tn = d6 // 4
    return pl.pallas_call(
        _ada_kernel,
        out_shape=SDS((depth, r, d6), F32),
        grid=(depth, d6 // tn),
        in_specs=[BS((r, d), lambda l, j: (0, 0)),
                  BS((1, d, tn), lambda l, j: (l, 0, j)),
                  BS((1, 1, tn), lambda l, j: (l, 0, j))],
        out_specs=BS((1, r, tn), lambda l, j: (l, 0, j)),
        compiler_params=_cp("arbitrary", "arbitrary"),
        name="ada_mod",
    )(c_all, w_ada, b_ada.reshape(depth, 1, d6))


class _Group:
    def __init__(self, nb, t, d, mod, per_row):
        self.nb, self.t, self.d = nb, t, d
        self.n = nb * t
        self.mod = mod
        self.per_row = per_row

    def mod_arg(self, layer, tm, chunk, extra_grid=0):
        d = self.d
        if self.per_row:
            arr = self.mod[layer].reshape(self.n // tm, tm, 6 * d)
            imap = (lambda i, *_: (i, 0, chunk))
            return arr, BS((1, tm, d), imap)
        tpb = self.t // tm
        imap = (lambda i, *_: (i // tpb, 0, chunk))
        return self.mod[layer], BS((1, 1, d), imap)


def _rope_tables(pos):
    half = ROT_DIM // 2
    inv = ROPE_THETA ** (-jnp.arange(half, dtype=F32) * 2.0 / ROT_DIM)
    ang = pos[:, None] * inv[None, :]
    cos, sin = jnp.cos(ang), jnp.sin(ang)
    n = pos.shape[0]
    one = jnp.ones((n, HEAD_DIM - ROT_DIM), F32)
    zero = jnp.zeros((n, HEAD_DIM - ROT_DIM), F32)
    z8 = jnp.zeros((n, half), F32)
    c = jnp.concatenate([cos, cos, one], axis=1)
    s1 = jnp.concatenate([-sin, z8, zero], axis=1)
    s2 = jnp.concatenate([z8, sin, zero], axis=1)
    rep = LANES // HEAD_DIM
    return jnp.tile(c, (1, rep)), jnp.tile(s1, (1, rep)), jnp.tile(s2, (1, rep))


def _rope(y, c, s1, s2):
    w = y.shape[1]
    rep = w // LANES
    if rep > 1:
        c, s1, s2 = (jnp.concatenate([t] * rep, axis=1) for t in (c, s1, s2))
    half = ROT_DIM // 2
    return y * c + pltpu.roll(y, w - half, 1) * s1 + pltpu.roll(y, half, 1) * s2


def _log_sigmoid(x):
    return jnp.minimum(x, 0.0) - jnp.log(1.0 + jnp.exp(-jnp.abs(x)))


def _inproj_kernel(x_ref, g_ref, sh_ref, sc_ref, w_ref, c_ref, s1_ref, s2_ref, bf_ref, *out_refs, groups, cum_t):
    h = _modulate(x_ref[...], g_ref[...], sh_ref[0], sc_ref[0]).astype(_MM)
    c, s1, s2 = c_ref[...], s1_ref[...], s2_ref[...]
    off = 0
    for (name, width, rope, scale), o_ref in zip(groups, out_refs):
        y = _dot(h, w_ref[:, off:off + width])
        off += width
        if name == "misc":
            lane = lax.broadcasted_iota(I32, y.shape, 1)
            yr = _rope(y, c, s1, s2)
            lf = _log_sigmoid(y + bf_ref[...])
            in_fb = (lane >= MISC_FB) & (lane < MISC_FB + H_B)
            if cum_t:
                tok = lax.broadcasted_iota(I32, y.shape, 0) % cum_t
                cum = lf
                for k in range(1, cum_t):
                    cum = cum + jnp.where(tok >= k, pltpu.roll(lf, k, 0), 0.0)
                y = jnp.where((lane >= MISC_CQ) & (lane < MISC_CQ + H_B), pltpu.roll(cum, H_B, 1), y)
            y = jnp.where(lane < MISC_WI, yr, jnp.where(in_fb, lf, y))
        elif rope:
            y = _rope(y, c, s1, s2)
        if scale != 1.0:
            y = y * scale
        o_ref[...] = y.astype(o_ref.dtype)


def _pack_w_in(w_in, slots):
    d = w_in.shape[0]

    def cols(name):
        o, w = _OFF[name]
        return w_in[:, o:o + w]

    def slot(name, place):
        src = cols(name).reshape(d, -1, HEAD_DIM)
        nh = src.shape[1]
        out = jnp.zeros((d, nh, 2, HEAD_DIM), w_in.dtype)
        for hh in range(nh):
            out = out.at[:, hh, place(hh), :].set(src[:, hh, :])
        return out.reshape(d, nh * LANES)

    misc = jnp.concatenate([cols("ki"), cols("wi"), cols("fb"),
                            jnp.zeros((d, LANES - D_IDX - H_IDX - H_B), w_in.dtype)], axis=1)
    if slots:
        qa = slot("qa", lambda hh: hh // (H_A // KV_A))
        qi = slot("qi", lambda hh: 0)
        qb = slot("qb", lambda hh: hh % 2)
    else:
        qa, qi, qb = cols("qa"), cols("qi"), cols("qb")
    parts = [qa, qi, qb, cols("kb"), cols("vb"), cols("ka"), cols("va"), misc]
    groups = (("qa", qa.shape[1], True, Q_SCALE), ("qi", qi.shape[1], True, D_IDX ** -0.5),
              ("qb", qb.shape[1], False, Q_SCALE), ("kb", H_B * HEAD_DIM, False, 1.0),
              ("vb", H_B * HEAD_DIM, False, 1.0), ("ka", KV_A * HEAD_DIM, True, 1.0),
              ("va", KV_A * HEAD_DIM, False, 1.0), ("misc", LANES, False, 1.0))
    return jnp.concatenate(parts, axis=1).astype(_MM), groups


def _inproj(grp, layer, x, norm_g, w_packed, groups, b_f, tables, tm, cum_t=0):
    n, d = x.shape
    nt = n // tm
    sh_arr, sh_spec = grp.mod_arg(layer, tm, 0)
    sc_arr, sc_spec = grp.mod_arg(layer, tm, 1)
    c, s1, s2 = tables
    tr = c.shape[0] // tm
    tspec = BS((tm, LANES), lambda i: (i % tr, 0))
    bf = jnp.zeros((1, LANES), F32).at[0, MISC_FB:MISC_FB + H_B].set(b_f)
    out_dtypes = {"qa": _MM, "qi": _MM, "qb": _MM}
    out_shape = [SDS((n, w), out_dtypes.get(name, F32)) for name, w, _, _ in groups]
    out_specs = [BS((tm, w), lambda i: (i, 0)) for _, w, _, _ in groups]
    nc = w_packed.shape[1]
    return pl.pallas_call(
        functools.partial(_inproj_kernel, groups=groups, cum_t=cum_t),
        out_shape=out_shape,
        grid=(nt,),
        in_specs=[BS((tm, d), lambda i: (i, 0)), BS((1, d), lambda i: (0, 0)), sh_spec, sc_spec,
                  BS((d, nc), lambda i: (0, 0)), tspec, tspec, tspec, BS((1, LANES), lambda i: (0, 0))],
        out_specs=out_specs,
        compiler_params=_cp("arbitrary"),
        name="even_inproj",
    )(x, norm_g.reshape(1, d), sh_arr, sc_arr, w_packed, c, s1, s2, bf)


def _tri_lower(n):
    r = lax.broadcasted_iota(I32, (n, n), 0)
    c = lax.broadcasted_iota(I32, (n, n), 1)
    return jnp.where(r >= c, 1.0, 0.0).astype(F32)


def _dot_f32(a, b):
    return jnp.dot(a, b, preferred_element_type=F32, precision=lax.Precision.HIGHEST)


def _cumsum_p_kernel(lf_ref, f_ref, *, tc):
    t = lf_ref.shape[1]
    tri = _tri_lower(tc)
    carry = jnp.zeros((1, lf_ref.shape[2]), F32)
    for c in range(t // tc):
        fc = _dot_f32(tri, lf_ref[0, c * tc:(c + 1) * tc, :]) + carry
        f_ref[0, c * tc:(c + 1) * tc, :] = fc
        carry = fc[tc - 1:tc, :]


def _cumsum_prompt(logf):
    b, t, hb = logf.shape
    tc = min(256, t)
    return pl.pallas_call(
        functools.partial(_cumsum_p_kernel, tc=tc),
        out_shape=SDS((b, t, hb), F32),
        grid=(b,),
        in_specs=[BS((1, t, hb), lambda i: (i, 0, 0))],
        out_specs=BS((1, t, hb), lambda i: (i, 0, 0)),
        compiler_params=_cp("arbitrary"),
        name="cumsum_prompt",
    )(logf)


def _fox_p_kernel(q_ref, k_ref, v_ref, fq_ref, fk_ref, o_ref, m_sc, l_sc, acc_sc, *, tq, tk):
    i = pl.program_id(1)
    j = pl.program_id(2)
    nk = pl.num_programs(2)

    @pl.when(j == 0)
    def _():
        m_sc[...] = jnp.full(m_sc.shape, NEG_INF, F32)
        l_sc[...] = jnp.zeros(l_sc.shape, F32)
        acc_sc[...] = jnp.zeros(acc_sc.shape, F32)

    @pl.when(j * tk < (i + 1) * tq)
    def _():
        k = k_ref[0].astype(_MM)
        v = v_ref[0].astype(_MM)
        fq = fq_ref[0]
        fk = fk_ref[0]
        rows = i * tq + lax.broadcasted_iota(I32, (tq, tk), 0)
        cols = j * tk + lax.broadcasted_iota(I32, (tq, tk), 1)
        causal = cols <= rows
        lane = lax.broadcasted_iota(I32, (tq, LANES), 1)
        low = lane < HEAD_DIM
        for p in range(H_B // 2):
            kp = k[:, p * LANES:(p + 1) * LANES]
            vp = v[:, p * LANES:(p + 1) * LANES]
            alphas, pvs = [], []
            for e in range(2):
                hh = 2 * p + e
                s = _dot_nt(q_ref[0, :, hh * LANES:(hh + 1) * LANES], kp)
                s = s + fq[:, hh:hh + 1] - fk[hh:hh + 1, :]
                s = jnp.where(causal, s, NEG_INF)
                m_prev = m_sc[hh]
                m_new = jnp.maximum(m_prev, jnp.max(s, axis=1, keepdims=True))
                alpha = jnp.exp(m_prev - m_new)
                pe = jnp.exp(s - m_new)
                l_sc[hh] = alpha * l_sc[hh] + jnp.sum(pe, axis=1, keepdims=True)
                m_sc[hh] = m_new
                alphas.append(alpha)
                pvs.append(_dot(pe.astype(_MM), vp))
            acc_sc[p] = jnp.where(low, alphas[0], alphas[1]) * acc_sc[p] + jnp.where(low, pvs[0], pvs[1])

    @pl.when(j == nk - 1)
    def _():
        lane = lax.broadcasted_iota(I32, (tq, LANES), 1)
        low = lane < HEAD_DIM
        for p in range(H_B // 2):
            linv = jnp.where(low, 1.0 / l_sc[2 * p], 1.0 / l_sc[2 * p + 1])
            o_ref[0, :, p * LANES:(p + 1) * LANES] = (acc_sc[p] * linv).astype(o_ref.dtype)


def _fox_prompt(qb_s, kb, vb, f, ft, b, t):
    tq = tk = min(256, t)
    nq, nk = t // tq, t // tk
    dq = qb_s.shape[1]
    dk = kb.shape[1]
    kmap = lambda bb, i, j: (bb, jnp.minimum(j, ((i + 1) * tq - 1) // tk), 0)
    return pl.pallas_call(
        functools.partial(_fox_p_kernel, tq=tq, tk=tk),
        out_shape=SDS((b, t, dk), _MM),
        grid=(b, nq, nk),
        in_specs=[BS((1, tq, dq), lambda bb, i, j: (bb, i, 0)),
                  BS((1, tk, dk), kmap), BS((1, tk, dk), kmap),
                  BS((1, tq, H_B), lambda bb, i, j: (bb, i, 0)),
                  BS((1, H_B, tk), lambda bb, i, j: (bb, 0, jnp.minimum(j, ((i + 1) * tq - 1) // tk)))],
        out_specs=BS((1, tq, dk), lambda bb, i, j: (bb, i, 0)),
        scratch_shapes=[pltpu.VMEM((H_B, tq, 1), F32), pltpu.VMEM((H_B, tq, 1), F32),
                        pltpu.VMEM((H_B // 2, tq, LANES), F32)],
        compiler_params=_cp("arbitrary", "arbitrary", "arbitrary"),
        name="fox_prompt",
    )(qb_s.reshape(b, t, dq), kb.reshape(b, t, dk), vb.reshape(b, t, dk), f, ft)


def _order_key(score):
    bits = pltpu.bitcast(score + 0.0, I32)
    return jnp.where(bits < 0, bits ^ jnp.int32(0x7FFFFFFF), bits)


def _kth_largest(load, nchunks, k_f):
    rows = load(0).shape[0]

    def count_ge(cand):
        tot = None
        for c in range(nchunks):
            x = jnp.where(load(c) >= cand, 1.0, 0.0)
            tot = x if tot is None else tot + x
        return jnp.sum(tot, axis=1, keepdims=True)

    def body(it, tau):
        cand = tau ^ jnp.left_shift(jnp.int32(1), 31 - it)
        return jnp.where(count_ge(cand) >= k_f, cand, tau)

    tau = lax.fori_loop(0, 32, body, jnp.full((rows, 1), INT_MIN, I32))
    return tau, count_ge(tau)


def _select_with_ties(load, store, nchunks, width, tau, k_f):
    n_gt = None
    for c in range(nchunks):
        x = jnp.sum(jnp.where(load(c) > tau, 1.0, 0.0), axis=1, keepdims=True)
        n_gt = x if n_gt is None else n_gt + x
    need = k_f - n_gt
    r = lax.broadcasted_iota(I32, (width, width), 0)
    cc = lax.broadcasted_iota(I32, (width, width), 1)
    upper = jnp.where(r <= cc, 1.0, 0.0).astype(_MM)
    carry = jnp.zeros_like(need)
    for c in range(nchunks):
        key = load(c)
        eq = key == tau
        prefix = _dot(jnp.where(eq, 1.0, 0.0).astype(_MM), upper) + carry
        store(c, (key > tau) | (eq & (prefix <= need)))
        carry = prefix[:, width - 1:width]


def _dsa_p_kernel(qi_ref, qa_ref, mq_ref, mk_ref, ka_ref, va_ref, o_ref, key_sc, bias_sc,
                  *, tq, lk, q0, k_top, cw):
    i = pl.program_id(1)
    nch = lk // cw
    qpos = q0 + i * tq + lax.broadcasted_iota(I32, (tq, 1), 0)
    wi = mq_ref[0][:, MISC_WI:MISC_WI + H_IDX] * (H_IDX ** -0.5)
    for c in range(nch):
        kmat = mk_ref[0, c * cw:(c + 1) * cw, :].astype(_MM)
        score = jnp.zeros((tq, cw), F32)
        for hh in range(H_IDX):
            s = _dot_nt(qi_ref[0, :, hh * LANES:(hh + 1) * LANES], kmat)
            score = score + jnp.maximum(s, 0.0) * wi[:, hh:hh + 1]
        kpos = c * cw + lax.broadcasted_iota(I32, (tq, cw), 1)
        key_sc[:, c * cw:(c + 1) * cw] = jnp.where(kpos <= qpos, _order_key(score), INT_MIN)

    load = lambda c: key_sc[:, c * cw:(c + 1) * cw]
    k_f = jnp.minimum(k_top, qpos + 1).astype(F32)
    tau, cnt = _kth_largest(load, nch, k_f)
    has_tie = jnp.max(cnt - k_f) > 0.0

    def store(c, sel):
        bias_sc[:, c * cw:(c + 1) * cw] = jnp.where(sel, 0.0, NEG_INF)

    @pl.when(jnp.logical_not(has_tie))
    def _():
        for c in range(nch):
            store(c, load(c) >= tau)

    @pl.when(has_tie)
    def _():
        _select_with_ties(load, store, nch, cw, tau, k_f)

    ka = ka_ref[0].astype(_MM)
    va = va_ref[0].astype(_MM)
    bias = bias_sc[...]
    lane = lax.broadcasted_iota(I32, (tq, LANES), 1)
    for hh in range(H_A):
        grp = hh // (H_A // KV_A)
        lg = _dot_nt(qa_ref[0, :, hh * LANES:(hh + 1) * LANES], ka) + bias
        m = jnp.max(lg, axis=1, keepdims=True)
        pe = jnp.exp(lg - m)
        l = jnp.sum(pe, axis=1, keepdims=True)
        o = _dot(pe.astype(_MM), va) * (1.0 / l)
        o = jnp.where((lane >= grp * HEAD_DIM) & (lane < (grp + 1) * HEAD_DIM), o, 0.0)
        o_ref[0, :, hh * LANES:(hh + 1) * LANES] = o.astype(o_ref.dtype)


def _dsa_prompt(qi_s, qa_s, misc, ka, va, b, t, k_top):
    tq = min(128, t)
    nq = t // tq
    cw = min(512, t)
    dq = qi_s.shape[1]
    qmap = lambda bb, i: (bb, i, 0)
    kmap = lambda bb, i: (bb, 0, 0)
    return pl.pallas_call(
        functools.partial(_dsa_p_kernel, tq=tq, lk=t, q0=0, k_top=k_top, cw=cw),
        out_shape=SDS((b, t, dq), _MM),
        grid=(b, nq),
        in_specs=[BS((1, tq, dq), qmap), BS((1, tq, dq), qmap), BS((1, tq, LANES), qmap),
                  BS((1, t, LANES), kmap), BS((1, t, LANES), kmap), BS((1, t, LANES), kmap)],
        out_specs=BS((1, tq, dq), qmap),
        scratch_shapes=[pltpu.VMEM((tq, t), I32), pltpu.VMEM((tq, t), F32)],
        compiler_params=_cp("arbitrary", "arbitrary"),
        name="dsa_prompt",
    )(qi_s.reshape(b, t, dq), qa_s.reshape(b, t, dq), misc.reshape(b, t, LANES), misc.reshape(b, t, LANES),
      ka.reshape(b, t, LANES), va.reshape(b, t, LANES))


def _outproj_kernel(oa_ref, ob_ref, wa_ref, wb_ref, x_ref, gate_ref, o_ref):
    y = _dot(oa_ref[...], wa_ref[...]) + _dot(ob_ref[...], wb_ref[...])
    o_ref[...] = x_ref[...] + gate_ref[0] * y


def _outproj(grp, layer, oa, ob, wa, wb, x, tm):
    n, d = x.shape
    g_arr, g_spec = grp.mod_arg(layer, tm, 2)
    da, db = oa.shape[1], ob.shape[1]
    return pl.pallas_call(
        _outproj_kernel,
        out_shape=SDS((n, d), F32),
        grid=(n // tm,),
        in_specs=[BS((tm, da), lambda i: (i, 0)), BS((tm, db), lambda i: (i, 0)),
                  BS((da, d), lambda i: (0, 0)), BS((db, d), lambda i: (0, 0)),
                  BS((tm, d), lambda i: (i, 0)), g_spec],
        out_specs=BS((tm, d), lambda i: (i, 0)),
        compiler_params=_cp("arbitrary"),
        name="even_outproj",
    )(oa, ob, wa, wb, x, g_arr)


def _ffn_kernel(x_ref, g_ref, sh_ref, sc_ref, gate_ref, wg_ref, wu_ref, wo_ref, gf_ref, o_ref, h_sc, acc_sc,
                *, final):
    f = pl.program_id(1)

    @pl.when(f == 0)
    def _():
        h_sc[...] = _modulate(x_ref[...], g_ref[...], sh_ref[0], sc_ref[0]).astype(_MM)
        acc_sc[...] = jnp.zeros(acc_sc.shape, F32)

    h = h_sc[...]
    a = (_silu(_dot(h, wg_ref[...])) * _dot(h, wu_ref[...])).astype(_MM)
    acc_sc[...] += _dot(a, wo_ref[...])

    @pl.when(f == pl.num_programs(1) - 1)
    def _():
        y = x_ref[...] + gate_ref[0] * acc_sc[...]
        if final:
            y = _rms(y, gf_ref[...])
        o_ref[...] = y


def _ffn(grp, layer, x, norm_g, w_in, w_out, norm_final, final, tm, tf):
    n, d = x.shape
    ff = w_out.shape[0]
    nf = ff // tf
    sh_arr, sh_spec = grp.mod_arg(layer, tm, 3)
    sc_arr, sc_spec = grp.mod_arg(layer, tm, 4)
    g_arr, g_spec = grp.mod_arg(layer, tm, 5)
    return pl.pallas_call(
        functools.partial(_ffn_kernel, final=final),
        out_shape=SDS((n, d), F32),
        grid=(n // tm, nf),
        in_specs=[BS((tm, d), lambda i, f: (i, 0)), BS((1, d), lambda i, f: (0, 0)), sh_spec, sc_spec, g_spec,
                  BS((d, tf), lambda i, f: (0, f)), BS((d, tf), lambda i, f: (0, f + nf)),
                  BS((tf, d), lambda i, f: (f, 0)), BS((1, d), lambda i, f: (0, 0))],
        out_specs=BS((tm, d), lambda i, f: (i, 0)),
        scratch_shapes=[pltpu.VMEM((tm, d), _MM), pltpu.VMEM((tm, d), F32)],
        compiler_params=_cp("arbitrary", "arbitrary"),
        name="ffn",
    )(x, norm_g.reshape(1, d), sh_arr, sc_arr, g_arr, w_in, w_in, w_out, norm_final.reshape(1, d))


def _pw1_kernel(x_ref, g_ref, sh_ref, sc_ref, w_ref, b_ref, u_ref):
    h = _modulate(x_ref[...], g_ref[...], sh_ref[0], sc_ref[0]).astype(_MM)
    y = _dot(h, w_ref[...]) + b_ref[...]
    dc = y.shape[1] // 2
    u_ref[...] = y[:, :dc] * _sigmoid(y[:, dc:])


def _pw1(grp, layer, x, norm_g, w, bias, tm):
    n, d = x.shape
    dc2 = w.shape[1]
    sh_arr, sh_spec = grp.mod_arg(layer, tm, 0)
    sc_arr, sc_spec = grp.mod_arg(layer, tm, 1)
    return pl.pallas_call(
        _pw1_kernel,
        out_shape=SDS((n, dc2 // 2), F32),
        grid=(n // tm,),
        in_specs=[BS((tm, d), lambda i: (i, 0)), BS((1, d), lambda i: (0, 0)), sh_spec, sc_spec,
                  BS((d, dc2), lambda i: (0, 0)), BS((1, dc2), lambda i: (0, 0))],
        out_specs=BS((tm, dc2 // 2), lambda i: (i, 0)),
        compiler_params=_cp("arbitrary"),
        name="conv_pw1_glu",
    )(x, norm_g.reshape(1, d), sh_arr, sc_arr, w, bias.reshape(1, dc2))


def _ln_swish_pw2(z, lng, lnb, w2, b2):
    mu = jnp.mean(z, axis=-1, keepdims=True)
    zc = z - mu
    var = jnp.mean(zc * zc, axis=-1, keepdims=True)
    zn = zc * lax.rsqrt(var + EPS) * lng + lnb
    return _dot(_silu(zn).astype(_MM), w2) + b2


HALO = 32


def _conv_p_kernel(ucur_ref, uhalo_ref, wdw_ref, bdw_ref, lng_ref, lnb_ref, w2_ref, b2_ref, x_ref, gate_ref,
                   o_ref, full_sc, *, tm):
    i = pl.program_id(1)
    full_sc[0:HALO, :] = jnp.where(i > 0, uhalo_ref[0], 0.0)
    full_sc[HALO:HALO + tm, :] = ucur_ref[0]
    z = jnp.zeros((tm, full_sc.shape[1]), F32) + bdw_ref[...]
    base = HALO - (CONV_WIDTH - 1)
    for w in range(CONV_WIDTH):
        z = z + full_sc[base + w:base + w + tm, :] * wdw_ref[w:w + 1, :]
    y = _ln_swish_pw2(z, lng_ref[...], lnb_ref[...], w2_ref[...], b2_ref[...])
    o_ref[0] = x_ref[0] + gate_ref[0] * y


def _conv_prompt(grp, layer, u, x, wdw, bdw, lng, lnb, w2, b2, tm):
    b, t, d = grp.nb, grp.t, grp.d
    dc = u.shape[1]
    g_arr, _ = grp.mod_arg(layer, tm, 2)
    hb = tm // HALO
    wpad = jnp.zeros((HALO, dc), F32).at[:CONV_WIDTH].set(wdw)
    vec = lambda bb, i: (0, 0)
    return pl.pallas_call(
        functools.partial(_conv_p_kernel, tm=tm),
        out_shape=SDS((b, t, d), F32),
        grid=(b, t // tm),
        in_specs=[BS((1, tm, dc), lambda bb, i: (bb, i, 0)),
                  BS((1, HALO, dc), lambda bb, i: (bb, jnp.maximum(i * hb - 1, 0), 0)),
                  BS((HALO, dc), vec), BS((1, dc), vec), BS((1, dc), vec), BS((1, dc), vec),
                  BS((dc, d), vec), BS((1, d), vec),
                  BS((1, tm, d), lambda bb, i: (bb, i, 0)),
                  BS((1, 1, d), lambda bb, i: (bb, 0, 2))],
        out_specs=BS((1, tm, d), lambda bb, i: (bb, i, 0)),
        scratch_shapes=[pltpu.VMEM((HALO + tm, dc), F32)],
        compiler_params=_cp("arbitrary", "arbitrary"),
        name="conv_prompt",
    )(u.reshape(b, t, dc), u.reshape(b, t, dc), wpad, bdw.reshape(1, dc), lng.reshape(1, dc), lnb.reshape(1, dc),
      w2, b2.reshape(1, d), x.reshape(b, t, d), g_arr).reshape(b * t, d)


def _conv_s_kernel(buf_ref, u_ref, wdw_ref, bdw_ref, lng_ref, lnb_ref, w2_ref, b2_ref, x_ref, gate_ref, o_ref):
    nbuf = buf_ref.shape[0]
    t = u_ref.shape[0]
    zs = []
    for tt in range(t):
        z = jnp.zeros(u_ref.shape[1:], F32) + bdw_ref[...]
        for w in range(CONV_WIDTH):
            src = tt + w
            row = buf_ref[src] if src < nbuf else u_ref[src - nbuf]
            z = z + row * wdw_ref[w:w + 1, :]
        zs.append(z)
    y = _ln_swish_pw2(jnp.concatenate(zs, axis=0), lng_ref[...], lnb_ref[...], w2_ref[...], b2_ref[...])
    bb = u_ref.shape[1]
    for tt in range(t):
        o_ref[tt] = x_ref[tt] + gate_ref[...] * y[tt * bb:(tt + 1) * bb, :]


def _conv_sample(buf_t, u_t, x_t, gate, wdw, bdw, lng, lnb, w2, b2):
    nbuf, db, dc = buf_t.shape
    t, _, d = x_t.shape
    bb = min(32, db)
    wpad = jnp.zeros((HALO, dc), F32).at[:CONV_WIDTH].set(wdw)
    vec = lambda j: (0, 0)
    return pl.pallas_call(
        _conv_s_kernel,
        out_shape=SDS((t, db, d), F32),
        grid=(db // bb,),
        in_specs=[BS((nbuf, bb, dc), lambda j: (0, j, 0)), BS((t, bb, dc), lambda j: (0, j, 0)),
                  BS((HALO, dc), vec), BS((1, dc), vec), BS((1, dc), vec), BS((1, dc), vec),
                  BS((dc, d), vec), BS((1, d), vec),
                  BS((t, bb, d), lambda j: (0, j, 0)), BS((bb, d), lambda j: (j, 2))],
        out_specs=BS((t, bb, d), lambda j: (0, j, 0)),
        compiler_params=_cp("arbitrary"),
        name="conv_sample",
    )(buf_t, u_t, wpad, bdw.reshape(1, dc), lng.reshape(1, dc), lnb.reshape(1, dc), w2, b2.reshape(1, d), x_t, gate)


def _per_head_rows(x, t):
    nh, w = x.shape
    row = lax.broadcasted_iota(I32, (nh * t, w), 0) // t
    out = jnp.zeros((nh * t, w), x.dtype)
    for hh in range(nh):
        out = jnp.where(row == hh, x[hh:hh + 1, :], out)
    return out


def _strict_lower(n):
    r = lax.broadcasted_iota(I32, (n, n), 0)
    c = lax.broadcasted_iota(I32, (n, n), 1)
    return jnp.where(r > c, 1.0, 0.0).astype(F32)


def _attn_s_kernel(pt_ref, qb_ref, qa_ref, qi_ref, wi_ref, cq_ref, cqt_ref, kbn_ref, vbn_ref, kan_ref, van_ref,
                   kin_ref, *rest, past, k_top, t, gp):
    page_refs = rest[:6 * gp]
    oa_ref, ob_ref = rest[6 * gp:6 * gp + 2]
    (m_sc, l_sc, acc_sc, r_sc, ka_sc, va_sc, key_sc, sel_sc,
     kbn_sc, vbn_sc, kan_sc, van_sc, kin_sc) = rest[6 * gp + 2:]
    b = pl.program_id(0)
    s = pl.program_id(1)
    ns = pl.num_programs(1)
    npg = ka_sc.shape[0]
    ps = ka_sc.shape[2]
    rows = H_B * t
    nt8 = kbn_ref.shape[1]
    row_tok = lax.broadcasted_iota(I32, (rows, ps), 0) % t
    lane_r = lax.broadcasted_iota(I32, (rows, ps), 1)
    row8_tok = lax.broadcasted_iota(I32, (8, ps), 0) % t
    lane8 = lax.broadcasted_iota(I32, (8, ps), 1)
    wi = wi_ref[0] * (H_IDX ** -0.5)

    def dup_scores(s32):
        s32 = jnp.maximum(s32, 0.0) * wi
        sc = s32[0:t]
        for hh in range(1, H_IDX):
            sc = sc + s32[hh * t:(hh + 1) * t]
        return jnp.concatenate([sc] * (8 // t), axis=0)

    @pl.when((b == 0) & (s == 0))
    def _():
        for ref in (kbn_sc, vbn_sc, kan_sc, van_sc, kin_sc):
            ref[...] = jnp.zeros(ref.shape, F32)

    @pl.when(s == 0)
    def _():
        kbn_sc[0:nt8, :] = kbn_ref[0]
        vbn_sc[0:nt8, :] = vbn_ref[0]
        kan_sc[0:nt8, :] = kan_ref[0]
        van_sc[0:nt8, :] = van_ref[0]
        kin_sc[0:nt8, :] = kin_ref[0]
        r_sc[...] = jnp.zeros(r_sc.shape, F32)
        lg = _dot_nt(qb_ref[0], kbn_sc[...].astype(_MM)) + cq_ref[0] - _per_head_rows(cqt_ref[0], t)
        lg = jnp.where(lane_r <= row_tok, lg, NEG_INF)
        m = jnp.max(lg, axis=1, keepdims=True)
        pe = jnp.exp(lg - m)
        m_sc[...] = m
        l_sc[...] = jnp.sum(pe, axis=1, keepdims=True)
        acc_sc[...] = _dot(pe.astype(_MM), vbn_sc[...].astype(_MM))
        key_new = _order_key(dup_scores(_dot_nt(qi_ref[0], kin_sc[...].astype(_MM))))
        key_sc[npg] = jnp.where(lane8 <= row8_tok, key_new, INT_MIN)

    lower = _strict_lower(ps)
    for g in range(gp):
        kb_ref, vb_ref, ka_ref, va_ref, ki_ref, lf_ref = page_refs[6 * g:6 * g + 6]
        pi = npg - 1 - (s * gp + g)
        lft = lf_ref[0, 0].astype(F32)
        suf = _dot_f32(lft, lower) + r_sc[...]
        r_sc[...] = suf[:, 0:1] + lft[:, 0:1]
        kt = kb_ref[0, 0].reshape(H_B * HEAD_DIM, ps).astype(_MM)
        vt = vb_ref[0, 0].reshape(H_B * HEAD_DIM, ps).astype(_MM)
        lg = _dot(qb_ref[0], kt) + (_per_head_rows(suf, t) + cq_ref[0])
        m_prev = m_sc[...]
        m_new = jnp.maximum(m_prev, jnp.max(lg, axis=1, keepdims=True))
        alpha = jnp.exp(m_prev - m_new)
        pe = jnp.exp(lg - m_new)
        l_sc[...] = alpha * l_sc[...] + jnp.sum(pe, axis=1, keepdims=True)
        acc_sc[...] = alpha * acc_sc[...] + _dot_nt(pe.astype(_MM), vt)
        m_sc[...] = m_new
        key_sc[pi] = _order_key(dup_scores(_dot(qi_ref[0], ki_ref[0, 0].astype(_MM))))
        ka_sc[pi] = ka_ref[0, 0].reshape(KV_A * HEAD_DIM, ps).astype(_MM)
        va_sc[pi] = va_ref[0, 0].reshape(KV_A * HEAD_DIM, ps).astype(_MM)

    @pl.when(s == ns - 1)
    def _():
        o = acc_sc[...] * (1.0 / l_sc[...])
        lane_h = lax.broadcasted_iota(I32, o.shape, 1) // HEAD_DIM
        row_h = lax.broadcasted_iota(I32, o.shape, 0) // t
        o = jnp.where(lane_h == row_h, o, 0.0)
        ob = o[0:t]
        for hh in range(1, H_B):
            ob = ob + o[hh * t:(hh + 1) * t]
        ob_ref[0] = ob

        nch = npg + 1
        qpos = past + lax.broadcasted_iota(I32, (8, 1), 0) % t
        k_f = jnp.minimum(k_top, qpos + 1).astype(F32)
        load = lambda c: key_sc[c]
        tau, cnt = _kth_largest(load, nch, k_f)
        has_tie = jnp.max(cnt - k_f) > 0.0

        def store(c, sel):
            sel_sc[c] = jnp.where(sel, 0.0, NEG_INF)

        @pl.when(jnp.logical_not(has_tie))
        def _():
            for c in range(nch):
                store(c, load(c) >= tau)

        @pl.when(has_tie)
        def _():
            _select_with_ties(load, store, nch, ps, tau, k_f)

        qa = qa_ref[0]
        kan = kan_sc[...].astype(_MM)
        lgs = []
        for c in range(nch):
            prod = _dot(qa, ka_sc[c]) if c < npg else _dot_nt(qa, kan)
            lgs.append(prod + jnp.concatenate([sel_sc[c]] * (rows // 8), axis=0))
        m = lgs[0]
        for c in range(1, nch):
            m = jnp.maximum(m, lgs[c])
        m = jnp.max(m, axis=1, keepdims=True)
        lsum = jnp.zeros((rows, ps), F32)
        out = jnp.zeros((rows, KV_A * HEAD_DIM), F32)
        for c in range(nch):
            pe = jnp.exp(lgs[c] - m)
            lsum = lsum + pe
            if c < npg:
                out = out + _dot_nt(pe.astype(_MM), va_sc[c])
            else:
                out = out + _dot(pe.astype(_MM), van_sc[...].astype(_MM))
        out = out * (1.0 / jnp.sum(lsum, axis=1, keepdims=True))
        lane_j = lax.broadcasted_iota(I32, (t, KV_A * HEAD_DIM), 1) // HEAD_DIM
        for hh in range(H_A):
            piece = out[hh * t:(hh + 1) * t, :]
            oa_ref[0, :, hh * LANES:(hh + 1) * LANES] = jnp.where(lane_j == hh // (H_A // KV_A), piece, 0.0)


def _attn_sample(layer, page_table, caches_t, qbd, qabd, qi32, wi32, cq32, cqt, kbn, vbn, kan, van, kin, k_top, t):
    kb_t, vb_t, ka_t, va_t, ki_t, lf_t = caches_t
    db, npg = page_table.shape
    ps = ki_t.shape[-1]
    rows = H_B * t
    gp = 4 if npg % 4 == 0 else 1
    nt8 = kbn.shape[1]
    const3 = lambda b, s, pt: (b, 0, 0)
    full = lambda a: BS((1,) + a.shape[1:], const3)

    def page_specs(g):
        def pg(b, s, pt):
            return pt[b, npg - 1 - (s * gp + g)]
        five = lambda nh: BS((1, 1, nh, HEAD_DIM, ps), lambda b, s, pt: (layer, pg(b, s, pt), 0, 0, 0))
        return [five(H_B), five(H_B), five(KV_A), five(KV_A),
                BS((1, 1, D_IDX, ps), lambda b, s, pt: (layer, pg(b, s, pt), 0, 0)),
                BS((1, 1, H_B, ps), lambda b, s, pt: (layer, pg(b, s, pt), 0, 0))]

    small = [qbd, qabd, qi32, wi32, cq32, cqt, kbn, vbn, kan, van, kin]
    in_specs = [full(a) for a in small]
    pages = []
    for g in range(gp):
        in_specs += page_specs(g)
        pages += [kb_t, vb_t, ka_t, va_t, ki_t, lf_t]
    da = H_A * LANES
    db_ = H_B * HEAD_DIM
    return pl.pallas_call(
        functools.partial(_attn_s_kernel, past=npg * ps, k_top=k_top, t=t, gp=gp),
        out_shape=[SDS((db, t, da), F32), SDS((db, t, db_), F32)],
        grid_spec=pltpu.PrefetchScalarGridSpec(
            num_scalar_prefetch=1,
            grid=(db, npg // gp),
            in_specs=in_specs,
            out_specs=[BS((1, t, da), const3), BS((1, t, db_), const3)],
            scratch_shapes=[pltpu.VMEM((rows, 1), F32), pltpu.VMEM((rows, 1), F32), pltpu.VMEM((rows, db_), F32),
                            pltpu.VMEM((H_B, 1), F32),
                            pltpu.VMEM((npg, KV_A * HEAD_DIM, ps), _MM), pltpu.VMEM((npg, KV_A * HEAD_DIM, ps), _MM),
                            pltpu.VMEM((npg + 1, 8, ps), I32), pltpu.VMEM((npg + 1, 8, ps), F32),
                            pltpu.VMEM((ps, db_), F32), pltpu.VMEM((ps, db_), F32),
                            pltpu.VMEM((ps, KV_A * HEAD_DIM), F32), pltpu.VMEM((ps, KV_A * HEAD_DIM), F32),
                            pltpu.VMEM((ps, D_IDX), F32)]),
        compiler_params=_cp("arbitrary", "arbitrary"),
        name="attn_sample",
    )(page_table, *small, *pages)


def _old_attn_s_kernel(pt_ref, qb_ref, qa_ref, qi_ref, wi_ref, fq_ref, kb_ref, vb_ref, ka_ref, va_ref, ki_ref,
                   fk_ref, kbn_ref, vbn_ref, kan_ref, van_ref, kin_ref, fkn_ref, oa_ref, ob_ref,
                   m_sc, l_sc, acc_sc, ka_sc, va_sc, key_sc, sel_sc, *, past, k_top, t):
    p = pl.program_id(1)
    npg = pl.num_programs(1)
    ps = ki_ref.shape[2]
    rows = H_B * t
    wb = ps * H_B
    wa = ps * KV_A

    @pl.when(p == 0)
    def _():
        m_sc[...] = jnp.full(m_sc.shape, NEG_INF, F32)
        l_sc[...] = jnp.zeros(l_sc.shape, F32)
        acc_sc[...] = jnp.zeros(acc_sc.shape, F32)

    k2 = kb_ref[0, 0].reshape(wb, HEAD_DIM).astype(_MM)
    v2 = vb_ref[0, 0].reshape(wb, HEAD_DIM).astype(_MM)
    lt = _dot_nt(qb_ref[0], k2) + fq_ref[0] - fk_ref[0]
    r_b = lax.broadcasted_iota(I32, (rows, wb), 0)
    c_b = lax.broadcasted_iota(I32, (rows, wb), 1)
    lt = jnp.where((c_b % H_B) == (r_b // t), lt, NEG_INF)
    _online_update(lt, v2, m_sc, l_sc, acc_sc)

    def dup_scores(ki2):
        s = _dot_nt(qi_ref[0], ki2)
        s = jnp.maximum(s, 0.0) * (wi_ref[0] * (H_IDX ** -0.5))
        sc = s[0:t]
        for hh in range(1, H_IDX):
            sc = sc + s[hh * t:(hh + 1) * t]
        return jnp.concatenate([sc] * (8 // t), axis=0)

    r_e = lax.broadcasted_iota(I32, (wa, ps), 0)
    c_e = lax.broadcasted_iota(I32, (wa, ps), 1)
    expand = jnp.where((r_e // KV_A) == c_e, 1.0, 0.0).astype(_MM)
    ki2 = _dot(expand, ki_ref[0, 0].astype(_MM)).astype(_MM)
    key_sc[p] = _order_key(dup_scores(ki2))
    ka_sc[p] = ka_ref[0, 0].reshape(wa, HEAD_DIM).astype(_MM)
    va_sc[p] = va_ref[0, 0].reshape(wa, HEAD_DIM).astype(_MM)

    @pl.when(p == npg - 1)
    def _():
        nn = kbn_ref.shape[1]
        ltn = _dot_nt(qb_ref[0], kbn_ref[0].astype(_MM)) + fq_ref[0] - fkn_ref[0]
        r_n = lax.broadcasted_iota(I32, (rows, nn), 0)
        c_n = lax.broadcasted_iota(I32, (rows, nn), 1)
        ok = ((c_n % H_B) == (r_n // t)) & ((c_n // H_B) <= (r_n % t))
        _online_update(jnp.where(ok, ltn, NEG_INF), vbn_ref[0].astype(_MM), m_sc, l_sc, acc_sc)
        ob_ref[0] = acc_sc[...] * (1.0 / l_sc[...])

        nch = key_sc.shape[0]
        r8 = lax.broadcasted_iota(I32, (8, wa), 0)
        c8 = lax.broadcasted_iota(I32, (8, wa), 1)
        ok_new = (c8 // KV_A) <= (r8 % t)
        key_new = _order_key(dup_scores(kin_ref[0].astype(_MM)))
        key_sc[nch - 1] = jnp.where(ok_new, key_new, INT_MIN)
        ka_sc[nch - 1] = kan_ref[0].astype(_MM)
        va_sc[nch - 1] = van_ref[0].astype(_MM)

        qpos = past + lax.broadcasted_iota(I32, (8, 1), 0) % t
        k_f = (KV_A * jnp.minimum(k_top, qpos + 1)).astype(F32)
        load = lambda c: key_sc[c]
        tau, cnt = _kth_largest(load, nch, k_f)
        has_tie = jnp.max(cnt - k_f) > 0.0

        def store(c, sel):
            sel_sc[c] = jnp.where(sel, 0.0, NEG_INF)

        @pl.when(jnp.logical_not(has_tie))
        def _():
            for c in range(nch):
                store(c, load(c) >= tau)

        @pl.when(has_tie)
        def _():
            _select_with_ties(load, store, nch, wa, tau, k_f)

        m_sc[...] = jnp.full(m_sc.shape, NEG_INF, F32)
        l_sc[...] = jnp.zeros(l_sc.shape, F32)
        acc_sc[...] = jnp.zeros(acc_sc.shape, F32)
        r_a = lax.broadcasted_iota(I32, (rows, wa), 0)
        c_a = lax.broadcasted_iota(I32, (rows, wa), 1)
        grp_ok = (c_a % KV_A) == (r_a // (t * (H_A // KV_A)))
        for c in range(nch):
            bias = jnp.concatenate([sel_sc[c]] * (rows // 8), axis=0)
            lg = _dot_nt(qa_ref[0], ka_sc[c]) + bias
            _online_update(jnp.where(grp_ok, lg, NEG_INF), va_sc[c], m_sc, l_sc, acc_sc)
        oa_ref[0] = acc_sc[...] * (1.0 / l_sc[...])


def _old_attn_sample(layer, page_table, caches, qb32, qa32, qi32, wi32, fq32, fk_row, kbn, vbn, kan, van, kin, fkn_row,
                 k_top, t):
    cache_a_k, cache_a_v, cache_a_idx_k, cache_b_k, cache_b_v = caches
    db, npg = page_table.shape
    ps = cache_a_idx_k.shape[2]
    rows = H_B * t
    wb, wa = ps * H_B, ps * KV_A
    past = npg * ps
    qspec = BS((1, rows, HEAD_DIM), lambda b, p, pt: (b, 0, 0))
    cspec = BS((1, rows, 1), lambda b, p, pt: (b, 0, 0))
    pool5 = lambda nh: BS((1, 1, ps, nh, HEAD_DIM), lambda b, p, pt: (layer, pt[b, p], 0, 0, 0))
    newspec = lambda r: BS((1, r, HEAD_DIM), lambda b, p, pt: (b, 0, 0))
    return pl.pallas_call(
        functools.partial(_attn_s_kernel, past=past, k_top=k_top, t=t),
        out_shape=[SDS((db, rows, HEAD_DIM), F32), SDS((db, rows, HEAD_DIM), F32)],
        grid_spec=pltpu.PrefetchScalarGridSpec(
            num_scalar_prefetch=1,
            grid=(db, npg),
            in_specs=[qspec, qspec, qspec, cspec, cspec,
                      pool5(H_B), pool5(H_B), pool5(KV_A), pool5(KV_A),
                      BS((1, 1, ps, D_IDX), lambda b, p, pt: (layer, pt[b, p], 0, 0)),
                      BS((1, 1, wb), lambda b, p, pt: (b, 0, p)),
                      newspec(rows), newspec(rows), newspec(wa), newspec(wa), newspec(wa),
                      BS((1, 1, rows), lambda b, p, pt: (b, 0, 0))],
            out_specs=[BS((1, rows, HEAD_DIM), lambda b, p, pt: (b, 0, 0)),
                       BS((1, rows, HEAD_DIM), lambda b, p, pt: (b, 0, 0))],
            scratch_shapes=[pltpu.VMEM((rows, 1), F32), pltpu.VMEM((rows, 1), F32),
                            pltpu.VMEM((rows, HEAD_DIM), F32),
                            pltpu.VMEM((npg + 1, wa, HEAD_DIM), _MM), pltpu.VMEM((npg + 1, wa, HEAD_DIM), _MM),
                            pltpu.VMEM((npg + 1, 8, wa), I32), pltpu.VMEM((npg + 1, 8, wa), F32)]),
        compiler_params=_cp("arbitrary", "arbitrary"),
        name="attn_sample",
    )(page_table, qb32, qa32, qi32, wi32, fq32, cache_b_k, cache_b_v, cache_a_k, cache_a_v, cache_a_idx_k,
      fk_row, kbn, vbn, kan, van, kin, fkn_row)


def _row_tile(n, pref):
    tm = min(pref, n)
    assert n % tm == 0, (n, tm)
    return tm


def _even_prompt(grp, layer, x, w, st):
    b, t = grp.nb, grp.t
    tm = _row_tile(t, 256)
    outs = _inproj(grp, layer, x, w["norm_mix"], w["w_in_slots"], w["groups_slots"], w["b_f"], st["rope"], tm)
    qa_s, qi_s, qb_s, kb, vb, ka, va, misc = outs
    logf = misc[:, MISC_FB:MISC_FB + H_B]
    f = _cumsum_prompt(logf.reshape(b, t, H_B))
    o_b = _fox_prompt(qb_s, kb, vb, f, jnp.swapaxes(f, 1, 2), b, t)
    o_a = _dsa_prompt(qi_s, qa_s, misc, ka, va, b, t, min(TOPK_MAX, t // 4))
    x = _outproj(grp, layer, o_a.reshape(b * t, -1), o_b.reshape(b * t, -1), w["w_out_a_slots"], w["w_out_b"], x,
                 _row_tile(grp.n, 512))
    return x, (ka, va, misc[:, :D_IDX], kb, vb, logf)


def _even_sample(grp, layer, x, w, st):
    db, t = grp.nb, grp.t
    n = grp.n
    tm = _row_tile(n, 256)
    assert 8 % t == 0, t
    outs = _inproj(grp, layer, x, w["norm_mix"], w["w_in_plain"], w["groups_plain"], w["b_f"], st["rope"], tm,
                   cum_t=t)
    qa, qi, qb, kb, vb, ka, va, misc = outs
    ki, wi, logf = misc[:, :D_IDX], misc[:, MISC_WI:MISC_WI + H_IDX], misc[:, MISC_FB:MISC_FB + H_B]
    cq = misc[:, MISC_CQ:MISC_CQ + H_B]
    page_table = st["page_table"]
    npg = page_table.shape[1]
    ps = st["caches_t"][0].shape[-1]

    def head_major(a, nh):
        return a.reshape(db, t, nh, -1).transpose(0, 2, 1, 3)

    eye_b = jnp.eye(H_B, dtype=qb.dtype)
    qbd = (head_major(qb, H_B)[:, :, :, None, :] * eye_b[None, :, None, :, None]).reshape(db, H_B * t, -1)
    grp_hot = (jnp.arange(H_A)[:, None] // (H_A // KV_A) == jnp.arange(KV_A)[None, :]).astype(qa.dtype)
    qabd = (head_major(qa, H_A)[:, :, :, None, :] * grp_hot[None, :, None, :, None]).reshape(db, H_A * t, -1)
    qi32 = head_major(qi, H_IDX).reshape(db, H_IDX * t, D_IDX)
    wi32 = head_major(wi, H_IDX).reshape(db, H_IDX * t, 1)
    cq32 = head_major(cq, H_B).reshape(db, H_B * t, 1)
    cqt = jnp.pad(cq.reshape(db, t, H_B).transpose(0, 2, 1), ((0, 0), (0, 0), (0, ps - t)))
    pad8 = lambda a: jnp.pad(a.reshape(db, t, -1), ((0, 0), (0, 8 - t), (0, 0)))
    o_a, o_b = _attn_sample(st["att_layer"], page_table, st["caches_t"], qbd, qabd, qi32, wi32, cq32, cqt,
                            pad8(kb), pad8(vb), pad8(ka), pad8(va), pad8(ki),
                            min(TOPK_MAX, (npg * ps + t) // 4), t)
    x = _outproj(grp, layer, o_a.reshape(n, -1).astype(_MM), o_b.reshape(n, -1).astype(_MM), w["w_out_a_slots"],
                 w["w_out_b"], x, _row_tile(n, 512))
    return x, (ka, va, ki, kb, vb, logf)


def _odd_prompt(grp, layer, x, w, st):
    tm = _row_tile(grp.t, 256)
    u = _pw1(grp, layer, x, w["norm_mix"], w["w_pw1"], w["b_pw1"], _row_tile(grp.n, 256))
    x = _conv_prompt(grp, layer, u, x, w["w_dw"], w["b_dw"], w["ln_g"], w["ln_b"], w["w_pw2"], w["b_pw2"], tm)
    nb, t = grp.nb, grp.t
    state = u.reshape(nb, t, -1)[:, t - (CONV_WIDTH - 1):]
    return x, state


def _odd_sample(grp, layer, x, w, st):
    db, t, d = grp.nb, grp.t, grp.d
    u = _pw1(grp, layer, x, w["norm_mix"], w["w_pw1"], w["b_pw1"], _row_tile(grp.n, 256))
    buf = st["state_conv"][st["conv_layer"]]
    u3 = u.reshape(db, t, -1)
    x_t = _conv_sample(jnp.swapaxes(buf, 0, 1), jnp.swapaxes(u3, 0, 1), jnp.swapaxes(x.reshape(db, t, d), 0, 1),
                       st["mod_batch"][layer], w["w_dw"], w["b_dw"], w["ln_g"], w["ln_b"], w["w_pw2"], w["b_pw2"])
    x = jnp.swapaxes(x_t, 0, 1).reshape(db * t, d)
    state = jnp.concatenate([buf.astype(F32), u3], axis=1)[:, t:]
    return x, state


def _trunk(grp, x, even_fn, odd_fn, layer_w, st, norm_final):
    depth = len(layer_w)
    att, conv = [], []
    for i in range(depth):
        w = layer_w[i]
        if i % 2 == 0:
            x, s = even_fn(grp, i, x, w, dict(st, att_layer=i // 2))
            att.append(s)
        else:
            x, s = odd_fn(grp, i, x, w, dict(st, conv_layer=i // 2))
            conv.append(s)
        tm = _row_tile(grp.n, 512)
        x = _ffn(grp, i, x, w["norm_ffn"], w["w_ffn_in"], w["w_ffn_out"], norm_final, i == depth - 1, tm,
                 w["tf"])
    return x, att, conv


def kernel(x_prompt, x_sample, cache_a_k, cache_a_v, cache_a_idx_k, cache_b_k, cache_b_v, cache_b_logf,
           state_conv, page_table, c_prompt, c_sample, w_in_att, b_fgate, w_out_att, w_pw1, b_pw1, w_dw,
           b_dw, ln_conv_g, ln_conv_b, w_pw2, b_pw2, w_ada, b_ada, norm_mix, norm_ffn, w_ffn_in,
           w_ffn_out, norm_final):
    bp, sp, d = x_prompt.shape
    db, ts, _ = x_sample.shape
    depth = w_ada.shape[0]
    npg, ps = page_table.shape[1], cache_a_idx_k.shape[2]
    past = npg * ps

    mod = _ada(jnp.concatenate([c_prompt, c_sample], axis=0), w_ada, b_ada)
    mod_p = mod[:, :bp].reshape(depth, bp, 1, 6 * d)
    mod_b = mod[:, bp:]
    mod_s = jnp.broadcast_to(mod_b[:, :, None, :], (depth, db, ts, 6 * d)).reshape(depth, db * ts, 6 * d)
    grp_p = _Group(bp, sp, d, mod_p, per_row=False)
    grp_s = _Group(db, ts, d, mod_s, per_row=True)

    ff = w_ffn_out.shape[1]
    tf = 256 if ff % 256 == 0 else ff
    half_a = H_A * HEAD_DIM
    layer_w = []
    for i in range(depth):
        w = {"norm_mix": norm_mix[i], "norm_ffn": norm_ffn[i], "w_ffn_in": w_ffn_in[i].astype(_MM),
             "w_ffn_out": w_ffn_out[i].astype(_MM), "tf": tf}
        l = i // 2
        if i % 2 == 0:
            w["w_in_slots"], w["groups_slots"] = _pack_w_in(w_in_att[l], True)
            w["w_in_plain"], w["groups_plain"] = _pack_w_in(w_in_att[l], False)
            w["b_f"] = b_fgate[l]
            wo = w_out_att[l]
            wa = wo[:half_a].reshape(H_A, HEAD_DIM, d)
            wa_slots = jnp.zeros((H_A, 2, HEAD_DIM, d), wo.dtype)
            for hh in range(H_A):
                wa_slots = wa_slots.at[hh, hh // (H_A // KV_A)].set(wa[hh])
            w["w_out_a_slots"] = wa_slots.reshape(H_A * LANES, d).astype(_MM)
            w["w_out_a"] = wo[:half_a].astype(_MM)
            w["w_out_b"] = wo[half_a:].astype(_MM)
        else:
            w.update(w_pw1=w_pw1[l].astype(_MM), b_pw1=b_pw1[l], w_dw=w_dw[l], b_dw=b_dw[l], ln_g=ln_conv_g[l],
                     ln_b=ln_conv_b[l], w_pw2=w_pw2[l].astype(_MM), b_pw2=b_pw2[l])
        layer_w.append(w)

    st_p = {"rope": _rope_tables(jnp.arange(sp, dtype=I32).astype(F32))}
    tm_s = _row_tile(db * ts, 256)
    pos_s = (past + (jnp.arange(tm_s, dtype=I32) % ts)).astype(F32)
    pos_last5 = lambda a: jnp.transpose(a, (0, 1, 3, 4, 2))
    pos_last4 = lambda a: jnp.transpose(a, (0, 1, 3, 2))
    caches_t = (pos_last5(cache_b_k), pos_last5(cache_b_v), pos_last5(cache_a_k), pos_last5(cache_a_v),
                pos_last4(cache_a_idx_k), pos_last4(cache_b_logf))
    st_s = {"rope": _rope_tables(pos_s), "page_table": page_table, "caches_t": caches_t,
            "state_conv": state_conv, "mod_batch": mod_b}

    y_p, att_p, conv_p = _trunk(grp_p, x_prompt.reshape(bp * sp, d), _even_prompt, _odd_prompt, layer_w, st_p,
                                norm_final)
    y_s, att_s, conv_s = _trunk(grp_s, x_sample.reshape(db * ts, d), _even_sample, _odd_sample, layer_w, st_s,
                                norm_final)

    def stack_att(att, nb, t):
        shapes = ((KV_A, HEAD_DIM), (KV_A, HEAD_DIM), (D_IDX,), (H_B, HEAD_DIM), (H_B, HEAD_DIM), (H_B,))
        return [jnp.stack([s[j].reshape(nb, t, *shapes[j]) for s in att]) for j in range(6)]

    out_p = stack_att(att_p, bp, sp)
    out_s = stack_att(att_s, db, ts)
    return (y_p.reshape(bp, sp, d), y_s.reshape(db, ts, d), *out_p, jnp.stack(conv_p),
            *out_s, jnp.stack(conv_s))
```

```python
import functools

import jax
import jax.numpy as jnp
from jax import lax
from jax.experimental import pallas as pl
from jax.experimental.pallas import tpu as pltpu

F32 = jnp.float32
I32 = jnp.int32
_MM = jnp.bfloat16

HEAD_DIM = 64
H_A = 8
KV_A = 2
H_IDX = 8
D_IDX = 64
H_B = 8
ROT_DIM = HEAD_DIM // 4
ROPE_THETA = 500000.0
TOPK_MAX = 256
CONV_WIDTH = 31
EPS = 1e-6
LANES = 128
INT_MIN = -(2 ** 31)
NEG_INF = float("-inf")
Q_SCALE = HEAD_DIM ** -0.5
VMEM_LIMIT = 56 * 1024 * 1024

ROW_TILE = 256
FFN_ROW_TILE = 512
FOX_TQ, FOX_TK = 512, 256
DSA_TQ = 128
DSA_CHUNK = 512
DSA_BANDS = 4
PAGES_PER_STEP = 8

_SPLITS = (H_A * HEAD_DIM, KV_A * HEAD_DIM, KV_A * HEAD_DIM, H_IDX * D_IDX, D_IDX, H_IDX,
           H_B * HEAD_DIM, H_B * HEAD_DIM, H_B * HEAD_DIM, H_B)
_NAMES = ("qa", "ka", "va", "qi", "ki", "wi", "qb", "kb", "vb", "fb")
_OFF = {}
_o = 0
for _n, _w in zip(_NAMES, _SPLITS):
    _OFF[_n] = (_o, _w)
    _o += _w
MISC_WI = D_IDX
MISC_FB = D_IDX + H_IDX
MISC_CQ = MISC_FB + H_B

SDS = jax.ShapeDtypeStruct
BS = pl.BlockSpec


def _cp(*sem):
    return pltpu.CompilerParams(dimension_semantics=sem, vmem_limit_bytes=VMEM_LIMIT)


def _dot(a, b):
    return jnp.dot(a, b, preferred_element_type=F32)


def _dot_nt(a, b):
    return lax.dot_general(a, b, (((1,), (1,)), ((), ())), preferred_element_type=F32)


def _sigmoid(x):
    return 1.0 / (1.0 + jnp.exp(-x))


def _silu(x):
    return x * _sigmoid(x)


def _rms(x, g):
    return x * lax.rsqrt(jnp.mean(x * x, axis=-1, keepdims=True) + EPS) * g


def _modulate(x, g, shift, scale):
    return _rms(x, g) * (1.0 + scale) + shift


def _ada_kernel(c_ref, w_ref, b_ref, o_ref):
    a = _silu(c_ref[...]).astype(_MM)
    o_ref[0] = _dot(a, w_ref[0].astype(_MM)) + b_ref[0]


def _ada(c_all, w_ada, b_ada):
    depth, d, d6 = w_ada.shape
    r = c_all.shape[0]
    tn = d6 // 4
    return pl.pallas_call(
        _ada_kernel,
        out_shape=SDS((depth, r, d6), F32),
        grid=(depth, d6 // tn),
        in_specs=[BS((r, d), lambda l, j: (0, 0)),
                  BS((1, d, tn), lambda l, j: (l, 0, j)),
                  BS((1, 1, tn), lambda l, j: (l, 0, j))],
        out_specs=BS((1, r, tn), lambda l, j: (l, 0, j)),
        compiler_params=_cp("arbitrary", "arbitrary"),
        name="ada_mod",
    )(c_all, w_ada, b_ada.reshape(depth, 1, d6))


class _Group:
    def __init__(self, nb, t, d, mod, per_row):
        self.nb, self.t, self.d = nb, t, d
        self.n = nb * t
        self.mod = mod
        self.per_row = per_row

    def mod_arg(self, layer, tm, chunk, extra_grid=0):
        d = self.d
        if self.per_row:
            arr = self.mod[layer].reshape(self.n // tm, tm, 6 * d)
            imap = (lambda i, *_: (i, 0, chunk))
            return arr, BS((1, tm, d), imap)
        tpb = self.t // tm
        imap = (lambda i, *_: (i // tpb, 0, chunk))
        return self.mod[layer], BS((1, 1, d), imap)


def _rope_tables(pos):
    half = ROT_DIM // 2
    inv = ROPE_THETA ** (-jnp.arange(half, dtype=F32) * 2.0 / ROT_DIM)
    ang = pos[:, None] * inv[None, :]
    cos, sin = jnp.cos(ang), jnp.sin(ang)
    n = pos.shape[0]
    one = jnp.ones((n, HEAD_DIM - ROT_DIM), F32)
    zero = jnp.zeros((n, HEAD_DIM - ROT_DIM), F32)
    z8 = jnp.zeros((n, half), F32)
    c = jnp.concatenate([cos, cos, one], axis=1)
    s1 = jnp.concatenate([-sin, z8, zero], axis=1)
    s2 = jnp.concatenate([z8, sin, zero], axis=1)
    rep = LANES // HEAD_DIM
    return jnp.tile(c, (1, rep)), jnp.tile(s1, (1, rep)), jnp.tile(s2, (1, rep))


def _rope(y, c, s1, s2):
    w = y.shape[1]
    rep = w // LANES
    if rep > 1:
        c, s1, s2 = (jnp.concatenate([t] * rep, axis=1) for t in (c, s1, s2))
    half = ROT_DIM // 2
    return y * c + pltpu.roll(y, w - half, 1) * s1 + pltpu.roll(y, half, 1) * s2


def _log_sigmoid(x):
    return jnp.minimum(x, 0.0) - jnp.log(1.0 + jnp.exp(-jnp.abs(x)))


def _inproj_kernel(x_ref, g_ref, sh_ref, sc_ref, w_ref, c_ref, s1_ref, s2_ref, bf_ref, *out_refs, groups, cum_t):
    h = _modulate(x_ref[...], g_ref[...], sh_ref[0], sc_ref[0]).astype(_MM)
    c, s1, s2 = c_ref[...], s1_ref[...], s2_ref[...]
    off = 0
    for (name, width, rope, scale), o_ref in zip(groups, out_refs):
        y = _dot(h, w_ref[:, off:off + width])
        off += width
        if name == "misc":
            lane = lax.broadcasted_iota(I32, y.shape, 1)
            yr = _rope(y, c, s1, s2)
            lf = _log_sigmoid(y + bf_ref[...])
            in_fb = (lane >= MISC_FB) & (lane < MISC_FB + H_B)
            if cum_t:
                tok = lax.broadcasted_iota(I32, y.shape, 0) % cum_t
                cum = lf
                for k in range(1, cum_t):
                    cum = cum + jnp.where(tok >= k, pltpu.roll(lf, k, 0), 0.0)
                y = jnp.where((lane >= MISC_CQ) & (lane < MISC_CQ + H_B), pltpu.roll(cum, H_B, 1), y)
            y = jnp.where(lane < MISC_WI, yr, jnp.where(in_fb, lf, y))
        elif rope:
            y = _rope(y, c, s1, s2)
        if scale != 1.0:
            y = y * scale
        o_ref[...] = y.astype(o_ref.dtype)


def _pack_w_in(w_in, slots):
    d = w_in.shape[0]

    def cols(name):
        o, w = _OFF[name]
        return w_in[:, o:o + w]

    def slot(name, place):
        src = cols(name).reshape(d, -1, HEAD_DIM)
        nh = src.shape[1]
        out = jnp.zeros((d, nh, 2, HEAD_DIM), w_in.dtype)
        for hh in range(nh):
            out = out.at[:, hh, place(hh), :].set(src[:, hh, :])
        return out.reshape(d, nh * LANES)

    misc = jnp.concatenate([cols("ki"), cols("wi"), cols("fb"),
                            jnp.zeros((d, LANES - D_IDX - H_IDX - H_B), w_in.dtype)], axis=1)
    if slots:
        qa = slot("qa", lambda hh: hh // (H_A // KV_A))
        qi = slot("qi", lambda hh: 0)
        qb = slot("qb", lambda hh: hh % 2)
    else:
        qa, qi, qb = cols("qa"), cols("qi"), cols("qb")
    parts = [qa, qi, qb, cols("kb"), cols("vb"), cols("ka"), cols("va"), misc]
    groups = (("qa", qa.shape[1], True, Q_SCALE), ("qi", qi.shape[1], True, D_IDX ** -0.5),
              ("qb", qb.shape[1], False, Q_SCALE), ("kb", H_B * HEAD_DIM, False, 1.0),
              ("vb", H_B * HEAD_DIM, False, 1.0), ("ka", KV_A * HEAD_DIM, True, 1.0),
              ("va", KV_A * HEAD_DIM, False, 1.0), ("misc", LANES, False, 1.0))
    return jnp.concatenate(parts, axis=1).astype(_MM), groups


def _inproj(grp, layer, x, norm_g, w_packed, groups, b_f, tables, tm, cum_t=0):
    n, d = x.shape
    nt = n // tm
    sh_arr, sh_spec = grp.mod_arg(layer, tm, 0)
    sc_arr, sc_spec = grp.mod_arg(layer, tm, 1)
    c, s1, s2 = tables
    tr = c.shape[0] // tm
    tspec = BS((tm, LANES), lambda i: (i % tr, 0))
    bf = jnp.zeros((1, LANES), F32).at[0, MISC_FB:MISC_FB + H_B].set(b_f)
    out_dtypes = {"qa": _MM, "qi": _MM, "qb": _MM}
    out_shape = [SDS((n, w), out_dtypes.get(name, F32)) for name, w, _, _ in groups]
    out_specs = [BS((tm, w), lambda i: (i, 0)) for _, w, _, _ in groups]
    nc = w_packed.shape[1]
    return pl.pallas_call(
        functools.partial(_inproj_kernel, groups=groups, cum_t=cum_t),
        out_shape=out_shape,
        grid=(nt,),
        in_specs=[BS((tm, d), lambda i: (i, 0)), BS((1, d), lambda i: (0, 0)), sh_spec, sc_spec,
                  BS((d, nc), lambda i: (0, 0)), tspec, tspec, tspec, BS((1, LANES), lambda i: (0, 0))],
        out_specs=out_specs,
        compiler_params=_cp("arbitrary"),
        name="even_inproj",
    )(x, norm_g.reshape(1, d), sh_arr, sc_arr, w_packed, c, s1, s2, bf)


def _tri_lower(n):
    r = lax.broadcasted_iota(I32, (n, n), 0)
    c = lax.broadcasted_iota(I32, (n, n), 1)
    return jnp.where(r >= c, 1.0, 0.0).astype(F32)


def _dot_f32(a, b):
    return jnp.dot(a, b, preferred_element_type=F32, precision=lax.Precision.HIGHEST)


def _cumsum_p_kernel(lf_ref, f_ref, *, tc):
    t = lf_ref.shape[1]
    tri = _tri_lower(tc)
    carry = jnp.zeros((1, lf_ref.shape[2]), F32)
    for c in range(t // tc):
        fc = _dot_f32(tri, lf_ref[0, c * tc:(c + 1) * tc, :]) + carry
        f_ref[0, c * tc:(c + 1) * tc, :] = fc
        carry = fc[tc - 1:tc, :]


def _cumsum_prompt(logf):
    b, t, hb = logf.shape
    tc = min(256, t)
    return pl.pallas_call(
        functools.partial(_cumsum_p_kernel, tc=tc),
        out_shape=SDS((b, t, hb), F32),
        grid=(b,),
        in_specs=[BS((1, t, hb), lambda i: (i, 0, 0))],
        out_specs=BS((1, t, hb), lambda i: (i, 0, 0)),
        compiler_params=_cp("arbitrary"),
        name="cumsum_prompt",
    )(logf)


def _fox_p_kernel(q_ref, k_ref, v_ref, fq_ref, fk_ref, o_ref, m_sc, l_sc, acc_sc, fq_sc, *, tq, tk):
    i = pl.program_id(1)
    j = pl.program_id(2)
    nk = pl.num_programs(2)

    @pl.when(j == 0)
    def _():
        m_sc[...] = jnp.full(m_sc.shape, NEG_INF, F32)
        l_sc[...] = jnp.zeros(l_sc.shape, F32)
        acc_sc[...] = jnp.zeros(acc_sc.shape, F32)
        fq = fq_ref[0]
        for hh in range(H_B):
            fq_sc[hh] = fq[:, hh:hh + 1]

    def step(masked):
        k = k_ref[0].astype(_MM)
        v = v_ref[0].astype(_MM)
        fk = fk_ref[0]
        if masked:
            rows = i * tq + lax.broadcasted_iota(I32, (tq, tk), 0)
            cols = j * tk + lax.broadcasted_iota(I32, (tq, tk), 1)
            causal = cols <= rows
        lane = lax.broadcasted_iota(I32, (tq, LANES), 1)
        low = lane < HEAD_DIM
        for p in range(H_B // 2):
            kp = k[:, p * LANES:(p + 1) * LANES]
            vp = v[:, p * LANES:(p + 1) * LANES]
            alphas, pvs = [], []
            for e in range(2):
                hh = 2 * p + e
                z = _dot_nt(q_ref[0, :, hh * LANES:(hh + 1) * LANES], kp) - fk[hh:hh + 1, :]
                if masked:
                    z = jnp.where(causal, z, NEG_INF)
                fq = fq_sc[hh]
                m_prev = m_sc[hh]
                m_new = jnp.maximum(m_prev, fq + jnp.max(z, axis=1, keepdims=True))
                alpha = jnp.exp(m_prev - m_new)
                pe = jnp.exp(z + (fq - m_new))
                l_sc[hh] = alpha * l_sc[hh] + jnp.sum(pe, axis=1, keepdims=True)
                m_sc[hh] = m_new
                alphas.append(alpha)
                pvs.append(_dot(pe.astype(_MM), vp))
            acc_sc[p] = jnp.where(low, alphas[0], alphas[1]) * acc_sc[p] + jnp.where(low, pvs[0], pvs[1])

    visible = (j + 1) * tk <= i * tq + 1

    @pl.when(visible)
    def _():
        step(False)

    @pl.when(jnp.logical_not(visible) & (j * tk < (i + 1) * tq))
    def _():
        step(True)

    @pl.when(j == nk - 1)
    def _():
        lane = lax.broadcasted_iota(I32, (tq, LANES), 1)
        low = lane < HEAD_DIM
        for p in range(H_B // 2):
            linv = jnp.where(low, 1.0 / l_sc[2 * p], 1.0 / l_sc[2 * p + 1])
            o_ref[0, :, p * LANES:(p + 1) * LANES] = (acc_sc[p] * linv).astype(o_ref.dtype)


def _fox_prompt(qb_s, kb, vb, f, ft, b, t):
    tq, tk = min(FOX_TQ, t), min(FOX_TK, t)
    nq, nk = t // tq, t // tk
    dq = qb_s.shape[1]
    dk = kb.shape[1]
    kmap = lambda bb, i, j: (bb, jnp.minimum(j, ((i + 1) * tq - 1) // tk), 0)
    return pl.pallas_call(
        functools.partial(_fox_p_kernel, tq=tq, tk=tk),
        out_shape=SDS((b, t, dk), _MM),
        grid=(b, nq, nk),
        in_specs=[BS((1, tq, dq), lambda bb, i, j: (bb, i, 0)),
                  BS((1, tk, dk), kmap), BS((1, tk, dk), kmap),
                  BS((1, tq, H_B), lambda bb, i, j: (bb, i, 0)),
                  BS((1, H_B, tk), lambda bb, i, j: (bb, 0, jnp.minimum(j, ((i + 1) * tq - 1) // tk)))],
        out_specs=BS((1, tq, dk), lambda bb, i, j: (bb, i, 0)),
        scratch_shapes=[pltpu.VMEM((H_B, tq, 1), F32), pltpu.VMEM((H_B, tq, 1), F32),
                        pltpu.VMEM((H_B // 2, tq, LANES), F32), pltpu.VMEM((H_B, tq, 1), F32)],
        compiler_params=_cp("arbitrary", "arbitrary", "arbitrary"),
        name="fox_prompt",
    )(qb_s.reshape(b, t, dq), kb.reshape(b, t, dk), vb.reshape(b, t, dk), f, ft)


def _order_key(score):
    bits = pltpu.bitcast(score + 0.0, I32)
    return jnp.where(bits < 0, bits ^ jnp.int32(0x7FFFFFFF), bits)


def _kth_largest(load, nchunks, k_f, bits_per_pass=1):
    rows = load(0).shape[0]

    def count_ge(cand):
        tot = None
        for c in range(nchunks):
            x = jnp.where(load(c) >= cand, 1.0, 0.0)
            tot = x if tot is None else tot + x
        return jnp.sum(tot, axis=1, keepdims=True)

    def body(it, tau):
        shift = 32 - bits_per_pass * (it + 1)
        best = tau
        for digit in range(1, 2 ** bits_per_pass):
            cand = tau ^ jnp.left_shift(jnp.int32(digit), shift)
            best = jnp.where(count_ge(cand) >= k_f, cand, best)
        return best

    tau = lax.fori_loop(0, 32 // bits_per_pass, body, jnp.full((rows, 1), INT_MIN, I32))
    return tau, count_ge(tau)


def _select_with_ties(load, store, nchunks, width, tau, k_f):
    n_gt = None
    for c in range(nchunks):
        x = jnp.sum(jnp.where(load(c) > tau, 1.0, 0.0), axis=1, keepdims=True)
        n_gt = x if n_gt is None else n_gt + x
    need = k_f - n_gt
    r = lax.broadcasted_iota(I32, (width, width), 0)
    cc = lax.broadcasted_iota(I32, (width, width), 1)
    upper = jnp.where(r <= cc, 1.0, 0.0).astype(_MM)
    carry = jnp.zeros_like(need)
    for c in range(nchunks):
        key = load(c)
        eq = key == tau
        prefix = _dot(jnp.where(eq, 1.0, 0.0).astype(_MM), upper) + carry
        store(c, (key > tau) | (eq & (prefix <= need)))
        carry = prefix[:, width - 1:width]


def _dsa_p_kernel(qi_ref, qa_ref, mq_ref, mk_ref, ka_ref, va_ref, o_ref, key_sc, bias_sc,
                  *, tq, lk, q0, k_top, cw):
    i = pl.program_id(1)
    nch = lk // cw
    qpos = q0 + i * tq + lax.broadcasted_iota(I32, (tq, 1), 0)
    wi = mq_ref[0][:, MISC_WI:MISC_WI + H_IDX] * (H_IDX ** -0.5)
    for c in range(nch):
        kmat = mk_ref[0, c * cw:(c + 1) * cw, :].astype(_MM)
        score = jnp.zeros((tq, cw), F32)
        for hh in range(H_IDX):
            s = _dot_nt(qi_ref[0, :, hh * LANES:(hh + 1) * LANES], kmat)
            score = score + jnp.maximum(s, 0.0) * wi[:, hh:hh + 1]
        kpos = c * cw + lax.broadcasted_iota(I32, (tq, cw), 1)
        key_sc[:, c * cw:(c + 1) * cw] = jnp.where(kpos <= qpos, _order_key(score), INT_MIN)

    load = lambda c: key_sc[:, c * cw:(c + 1) * cw]
    k_f = jnp.minimum(k_top, qpos + 1).astype(F32)
    tau, cnt = _kth_largest(load, nch, k_f)
    has_tie = jnp.max(cnt - k_f) > 0.0

    def store(c, sel):
        bias_sc[:, c * cw:(c + 1) * cw] = jnp.where(sel, 0.0, NEG_INF)

    @pl.when(jnp.logical_not(has_tie))
    def _():
        for c in range(nch):
            store(c, load(c) >= tau)

    @pl.when(has_tie)
    def _():
        _select_with_ties(load, store, nch, cw, tau, k_f)

    ka = ka_ref[0].astype(_MM)
    va = va_ref[0].astype(_MM)
    bias = bias_sc[...]
    lane = lax.broadcasted_iota(I32, (tq, LANES), 1)
    for hh in range(H_A):
        grp = hh // (H_A // KV_A)
        lg = _dot_nt(qa_ref[0, :, hh * LANES:(hh + 1) * LANES], ka) + bias
        m = jnp.max(lg, axis=1, keepdims=True)
        pe = jnp.exp(lg - m)
        l = jnp.sum(pe, axis=1, keepdims=True)
        o = _dot(pe.astype(_MM), va) * (1.0 / l)
        o = jnp.where((lane >= grp * HEAD_DIM) & (lane < (grp + 1) * HEAD_DIM), o, 0.0)
        o_ref[0, :, hh * LANES:(hh + 1) * LANES] = o.astype(o_ref.dtype)


def _dsa_prompt(qi_s, qa_s, misc, ka, va, b, t, k_top):
    band = t // DSA_BANDS
    tq = min(DSA_TQ, band)
    cw = min(DSA_CHUNK, band)
    dq = qi_s.shape[1]
    qi3, qa3, misc3 = qi_s.reshape(b, t, dq), qa_s.reshape(b, t, dq), misc.reshape(b, t, LANES)
    ka3, va3 = ka.reshape(b, t, LANES), va.reshape(b, t, LANES)
    kmap = lambda bb, i: (bb, 0, 0)
    outs = []
    for c in range(DSA_BANDS):
        lk = (c + 1) * band
        q_first = c * (band // tq)
        qmap = lambda bb, i, q_first=q_first: (bb, q_first + i, 0)
        outs.append(pl.pallas_call(
            functools.partial(_dsa_p_kernel, tq=tq, lk=lk, q0=c * band, k_top=k_top, cw=cw),
            out_shape=SDS((b, band, dq), _MM),
            grid=(b, band // tq),
            in_specs=[BS((1, tq, dq), qmap), BS((1, tq, dq), qmap), BS((1, tq, LANES), qmap),
                      BS((1, lk, LANES), kmap), BS((1, lk, LANES), kmap), BS((1, lk, LANES), kmap)],
            out_specs=BS((1, tq, dq), lambda bb, i: (bb, i, 0)),
            scratch_shapes=[pltpu.VMEM((tq, lk), I32), pltpu.VMEM((tq, lk), F32)],
            compiler_params=_cp("arbitrary", "arbitrary"),
            name="dsa_prompt",
        )(qi3, qa3, misc3, misc3, ka3, va3))
    return jnp.concatenate(outs, axis=1)


def _outproj_kernel(oa_ref, ob_ref, wa_ref, wb_ref, x_ref, gate_ref, o_ref):
    y = _dot(oa_ref[...], wa_ref[...]) + _dot(ob_ref[...], wb_ref[...])
    o_ref[...] = x_ref[...] + gate_ref[0] * y


def _outproj(grp, layer, oa, ob, wa, wb, x, tm):
    n, d = x.shape
    g_arr, g_spec = grp.mod_arg(layer, tm, 2)
    da, db = oa.shape[1], ob.shape[1]
    return pl.pallas_call(
        _outproj_kernel,
        out_shape=SDS((n, d), F32),
        grid=(n // tm,),
        in_specs=[BS((tm, da), lambda i: (i, 0)), BS((tm, db), lambda i: (i, 0)),
                  BS((da, d), lambda i: (0, 0)), BS((db, d), lambda i: (0, 0)),
                  BS((tm, d), lambda i: (i, 0)), g_spec],
        out_specs=BS((tm, d), lambda i: (i, 0)),
        compiler_params=_cp("arbitrary"),
        name="even_outproj",
    )(oa, ob, wa, wb, x, g_arr)


def _ffn_kernel(x_ref, g_ref, sh_ref, sc_ref, gate_ref, wi_ref, wo_ref, gf_ref, o_ref, *, final, tf):
    ff = wo_ref.shape[0]
    x = x_ref[...]
    h = _modulate(x, g_ref[...], sh_ref[0], sc_ref[0]).astype(_MM)
    acc = None
    for c in range(ff // tf):
        gate_part = _dot(h, wi_ref[:, c * tf:(c + 1) * tf])
        up_part = _dot(h, wi_ref[:, ff + c * tf:ff + (c + 1) * tf])
        a = (_silu(gate_part) * up_part).astype(_MM)
        part = _dot(a, wo_ref[c * tf:(c + 1) * tf, :])
        acc = part if acc is None else acc + part
    y = x + gate_ref[0] * acc
    if final:
        y = _rms(y, gf_ref[...])
    o_ref[...] = y


def _ffn(grp, layer, x, norm_g, w_in, w_out, norm_final, final, tm, tf):
    n, d = x.shape
    ff = w_out.shape[0]
    sh_arr, sh_spec = grp.mod_arg(layer, tm, 3)
    sc_arr, sc_spec = grp.mod_arg(layer, tm, 4)
    g_arr, g_spec = grp.mod_arg(layer, tm, 5)
    resident = lambda shape: BS(shape, lambda i: (0, 0), pipeline_mode=pl.Buffered(1))
    return pl.pallas_call(
        functools.partial(_ffn_kernel, final=final, tf=tf),
        out_shape=SDS((n, d), F32),
        grid=(n // tm,),
        in_specs=[BS((tm, d), lambda i: (i, 0)), BS((1, d), lambda i: (0, 0)), sh_spec, sc_spec, g_spec,
                  resident((d, 2 * ff)), resident((ff, d)), BS((1, d), lambda i: (0, 0))],
        out_specs=BS((tm, d), lambda i: (i, 0)),
        compiler_params=_cp("arbitrary"),
        name="ffn",
    )(x, norm_g.reshape(1, d), sh_arr, sc_arr, g_arr, w_in, w_out, norm_final.reshape(1, d))


def _pw1_kernel(x_ref, g_ref, sh_ref, sc_ref, w_ref, b_ref, u_ref):
    h = _modulate(x_ref[...], g_ref[...], sh_ref[0], sc_ref[0]).astype(_MM)
    y = _dot(h, w_ref[...]) + b_ref[...]
    dc = y.shape[1] // 2
    u_ref[...] = y[:, :dc] * _sigmoid(y[:, dc:])


def _pw1(grp, layer, x, norm_g, w, bias, tm):
    n, d = x.shape
    dc2 = w.shape[1]
    sh_arr, sh_spec = grp.mod_arg(layer, tm, 0)
    sc_arr, sc_spec = grp.mod_arg(layer, tm, 1)
    return pl.pallas_call(
        _pw1_kernel,
        out_shape=SDS((n, dc2 // 2), F32),
        grid=(n // tm,),
        in_specs=[BS((tm, d), lambda i: (i, 0)), BS((1, d), lambda i: (0, 0)), sh_spec, sc_spec,
                  BS((d, dc2), lambda i: (0, 0)), BS((1, dc2), lambda i: (0, 0))],
        out_specs=BS((tm, dc2 // 2), lambda i: (i, 0)),
        compiler_params=_cp("arbitrary"),
        name="conv_pw1_glu",
    )(x, norm_g.reshape(1, d), sh_arr, sc_arr, w, bias.reshape(1, dc2))


def _ln_swish_pw2(z, lng, lnb, w2, b2):
    mu = jnp.mean(z, axis=-1, keepdims=True)
    zc = z - mu
    var = jnp.mean(zc * zc, axis=-1, keepdims=True)
    zn = zc * lax.rsqrt(var + EPS) * lng + lnb
    return _dot(_silu(zn).astype(_MM), w2) + b2


HALO = 32


def _conv_p_kernel(ucur_ref, uhalo_ref, wdw_ref, bdw_ref, lng_ref, lnb_ref, w2_ref, b2_ref, x_ref, gate_ref,
                   o_ref, full_sc, *, tm):
    i = pl.program_id(1)
    full_sc[0:HALO, :] = jnp.where(i > 0, uhalo_ref[0], 0.0)
    full_sc[HALO:HALO + tm, :] = ucur_ref[0]
    z = jnp.zeros((tm, full_sc.shape[1]), F32) + bdw_ref[...]
    base = HALO - (CONV_WIDTH - 1)
    for w in range(CONV_WIDTH):
        z = z + full_sc[base + w:base + w + tm, :] * wdw_ref[w:w + 1, :]
    y = _ln_swish_pw2(z, lng_ref[...], lnb_ref[...], w2_ref[...], b2_ref[...])
    o_ref[0] = x_ref[0] + gate_ref[0] * y


def _conv_prompt(grp, layer, u, x, wdw, bdw, lng, lnb, w2, b2, tm):
    b, t, d = grp.nb, grp.t, grp.d
    dc = u.shape[1]
    g_arr, _ = grp.mod_arg(layer, tm, 2)
    hb = tm // HALO
    wpad = jnp.zeros((HALO, dc), F32).at[:CONV_WIDTH].set(wdw)
    vec = lambda bb, i: (0, 0)
    return pl.pallas_call(
        functools.partial(_conv_p_kernel, tm=tm),
        out_shape=SDS((b, t, d), F32),
        grid=(b, t // tm),
        in_specs=[BS((1, tm, dc), lambda bb, i: (bb, i, 0)),
                  BS((1, HALO, dc), lambda bb, i: (bb, jnp.maximum(i * hb - 1, 0), 0)),
                  BS((HALO, dc), vec), BS((1, dc), vec), BS((1, dc), vec), BS((1, dc), vec),
                  BS((dc, d), vec), BS((1, d), vec),
                  BS((1, tm, d), lambda bb, i: (bb, i, 0)),
                  BS((1, 1, d), lambda bb, i: (bb, 0, 2))],
        out_specs=BS((1, tm, d), lambda bb, i: (bb, i, 0)),
        scratch_shapes=[pltpu.VMEM((HALO + tm, dc), F32)],
        compiler_params=_cp("arbitrary", "arbitrary"),
        name="conv_prompt",
    )(u.reshape(b, t, dc), u.reshape(b, t, dc), wpad, bdw.reshape(1, dc), lng.reshape(1, dc), lnb.reshape(1, dc),
      w2, b2.reshape(1, d), x.reshape(b, t, d), g_arr).reshape(b * t, d)


def _conv_s_kernel(buf_ref, u_ref, wdw_ref, bdw_ref, lng_ref, lnb_ref, w2_ref, b2_ref, x_ref, gate_ref, o_ref):
    nbuf = buf_ref.shape[0]
    t = u_ref.shape[0]
    zs = []
    for tt in range(t):
        z = jnp.zeros(u_ref.shape[1:], F32) + bdw_ref[...]
        for w in range(CONV_WIDTH):
            src = tt + w
            row = buf_ref[src] if src < nbuf else u_ref[src - nbuf]
            z = z + row * wdw_ref[w:w + 1, :]
        zs.append(z)
    y = _ln_swish_pw2(jnp.concatenate(zs, axis=0), lng_ref[...], lnb_ref[...], w2_ref[...], b2_ref[...])
    bb = u_ref.shape[1]
    for tt in range(t):
        o_ref[tt] = x_ref[tt] + gate_ref[...] * y[tt * bb:(tt + 1) * bb, :]


def _conv_sample(buf_t, u_t, x_t, gate, wdw, bdw, lng, lnb, w2, b2):
    nbuf, db, dc = buf_t.shape
    t, _, d = x_t.shape
    bb = min(32, db)
    wpad = jnp.zeros((HALO, dc), F32).at[:CONV_WIDTH].set(wdw)
    vec = lambda j: (0, 0)
    return pl.pallas_call(
        _conv_s_kernel,
        out_shape=SDS((t, db, d), F32),
        grid=(db // bb,),
        in_specs=[BS((nbuf, bb, dc), lambda j: (0, j, 0)), BS((t, bb, dc), lambda j: (0, j, 0)),
                  BS((HALO, dc), vec), BS((1, dc), vec), BS((1, dc), vec), BS((1, dc), vec),
                  BS((dc, d), vec), BS((1, d), vec),
                  BS((t, bb, d), lambda j: (0, j, 0)), BS((bb, d), lambda j: (j, 2))],
        out_specs=BS((t, bb, d), lambda j: (0, j, 0)),
        compiler_params=_cp("arbitrary"),
        name="conv_sample",
    )(buf_t, u_t, wpad, bdw.reshape(1, dc), lng.reshape(1, dc), lnb.reshape(1, dc), w2, b2.reshape(1, d), x_t, gate)


def _per_head_rows(x, t):
    nh, w = x.shape
    row = lax.broadcasted_iota(I32, (nh * t, w), 0) // t
    out = jnp.zeros((nh * t, w), x.dtype)
    for hh in range(nh):
        out = jnp.where(row == hh, x[hh:hh + 1, :], out)
    return out


def _strict_lower(n):
    r = lax.broadcasted_iota(I32, (n, n), 0)
    c = lax.broadcasted_iota(I32, (n, n), 1)
    return jnp.where(r > c, 1.0, 0.0).astype(F32)


def _attn_s_kernel(pt_ref, qb_ref, qa_ref, qi_ref, wi_ref, cq_ref, cqt_ref, kbn_ref, vbn_ref, kan_ref, van_ref,
                   kin_ref, *rest, past, k_top, t, gp):
    page_refs = rest[:6 * gp]
    oa_ref, ob_ref = rest[6 * gp:6 * gp + 2]
    (m_sc, l_sc, acc_sc, r_sc, kt_sc, vt_sc, ki_sc, ka_sc, va_sc, key_sc, sel_sc,
     kbn_sc, vbn_sc, kan_sc, van_sc, kin_sc) = rest[6 * gp + 2:]
    b = pl.program_id(0)
    s = pl.program_id(1)
    ns = ka_sc.shape[0]
    ps = kin_sc.shape[0]
    w = gp * ps
    rows = H_B * t
    nt8 = kbn_ref.shape[1]
    row_tok = lax.broadcasted_iota(I32, (rows, ps), 0) % t
    lane_r = lax.broadcasted_iota(I32, (rows, ps), 1)
    row8_tok = lax.broadcasted_iota(I32, (8, ps), 0) % t
    lane8 = lax.broadcasted_iota(I32, (8, ps), 1)
    wi = wi_ref[0] * (H_IDX ** -0.5)

    def dup_scores(s32):
        s32 = jnp.maximum(s32, 0.0) * wi
        sc = s32[0:t]
        for hh in range(1, H_IDX):
            sc = sc + s32[hh * t:(hh + 1) * t]
        return jnp.concatenate([sc] * (8 // t), axis=0)

    @pl.when((b == 0) & (s == 0))
    def _():
        for ref in (kbn_sc, vbn_sc, kan_sc, van_sc, kin_sc):
            ref[...] = jnp.zeros(ref.shape, F32)

    @pl.when(s == 0)
    def _():
        kbn_sc[0:nt8, :] = kbn_ref[0]
        vbn_sc[0:nt8, :] = vbn_ref[0]
        kan_sc[0:nt8, :] = kan_ref[0]
        van_sc[0:nt8, :] = van_ref[0]
        kin_sc[0:nt8, :] = kin_ref[0]
        r_sc[...] = jnp.zeros(r_sc.shape, F32)
        lg = _dot_nt(qb_ref[0], kbn_sc[...].astype(_MM)) + cq_ref[0] - _per_head_rows(cqt_ref[0], t)
        lg = jnp.where(lane_r <= row_tok, lg, NEG_INF)
        m = jnp.max(lg, axis=1, keepdims=True)
        pe = jnp.exp(lg - m)
        m_sc[...] = m
        l_sc[...] = jnp.sum(pe, axis=1, keepdims=True)
        acc_sc[...] = _dot(pe.astype(_MM), vbn_sc[...].astype(_MM))
        key_new = _order_key(dup_scores(_dot_nt(qi_ref[0], kin_sc[...].astype(_MM))))
        key_new = jnp.where(lane8 <= row8_tok, key_new, INT_MIN)
        if gp > 1:
            key_new = jnp.concatenate([key_new, jnp.full((8, w - ps), INT_MIN, I32)], axis=1)
        key_sc[ns] = key_new

    chunk = ns - 1 - s
    lfts = []
    for g in range(gp):
        kb_ref, vb_ref, ka_ref, va_ref, ki_ref, lf_ref = page_refs[6 * g:6 * g + 6]
        lanes = slice(g * ps, (g + 1) * ps)
        kt_sc[:, lanes] = kb_ref[0, 0].reshape(H_B * HEAD_DIM, ps).astype(_MM)
        vt_sc[:, lanes] = vb_ref[0, 0].reshape(H_B * HEAD_DIM, ps).astype(_MM)
        ki_sc[:, lanes] = ki_ref[0, 0].astype(_MM)
        ka_sc[chunk, :, lanes] = ka_ref[0, 0].reshape(KV_A * HEAD_DIM, ps).astype(_MM)
        va_sc[chunk, :, lanes] = va_ref[0, 0].reshape(KV_A * HEAD_DIM, ps).astype(_MM)
        lfts.append(lf_ref[0, 0].astype(F32))

    lf_all = jnp.concatenate(lfts, axis=0)
    suf_loc = _dot_f32(lf_all, _strict_lower(ps))
    carry = r_sc[...]
    sufs = [None] * gp
    for g in reversed(range(gp)):
        loc = suf_loc[g * H_B:(g + 1) * H_B]
        sufs[g] = loc + carry
        carry = carry + loc[:, 0:1] + lfts[g][:, 0:1]
    r_sc[...] = carry
    suf = jnp.concatenate(sufs, axis=1) if gp > 1 else sufs[0]

    lg = _dot(qb_ref[0], kt_sc[...]) + (_per_head_rows(suf, t) + cq_ref[0])
    m_prev = m_sc[...]
    m_new = jnp.maximum(m_prev, jnp.max(lg, axis=1, keepdims=True))
    alpha = jnp.exp(m_prev - m_new)
    pe = jnp.exp(lg - m_new)
    l_sc[...] = alpha * l_sc[...] + jnp.sum(pe, axis=1, keepdims=True)
    acc_sc[...] = alpha * acc_sc[...] + _dot_nt(pe.astype(_MM), vt_sc[...])
    m_sc[...] = m_new
    key_sc[chunk] = _order_key(dup_scores(_dot(qi_ref[0], ki_sc[...])))

    @pl.when(s == ns - 1)
    def _():
        o = acc_sc[...] * (1.0 / l_sc[...])
        lane_h = lax.broadcasted_iota(I32, o.shape, 1) // HEAD_DIM
        row_h = lax.broadcasted_iota(I32, o.shape, 0) // t
        o = jnp.where(lane_h == row_h, o, 0.0)
        ob = o[0:t]
        for hh in range(1, H_B):
            ob = ob + o[hh * t:(hh + 1) * t]
        ob_ref[0] = ob

        nch = ns + 1
        qpos = past + lax.broadcasted_iota(I32, (8, 1), 0) % t
        k_f = jnp.minimum(k_top, qpos + 1).astype(F32)
        load = lambda c: key_sc[c]
        tau, cnt = _kth_largest(load, nch, k_f, bits_per_pass=2)
        has_tie = jnp.max(cnt - k_f) > 0.0

        def store(c, sel):
            sel_sc[c] = jnp.where(sel, 0.0, NEG_INF)

        @pl.when(jnp.logical_not(has_tie))
        def _():
            for c in range(nch):
                store(c, load(c) >= tau)

        @pl.when(has_tie)
        def _():
            _select_with_ties(load, store, nch, w, tau, k_f)

        qa = qa_ref[0]
        lgs = []
        for c in range(nch):
            if c < ns:
                prod, bias = _dot(qa, ka_sc[c]), sel_sc[c]
            else:
                prod, bias = _dot_nt(qa, kan_sc[...].astype(_MM)), sel_sc[c][:, 0:ps]
            lgs.append(prod + jnp.concatenate([bias] * (rows // 8), axis=0))
        m = jnp.max(lgs[0], axis=1, keepdims=True)
        for c in range(1, nch):
            m = jnp.maximum(m, jnp.max(lgs[c], axis=1, keepdims=True))
        lsum = jnp.zeros((rows, 1), F32)
        out = jnp.zeros((rows, KV_A * HEAD_DIM), F32)
        for c in range(nch):
            pe = jnp.exp(lgs[c] - m)
            lsum = lsum + jnp.sum(pe, axis=1, keepdims=True)
            if c < ns:
                out = out + _dot_nt(pe.astype(_MM), va_sc[c])
            else:
                out = out + _dot(pe.astype(_MM), van_sc[...].astype(_MM))
        out = out * (1.0 / lsum)
        lane_j = lax.broadcasted_iota(I32, (t, KV_A * HEAD_DIM), 1) // HEAD_DIM
        for hh in range(H_A):
            piece = out[hh * t:(hh + 1) * t, :]
            oa_ref[0, :, hh * LANES:(hh + 1) * LANES] = jnp.where(lane_j == hh // (H_A // KV_A), piece, 0.0)


def _attn_sample(layer, page_table, caches_t, qbd, qabd, qi32, wi32, cq32, cqt, kbn, vbn, kan, van, kin, k_top, t):
    kb_t, vb_t, ka_t, va_t, ki_t, lf_t = caches_t
    db, npg = page_table.shape
    ps = ki_t.shape[-1]
    rows = H_B * t
    gp = PAGES_PER_STEP if npg % PAGES_PER_STEP == 0 else 1
    ns = npg // gp
    w = gp * ps
    const3 = lambda b, s, pt: (b, 0, 0)
    full = lambda a: BS((1,) + a.shape[1:], const3)

    def page_specs(g):
        def pg(b, s, pt):
            return pt[b, npg - (s + 1) * gp + g]
        five = lambda nh: BS((1, 1, nh, HEAD_DIM, ps), lambda b, s, pt: (layer, pg(b, s, pt), 0, 0, 0))
        return [five(H_B), five(H_B), five(KV_A), five(KV_A),
                BS((1, 1, D_IDX, ps), lambda b, s, pt: (layer, pg(b, s, pt), 0, 0)),
                BS((1, 1, H_B, ps), lambda b, s, pt: (layer, pg(b, s, pt), 0, 0))]

    small = [qbd, qabd, qi32, wi32, cq32, cqt, kbn, vbn, kan, van, kin]
    in_specs = [full(a) for a in small]
    pages = []
    for g in range(gp):
        in_specs += page_specs(g)
        pages += [kb_t, vb_t, ka_t, va_t, ki_t, lf_t]
    da = H_A * LANES
    db_ = H_B * HEAD_DIM
    return pl.pallas_call(
        functools.partial(_attn_s_kernel, past=npg * ps, k_top=k_top, t=t, gp=gp),
        out_shape=[SDS((db, t, da), F32), SDS((db, t, db_), F32)],
        grid_spec=pltpu.PrefetchScalarGridSpec(
            num_scalar_prefetch=1,
            grid=(db, ns),
            in_specs=in_specs,
            out_specs=[BS((1, t, da), const3), BS((1, t, db_), const3)],
            scratch_shapes=[pltpu.VMEM((rows, 1), F32), pltpu.VMEM((rows, 1), F32), pltpu.VMEM((rows, db_), F32),
                            pltpu.VMEM((H_B, 1), F32),
                            pltpu.VMEM((db_, w), _MM), pltpu.VMEM((db_, w), _MM), pltpu.VMEM((D_IDX, w), _MM),
                            pltpu.VMEM((ns, KV_A * HEAD_DIM, w), _MM), pltpu.VMEM((ns, KV_A * HEAD_DIM, w), _MM),
                            pltpu.VMEM((ns + 1, 8, w), I32), pltpu.VMEM((ns + 1, 8, w), F32),
                            pltpu.VMEM((ps, db_), F32), pltpu.VMEM((ps, db_), F32),
                            pltpu.VMEM((ps, KV_A * HEAD_DIM), F32), pltpu.VMEM((ps, KV_A * HEAD_DIM), F32),
                            pltpu.VMEM((ps, D_IDX), F32)]),
        compiler_params=_cp("arbitrary", "arbitrary"),
        name="attn_sample",
    )(page_table, *small, *pages)


def _old_attn_s_kernel(pt_ref, qb_ref, qa_ref, qi_ref, wi_ref, fq_ref, kb_ref, vb_ref, ka_ref, va_ref, ki_ref,
                   fk_ref, kbn_ref, vbn_ref, kan_ref, van_ref, kin_ref, fkn_ref, oa_ref, ob_ref,
                   m_sc, l_sc, acc_sc, ka_sc, va_sc, key_sc, sel_sc, *, past, k_top, t):
    p = pl.program_id(1)
    npg = pl.num_programs(1)
    ps = ki_ref.shape[2]
    rows = H_B * t
    wb = ps * H_B
    wa = ps * KV_A

    @pl.when(p == 0)
    def _():
        m_sc[...] = jnp.full(m_sc.shape, NEG_INF, F32)
        l_sc[...] = jnp.zeros(l_sc.shape, F32)
        acc_sc[...] = jnp.zeros(acc_sc.shape, F32)

    k2 = kb_ref[0, 0].reshape(wb, HEAD_DIM).astype(_MM)
    v2 = vb_ref[0, 0].reshape(wb, HEAD_DIM).astype(_MM)
    lt = _dot_nt(qb_ref[0], k2) + fq_ref[0] - fk_ref[0]
    r_b = lax.broadcasted_iota(I32, (rows, wb), 0)
    c_b = lax.broadcasted_iota(I32, (rows, wb), 1)
    lt = jnp.where((c_b % H_B) == (r_b // t), lt, NEG_INF)
    _online_update(lt, v2, m_sc, l_sc, acc_sc)

    def dup_scores(ki2):
        s = _dot_nt(qi_ref[0], ki2)
        s = jnp.maximum(s, 0.0) * (wi_ref[0] * (H_IDX ** -0.5))
        sc = s[0:t]
        for hh in range(1, H_IDX):
            sc = sc + s[hh * t:(hh + 1) * t]
        return jnp.concatenate([sc] * (8 // t), axis=0)

    r_e = lax.broadcasted_iota(I32, (wa, ps), 0)
    c_e = lax.broadcasted_iota(I32, (wa, ps), 1)
    expand = jnp.where((r_e // KV_A) == c_e, 1.0, 0.0).astype(_MM)
    ki2 = _dot(expand, ki_ref[0, 0].astype(_MM)).astype(_MM)
    key_sc[p] = _order_key(dup_scores(ki2))
    ka_sc[p] = ka_ref[0, 0].reshape(wa, HEAD_DIM).astype(_MM)
    va_sc[p] = va_ref[0, 0].reshape(wa, HEAD_DIM).astype(_MM)

    @pl.when(p == npg - 1)
    def _():
        nn = kbn_ref.shape[1]
        ltn = _dot_nt(qb_ref[0], kbn_ref[0].astype(_MM)) + fq_ref[0] - fkn_ref[0]
        r_n = lax.broadcasted_iota(I32, (rows, nn), 0)
        c_n = lax.broadcasted_iota(I32, (rows, nn), 1)
        ok = ((c_n % H_B) == (r_n // t)) & ((c_n // H_B) <= (r_n % t))
        _online_update(jnp.where(ok, ltn, NEG_INF), vbn_ref[0].astype(_MM), m_sc, l_sc, acc_sc)
        ob_ref[0] = acc_sc[...] * (1.0 / l_sc[...])

        nch = key_sc.shape[0]
        r8 = lax.broadcasted_iota(I32, (8, wa), 0)
        c8 = lax.broadcasted_iota(I32, (8, wa), 1)
        ok_new = (c8 // KV_A) <= (r8 % t)
        key_new = _order_key(dup_scores(kin_ref[0].astype(_MM)))
        key_sc[nch - 1] = jnp.where(ok_new, key_new, INT_MIN)
        ka_sc[nch - 1] = kan_ref[0].astype(_MM)
        va_sc[nch - 1] = van_ref[0].astype(_MM)

        qpos = past + lax.broadcasted_iota(I32, (8, 1), 0) % t
        k_f = (KV_A * jnp.minimum(k_top, qpos + 1)).astype(F32)
        load = lambda c: key_sc[c]
        tau, cnt = _kth_largest(load, nch, k_f)
        has_tie = jnp.max(cnt - k_f) > 0.0

        def store(c, sel):
            sel_sc[c] = jnp.where(sel, 0.0, NEG_INF)

        @pl.when(jnp.logical_not(has_tie))
        def _():
            for c in range(nch):
                store(c, load(c) >= tau)

        @pl.when(has_tie)
        def _():
            _select_with_ties(load, store, nch, wa, tau, k_f)

        m_sc[...] = jnp.full(m_sc.shape, NEG_INF, F32)
        l_sc[...] = jnp.zeros(l_sc.shape, F32)
        acc_sc[...] = jnp.zeros(acc_sc.shape, F32)
        r_a = lax.broadcasted_iota(I32, (rows, wa), 0)
        c_a = lax.broadcasted_iota(I32, (rows, wa), 1)
        grp_ok = (c_a % KV_A) == (r_a // (t * (H_A // KV_A)))
        for c in range(nch):
            bias = jnp.concatenate([sel_sc[c]] * (rows // 8), axis=0)
            lg = _dot_nt(qa_ref[0], ka_sc[c]) + bias
            _online_update(jnp.where(grp_ok, lg, NEG_INF), va_sc[c], m_sc, l_sc, acc_sc)
        oa_ref[0] = acc_sc[...] * (1.0 / l_sc[...])


def _old_attn_sample(layer, page_table, caches, qb32, qa32, qi32, wi32, fq32, fk_row, kbn, vbn, kan, van, kin, fkn_row,
                 k_top, t):
    cache_a_k, cache_a_v, cache_a_idx_k, cache_b_k, cache_b_v = caches
    db, npg = page_table.shape
    ps = cache_a_idx_k.shape[2]
    rows = H_B * t
    wb, wa = ps * H_B, ps * KV_A
    past = npg * ps
    qspec = BS((1, rows, HEAD_DIM), lambda b, p, pt: (b, 0, 0))
    cspec = BS((1, rows, 1), lambda b, p, pt: (b, 0, 0))
    pool5 = lambda nh: BS((1, 1, ps, nh, HEAD_DIM), lambda b, p, pt: (layer, pt[b, p], 0, 0, 0))
    newspec = lambda r: BS((1, r, HEAD_DIM), lambda b, p, pt: (b, 0, 0))
    return pl.pallas_call(
        functools.partial(_attn_s_kernel, past=past, k_top=k_top, t=t),
        out_shape=[SDS((db, rows, HEAD_DIM), F32), SDS((db, rows, HEAD_DIM), F32)],
        grid_spec=pltpu.PrefetchScalarGridSpec(
            num_scalar_prefetch=1,
            grid=(db, npg),
            in_specs=[qspec, qspec, qspec, cspec, cspec,
                      pool5(H_B), pool5(H_B), pool5(KV_A), pool5(KV_A),
                      BS((1, 1, ps, D_IDX), lambda b, p, pt: (layer, pt[b, p], 0, 0)),
                      BS((1, 1, wb), lambda b, p, pt: (b, 0, p)),
                      newspec(rows), newspec(rows), newspec(wa), newspec(wa), newspec(wa),
                      BS((1, 1, rows), lambda b, p, pt: (b, 0, 0))],
            out_specs=[BS((1, rows, HEAD_DIM), lambda b, p, pt: (b, 0, 0)),
                       BS((1, rows, HEAD_DIM), lambda b, p, pt: (b, 0, 0))],
            scratch_shapes=[pltpu.VMEM((rows, 1), F32), pltpu.VMEM((rows, 1), F32),
                            pltpu.VMEM((rows, HEAD_DIM), F32),
                            pltpu.VMEM((npg + 1, wa, HEAD_DIM), _MM), pltpu.VMEM((npg + 1, wa, HEAD_DIM), _MM),
                            pltpu.VMEM((npg + 1, 8, wa), I32), pltpu.VMEM((npg + 1, 8, wa), F32)]),
        compiler_params=_cp("arbitrary", "arbitrary"),
        name="attn_sample",
    )(page_table, qb32, qa32, qi32, wi32, fq32, cache_b_k, cache_b_v, cache_a_k, cache_a_v, cache_a_idx_k,
      fk_row, kbn, vbn, kan, van, kin, fkn_row)


def _row_tile(n, pref):
    tm = min(pref, n)
    assert n % tm == 0, (n, tm)
    return tm


def _even_prompt(grp, layer, x, w, st):
    b, t = grp.nb, grp.t
    tm = _row_tile(t, ROW_TILE)
    outs = _inproj(grp, layer, x, w["norm_mix"], w["w_in_slots"], w["groups_slots"], w["b_f"], st["rope"], tm)
    qa_s, qi_s, qb_s, kb, vb, ka, va, misc = outs
    logf = misc[:, MISC_FB:MISC_FB + H_B]
    f = _cumsum_prompt(logf.reshape(b, t, H_B))
    o_b = _fox_prompt(qb_s, kb, vb, f, jnp.swapaxes(f, 1, 2), b, t)
    o_a = _dsa_prompt(qi_s, qa_s, misc, ka, va, b, t, min(TOPK_MAX, t // 4))
    x = _outproj(grp, layer, o_a.reshape(b * t, -1), o_b.reshape(b * t, -1), w["w_out_a_slots"], w["w_out_b"], x,
                 _row_tile(grp.n, FFN_ROW_TILE))
    return x, (ka, va, misc[:, :D_IDX], kb, vb, logf)


def _even_sample(grp, layer, x, w, st):
    db, t = grp.nb, grp.t
    n = grp.n
    tm = _row_tile(n, ROW_TILE)
    assert 8 % t == 0, t
    outs = _inproj(grp, layer, x, w["norm_mix"], w["w_in_plain"], w["groups_plain"], w["b_f"], st["rope"], tm,
                   cum_t=t)
    qa, qi, qb, kb, vb, ka, va, misc = outs
    ki, wi, logf = misc[:, :D_IDX], misc[:, MISC_WI:MISC_WI + H_IDX], misc[:, MISC_FB:MISC_FB + H_B]
    cq = misc[:, MISC_CQ:MISC_CQ + H_B]
    page_table = st["page_table"]
    npg = page_table.shape[1]
    ps = st["caches_t"][0].shape[-1]

    def head_major(a, nh):
        return a.reshape(db, t, nh, -1).transpose(0, 2, 1, 3)

    eye_b = jnp.eye(H_B, dtype=qb.dtype)
    qbd = (head_major(qb, H_B)[:, :, :, None, :] * eye_b[None, :, None, :, None]).reshape(db, H_B * t, -1)
    grp_hot = (jnp.arange(H_A)[:, None] // (H_A // KV_A) == jnp.arange(KV_A)[None, :]).astype(qa.dtype)
    qabd = (head_major(qa, H_A)[:, :, :, None, :] * grp_hot[None, :, None, :, None]).reshape(db, H_A * t, -1)
    qi32 = head_major(qi, H_IDX).reshape(db, H_IDX * t, D_IDX)
    wi32 = head_major(wi, H_IDX).reshape(db, H_IDX * t, 1)
    cq32 = head_major(cq, H_B).reshape(db, H_B * t, 1)
    cqt = jnp.pad(cq.reshape(db, t, H_B).transpose(0, 2, 1), ((0, 0), (0, 0), (0, ps - t)))
    pad8 = lambda a: jnp.pad(a.reshape(db, t, -1), ((0, 0), (0, 8 - t), (0, 0)))
    o_a, o_b = _attn_sample(st["att_layer"], page_table, st["caches_t"], qbd, qabd, qi32, wi32, cq32, cqt,
                            pad8(kb), pad8(vb), pad8(ka), pad8(va), pad8(ki),
                            min(TOPK_MAX, (npg * ps + t) // 4), t)
    x = _outproj(grp, layer, o_a.reshape(n, -1).astype(_MM), o_b.reshape(n, -1).astype(_MM), w["w_out_a_slots"],
                 w["w_out_b"], x, _row_tile(n, FFN_ROW_TILE))
    return x, (ka, va, ki, kb, vb, logf)


def _odd_prompt(grp, layer, x, w, st):
    tm = _row_tile(grp.t, ROW_TILE)
    u = _pw1(grp, layer, x, w["norm_mix"], w["w_pw1"], w["b_pw1"], _row_tile(grp.n, ROW_TILE))
    x = _conv_prompt(grp, layer, u, x, w["w_dw"], w["b_dw"], w["ln_g"], w["ln_b"], w["w_pw2"], w["b_pw2"], tm)
    nb, t = grp.nb, grp.t
    state = u.reshape(nb, t, -1)[:, t - (CONV_WIDTH - 1):]
    return x, state


def _odd_sample(grp, layer, x, w, st):
    db, t, d = grp.nb, grp.t, grp.d
    u = _pw1(grp, layer, x, w["norm_mix"], w["w_pw1"], w["b_pw1"], _row_tile(grp.n, ROW_TILE))
    buf = st["state_conv"][st["conv_layer"]]
    u3 = u.reshape(db, t, -1)
    x_t = _conv_sample(jnp.swapaxes(buf, 0, 1), jnp.swapaxes(u3, 0, 1), jnp.swapaxes(x.reshape(db, t, d), 0, 1),
                       st["mod_batch"][layer], w["w_dw"], w["b_dw"], w["ln_g"], w["ln_b"], w["w_pw2"], w["b_pw2"])
    x = jnp.swapaxes(x_t, 0, 1).reshape(db * t, d)
    state = jnp.concatenate([buf.astype(F32), u3], axis=1)[:, t:]
    return x, state


def _trunk(grp, x, even_fn, odd_fn, layer_w, st, norm_final):
    depth = len(layer_w)
    att, conv = [], []
    for i in range(depth):
        w = layer_w[i]
        if i % 2 == 0:
            x, s = even_fn(grp, i, x, w, dict(st, att_layer=i // 2))
            att.append(s)
        else:
            x, s = odd_fn(grp, i, x, w, dict(st, conv_layer=i // 2))
            conv.append(s)
        tm = _row_tile(grp.n, FFN_ROW_TILE)
        x = _ffn(grp, i, x, w["norm_ffn"], w["w_ffn_in"], w["w_ffn_out"], norm_final, i == depth - 1, tm,
                 w["tf"])
    return x, att, conv


def kernel(x_prompt, x_sample, cache_a_k, cache_a_v, cache_a_idx_k, cache_b_k, cache_b_v, cache_b_logf,
           state_conv, page_table, c_prompt, c_sample, w_in_att, b_fgate, w_out_att, w_pw1, b_pw1, w_dw,
           b_dw, ln_conv_g, ln_conv_b, w_pw2, b_pw2, w_ada, b_ada, norm_mix, norm_ffn, w_ffn_in,
           w_ffn_out, norm_final):
    bp, sp, d = x_prompt.shape
    db, ts, _ = x_sample.shape
    depth = w_ada.shape[0]
    npg, ps = page_table.shape[1], cache_a_idx_k.shape[2]
    past = npg * ps

    mod = _ada(jnp.concatenate([c_prompt, c_sample], axis=0), w_ada, b_ada)
    mod_p = mod[:, :bp].reshape(depth, bp, 1, 6 * d)
    mod_b = mod[:, bp:]
    mod_s = jnp.broadcast_to(mod_b[:, :, None, :], (depth, db, ts, 6 * d)).reshape(depth, db * ts, 6 * d)
    grp_p = _Group(bp, sp, d, mod_p, per_row=False)
    grp_s = _Group(db, ts, d, mod_s, per_row=True)

    ff = w_ffn_out.shape[1]
    tf = ff // 2 if ff % (2 * LANES) == 0 else ff
    half_a = H_A * HEAD_DIM
    layer_w = []
    for i in range(depth):
        w = {"norm_mix": norm_mix[i], "norm_ffn": norm_ffn[i], "w_ffn_in": w_ffn_in[i].astype(_MM),
             "w_ffn_out": w_ffn_out[i].astype(_MM), "tf": tf}
        l = i // 2
        if i % 2 == 0:
            w["w_in_slots"], w["groups_slots"] = _pack_w_in(w_in_att[l], True)
            w["w_in_plain"], w["groups_plain"] = _pack_w_in(w_in_att[l], False)
            w["b_f"] = b_fgate[l]
            wo = w_out_att[l]
            wa = wo[:half_a].reshape(H_A, HEAD_DIM, d)
            wa_slots = jnp.zeros((H_A, 2, HEAD_DIM, d), wo.dtype)
            for hh in range(H_A):
                wa_slots = wa_slots.at[hh, hh // (H_A // KV_A)].set(wa[hh])
            w["w_out_a_slots"] = wa_slots.reshape(H_A * LANES, d).astype(_MM)
            w["w_out_a"] = wo[:half_a].astype(_MM)
            w["w_out_b"] = wo[half_a:].astype(_MM)
        else:
            w.update(w_pw1=w_pw1[l].astype(_MM), b_pw1=b_pw1[l], w_dw=w_dw[l], b_dw=b_dw[l], ln_g=ln_conv_g[l],
                     ln_b=ln_conv_b[l], w_pw2=w_pw2[l].astype(_MM), b_pw2=b_pw2[l])
        layer_w.append(w)

    st_p = {"rope": _rope_tables(jnp.arange(sp, dtype=I32).astype(F32))}
    tm_s = _row_tile(db * ts, ROW_TILE)
    pos_s = (past + (jnp.arange(tm_s, dtype=I32) % ts)).astype(F32)
    pos_last5 = lambda a: jnp.transpose(a, (0, 1, 3, 4, 2))
    pos_last4 = lambda a: jnp.transpose(a, (0, 1, 3, 2))
    caches_t = (pos_last5(cache_b_k), pos_last5(cache_b_v), pos_last5(cache_a_k), pos_last5(cache_a_v),
                pos_last4(cache_a_idx_k), pos_last4(cache_b_logf))
    st_s = {"rope": _rope_tables(pos_s), "page_table": page_table, "caches_t": caches_t,
            "state_conv": state_conv, "mod_batch": mod_b}

    y_p, att_p, conv_p = _trunk(grp_p, x_prompt.reshape(bp * sp, d), _even_prompt, _odd_prompt, layer_w, st_p,
                                norm_final)
    y_s, att_s, conv_s = _trunk(grp_s, x_sample.reshape(db * ts, d), _even_sample, _odd_sample, layer_w, st_s,
                                norm_final)

    def stack_att(att, nb, t):
        shapes = ((KV_A, HEAD_DIM), (KV_A, HEAD_DIM), (D_IDX,), (H_B, HEAD_DIM), (H_B, HEAD_DIM), (H_B,))
        return [jnp.stack([s[j].reshape(nb, t, *shapes[j]) for s in att]) for j in range(6)]

    out_p = stack_att(att_p, bp, sp)
    out_s = stack_att(att_s, db, ts)
    return (y_p.reshape(bp, sp, d), y_s.reshape(db, ts, d), *out_p, jnp.stack(conv_p),
            *out_s, jnp.stack(conv_s))
```

```python
import functools

import jax
import jax.numpy as jnp
from jax import lax
from jax.experimental import pallas as pl
from jax.experimental.pallas import tpu as pltpu

F32 = jnp.float32
I32 = jnp.int32
_MM = jnp.bfloat16

HEAD_DIM = 64
H_A = 8
KV_A = 2
H_IDX = 8
D_IDX = 64
H_B = 8
ROT_DIM = HEAD_DIM // 4
ROPE_THETA = 500000.0
TOPK_MAX = 256
CONV_WIDTH = 31
EPS = 1e-6
LANES = 128
INT_MIN = -(2 ** 31)
NEG_INF = float("-inf")
Q_SCALE = HEAD_DIM ** -0.5
VMEM_LIMIT = 56 * 1024 * 1024

ROW_TILE = 256
FFN_ROW_TILE = 512
FOX_TQ, FOX_TK = 512, 512
DSA_TQ = 256
DSA_CHUNK = 512
DSA_BANDS = 4
PAGES_PER_STEP = 8
BATCHES_PER_STEP = 2

_SPLITS = (H_A * HEAD_DIM, KV_A * HEAD_DIM, KV_A * HEAD_DIM, H_IDX * D_IDX, D_IDX, H_IDX,
           H_B * HEAD_DIM, H_B * HEAD_DIM, H_B * HEAD_DIM, H_B)
_NAMES = ("qa", "ka", "va", "qi", "ki", "wi", "qb", "kb", "vb", "fb")
_OFF = {}
_o = 0
for _n, _w in zip(_NAMES, _SPLITS):
    _OFF[_n] = (_o, _w)
    _o += _w
MISC_WI = D_IDX
MISC_FB = D_IDX + H_IDX
MISC_CQ = MISC_FB + H_B

SDS = jax.ShapeDtypeStruct
BS = pl.BlockSpec


def _cp(*sem):
    return pltpu.CompilerParams(dimension_semantics=sem, vmem_limit_bytes=VMEM_LIMIT)


def _dot(a, b):
    return jnp.dot(a, b, preferred_element_type=F32)


def _dot_nt(a, b):
    return lax.dot_general(a, b, (((1,), (1,)), ((), ())), preferred_element_type=F32)


def _sigmoid(x):
    return 1.0 / (1.0 + jnp.exp(-x))


def _silu(x):
    return x * _sigmoid(x)


def _rms(x, g):
    return x * lax.rsqrt(jnp.mean(x * x, axis=-1, keepdims=True) + EPS) * g


def _modulate(x, g, shift, scale):
    return _rms(x, g) * (1.0 + scale) + shift


def _ada_kernel(c_ref, w_ref, b_ref, o_ref):
    a = _silu(c_ref[...]).astype(_MM)
    o_ref[0] = _dot(a, w_ref[0].astype(_MM)) + b_ref[0]


def _ada(c_all, w_ada, b_ada):
    depth, d, d6 = w_ada.shape
    r = c_all.shape[0]
    tn = d6 // 4
    return pl.pallas_call(
        _ada_kernel,
        out_shape=SDS((depth, r, d6), F32),
        grid=(depth, d6 // tn),
        in_specs=[BS((r, d), lambda l, j: (0, 0)),
                  BS((1, d, tn), lambda l, j: (l, 0, j)),
                  BS((1, 1, tn), lambda l, j: (l, 0, j))],
        out_specs=BS((1, r, tn), lambda l, j: (l, 0, j)),
        compiler_params=_cp("arbitrary", "arbitrary"),
        name="ada_mod",
    )(c_all, w_ada, b_ada.reshape(depth, 1, d6))


class _Group:
    def __init__(self, nb, t, d, mod, per_row):
        self.nb, self.t, self.d = nb, t, d
        self.n = nb * t
        self.mod = mod
        self.per_row = per_row

    def mod_arg(self, layer, tm, chunk, extra_grid=0):
        d = self.d
        if self.per_row:
            arr = self.mod[layer].reshape(self.n // tm, tm, 6 * d)
            imap = (lambda i, *_: (i, 0, chunk))
            return arr, BS((1, tm, d), imap)
        tpb = self.t // tm
        imap = (lambda i, *_: (i // tpb, 0, chunk))
        return self.mod[layer], BS((1, 1, d), imap)


def _rope_tables(pos):
    half = ROT_DIM // 2
    inv = ROPE_THETA ** (-jnp.arange(half, dtype=F32) * 2.0 / ROT_DIM)
    ang = pos[:, None] * inv[None, :]
    cos, sin = jnp.cos(ang), jnp.sin(ang)
    n = pos.shape[0]
    one = jnp.ones((n, HEAD_DIM - ROT_DIM), F32)
    zero = jnp.zeros((n, HEAD_DIM - ROT_DIM), F32)
    z8 = jnp.zeros((n, half), F32)
    c = jnp.concatenate([cos, cos, one], axis=1)
    s1 = jnp.concatenate([-sin, z8, zero], axis=1)
    s2 = jnp.concatenate([z8, sin, zero], axis=1)
    rep = LANES // HEAD_DIM
    return jnp.tile(c, (1, rep)), jnp.tile(s1, (1, rep)), jnp.tile(s2, (1, rep))


def _rope(y, c, s1, s2):
    w = y.shape[1]
    rep = w // LANES
    if rep > 1:
        c, s1, s2 = (jnp.concatenate([t] * rep, axis=1) for t in (c, s1, s2))
    half = ROT_DIM // 2
    return y * c + pltpu.roll(y, w - half, 1) * s1 + pltpu.roll(y, half, 1) * s2


def _log_sigmoid(x):
    return jnp.minimum(x, 0.0) - jnp.log(1.0 + jnp.exp(-jnp.abs(x)))


def _inproj_kernel(x_ref, g_ref, sh_ref, sc_ref, w_ref, c_ref, s1_ref, s2_ref, bf_ref, *out_refs, groups, cum_t):
    h = _modulate(x_ref[...], g_ref[...], sh_ref[0], sc_ref[0]).astype(_MM)
    c, s1, s2 = c_ref[...], s1_ref[...], s2_ref[...]
    off = 0
    for (name, width, rope, scale), o_ref in zip(groups, out_refs):
        y = _dot(h, w_ref[:, off:off + width])
        off += width
        if name == "misc":
            lane = lax.broadcasted_iota(I32, y.shape, 1)
            yr = _rope(y, c, s1, s2)
            lf = _log_sigmoid(y + bf_ref[...])
            in_fb = (lane >= MISC_FB) & (lane < MISC_FB + H_B)
            if cum_t:
                tok = lax.broadcasted_iota(I32, y.shape, 0) % cum_t
                cum = lf
                for k in range(1, cum_t):
                    cum = cum + jnp.where(tok >= k, pltpu.roll(lf, k, 0), 0.0)
                y = jnp.where((lane >= MISC_CQ) & (lane < MISC_CQ + H_B), pltpu.roll(cum, H_B, 1), y)
            y = jnp.where(lane < MISC_WI, yr, jnp.where(in_fb, lf, y))
        elif rope:
            y = _rope(y, c, s1, s2)
        if scale != 1.0:
            y = y * scale
        o_ref[...] = y.astype(o_ref.dtype)


def _pack_w_in(w_in, slots):
    d = w_in.shape[0]

    def cols(name):
        o, w = _OFF[name]
        return w_in[:, o:o + w]

    def slot(name, place):
        src = cols(name).reshape(d, -1, HEAD_DIM)
        nh = src.shape[1]
        out = jnp.zeros((d, nh, 2, HEAD_DIM), w_in.dtype)
        for hh in range(nh):
            out = out.at[:, hh, place(hh), :].set(src[:, hh, :])
        return out.reshape(d, nh * LANES)

    misc = jnp.concatenate([cols("ki"), cols("wi"), cols("fb"),
                            jnp.zeros((d, LANES - D_IDX - H_IDX - H_B), w_in.dtype)], axis=1)
    if slots:
        qa = slot("qa", lambda hh: hh // (H_A // KV_A))
        qi = slot("qi", lambda hh: 0)
        qb = slot("qb", lambda hh: hh % 2)
    else:
        qa, qi, qb = cols("qa"), cols("qi"), cols("qb")
    parts = [qa, qi, qb, cols("kb"), cols("vb"), cols("ka"), cols("va"), misc]
    groups = (("qa", qa.shape[1], True, Q_SCALE), ("qi", qi.shape[1], True, D_IDX ** -0.5),
              ("qb", qb.shape[1], False, Q_SCALE), ("kb", H_B * HEAD_DIM, False, 1.0),
              ("vb", H_B * HEAD_DIM, False, 1.0), ("ka", KV_A * HEAD_DIM, True, 1.0),
              ("va", KV_A * HEAD_DIM, False, 1.0), ("misc", LANES, False, 1.0))
    return jnp.concatenate(parts, axis=1).astype(_MM), groups


def _inproj(grp, layer, x, norm_g, w_packed, groups, b_f, tables, tm, cum_t=0):
    n, d = x.shape
    nt = n // tm
    sh_arr, sh_spec = grp.mod_arg(layer, tm, 0)
    sc_arr, sc_spec = grp.mod_arg(layer, tm, 1)
    c, s1, s2 = tables
    tr = c.shape[0] // tm
    tspec = BS((tm, LANES), lambda i: (i % tr, 0))
    bf = jnp.zeros((1, LANES), F32).at[0, MISC_FB:MISC_FB + H_B].set(b_f)
    out_dtypes = {"qa": _MM, "qi": _MM, "qb": _MM}
    out_shape = [SDS((n, w), out_dtypes.get(name, F32)) for name, w, _, _ in groups]
    out_specs = [BS((tm, w), lambda i: (i, 0)) for _, w, _, _ in groups]
    nc = w_packed.shape[1]
    return pl.pallas_call(
        functools.partial(_inproj_kernel, groups=groups, cum_t=cum_t),
        out_shape=out_shape,
        grid=(nt,),
        in_specs=[BS((tm, d), lambda i: (i, 0)), BS((1, d), lambda i: (0, 0)), sh_spec, sc_spec,
                  BS((d, nc), lambda i: (0, 0)), tspec, tspec, tspec, BS((1, LANES), lambda i: (0, 0))],
        out_specs=out_specs,
        compiler_params=_cp("arbitrary"),
        name="even_inproj",
    )(x, norm_g.reshape(1, d), sh_arr, sc_arr, w_packed, c, s1, s2, bf)


def _tri_lower(n):
    r = lax.broadcasted_iota(I32, (n, n), 0)
    c = lax.broadcasted_iota(I32, (n, n), 1)
    return jnp.where(r >= c, 1.0, 0.0).astype(F32)


def _dot_f32(a, b):
    return jnp.dot(a, b, preferred_element_type=F32, precision=lax.Precision.HIGHEST)


def _cumsum_p_kernel(lf_ref, f_ref, *, tc):
    t = lf_ref.shape[1]
    tri = _tri_lower(tc)
    carry = jnp.zeros((1, lf_ref.shape[2]), F32)
    for c in range(t // tc):
        fc = _dot_f32(tri, lf_ref[0, c * tc:(c + 1) * tc, :]) + carry
        f_ref[0, c * tc:(c + 1) * tc, :] = fc
        carry = fc[tc - 1:tc, :]


def _cumsum_prompt(logf):
    b, t, hb = logf.shape
    tc = min(256, t)
    return pl.pallas_call(
        functools.partial(_cumsum_p_kernel, tc=tc),
        out_shape=SDS((b, t, hb), F32),
        grid=(b,),
        in_specs=[BS((1, t, hb), lambda i: (i, 0, 0))],
        out_specs=BS((1, t, hb), lambda i: (i, 0, 0)),
        compiler_params=_cp("arbitrary"),
        name="cumsum_prompt",
    )(logf)


def _fox_p_kernel(q_ref, k_ref, v_ref, fq_ref, fk_ref, o_ref, m_sc, l_sc, acc_sc, fq_sc, *, tq, tk):
    i = pl.program_id(1)
    j = pl.program_id(2)
    nk = pl.num_programs(2)

    @pl.when(j == 0)
    def _():
        m_sc[...] = jnp.full(m_sc.shape, NEG_INF, F32)
        l_sc[...] = jnp.zeros(l_sc.shape, F32)
        acc_sc[...] = jnp.zeros(acc_sc.shape, F32)
        fq = fq_ref[0]
        for hh in range(H_B):
            fq_sc[hh] = jnp.broadcast_to(fq[:, hh:hh + 1], (tq, LANES))

    def step(masked):
        k = k_ref[0].astype(_MM)
        v = v_ref[0].astype(_MM)
        fk = fk_ref[0]
        if masked:
            rows = i * tq + lax.broadcasted_iota(I32, (tq, tk), 0)
            cols = j * tk + lax.broadcasted_iota(I32, (tq, tk), 1)
            causal = cols <= rows
        lane = lax.broadcasted_iota(I32, (tq, LANES), 1)
        low = lane < HEAD_DIM
        for p in range(H_B // 2):
            kp = k[:, p * LANES:(p + 1) * LANES]
            vp = v[:, p * LANES:(p + 1) * LANES]
            alphas, pvs = [], []
            for e in range(2):
                hh = 2 * p + e
                z = _dot_nt(q_ref[0, :, hh * LANES:(hh + 1) * LANES], kp) - fk[hh:hh + 1, :]
                if masked:
                    z = jnp.where(causal, z, NEG_INF)
                fq = fq_sc[hh]
                m_prev = m_sc[hh]
                m_new = jnp.maximum(m_prev, fq + jnp.max(z, axis=1, keepdims=True))
                alpha = jnp.exp(m_prev - m_new)
                pe = jnp.exp(z + jnp.concatenate([fq - m_new] * (tk // LANES), axis=1))
                l_sc[hh] = alpha * l_sc[hh] + jnp.sum(pe, axis=1, keepdims=True)
                m_sc[hh] = m_new
                alphas.append(alpha)
                pvs.append(_dot(pe.astype(_MM), vp))
            acc_sc[p] = jnp.where(low, alphas[0], alphas[1]) * acc_sc[p] + jnp.where(low, pvs[0], pvs[1])

    visible = (j + 1) * tk <= i * tq + 1

    @pl.when(visible)
    def _():
        step(False)

    @pl.when(jnp.logical_not(visible) & (j * tk < (i + 1) * tq))
    def _():
        step(True)

    @pl.when(j == nk - 1)
    def _():
        lane = lax.broadcasted_iota(I32, (tq, LANES), 1)
        low = lane < HEAD_DIM
        for p in range(H_B // 2):
            linv = jnp.where(low, 1.0 / l_sc[2 * p], 1.0 / l_sc[2 * p + 1])
            o_ref[0, :, p * LANES:(p + 1) * LANES] = (acc_sc[p] * linv).astype(o_ref.dtype)


def _fox_prompt(qb_s, kb, vb, f, ft, b, t):
    tq, tk = min(FOX_TQ, t), min(FOX_TK, t)
    nq, nk = t // tq, t // tk
    dq = qb_s.shape[1]
    dk = kb.shape[1]
    kmap = lambda bb, i, j: (bb, jnp.minimum(j, ((i + 1) * tq - 1) // tk), 0)
    return pl.pallas_call(
        functools.partial(_fox_p_kernel, tq=tq, tk=tk),
        out_shape=SDS((b, t, dk), _MM),
        grid=(b, nq, nk),
        in_specs=[BS((1, tq, dq), lambda bb, i, j: (bb, i, 0)),
                  BS((1, tk, dk), kmap), BS((1, tk, dk), kmap),
                  BS((1, tq, H_B), lambda bb, i, j: (bb, i, 0)),
                  BS((1, H_B, tk), lambda bb, i, j: (bb, 0, jnp.minimum(j, ((i + 1) * tq - 1) // tk)))],
        out_specs=BS((1, tq, dk), lambda bb, i, j: (bb, i, 0)),
        scratch_shapes=[pltpu.VMEM((H_B, tq, LANES), F32), pltpu.VMEM((H_B, tq, LANES), F32),
                        pltpu.VMEM((H_B // 2, tq, LANES), F32), pltpu.VMEM((H_B, tq, LANES), F32)],
        compiler_params=_cp("arbitrary", "arbitrary", "arbitrary"),
        name="fox_prompt",
    )(qb_s.reshape(b, t, dq), kb.reshape(b, t, dk), vb.reshape(b, t, dk), f, ft)


def _order_key(score):
    bits = pltpu.bitcast(score + 0.0, I32)
    return jnp.where(bits < 0, bits ^ jnp.int32(0x7FFFFFFF), bits)


def _kth_largest(load, nchunks, k_f, bits_per_pass=1):
    rows = load(0).shape[0]

    def count_ge(cand):
        tot = None
        for c in range(nchunks):
            x = jnp.where(load(c) >= cand, 1.0, 0.0)
            tot = x if tot is None else tot + x
        return jnp.sum(tot, axis=1, keepdims=True)

    def body(it, tau):
        shift = 32 - bits_per_pass * (it + 1)
        best = tau
        for digit in range(1, 2 ** bits_per_pass):
            cand = tau ^ jnp.left_shift(jnp.int32(digit), shift)
            best = jnp.where(count_ge(cand) >= k_f, cand, best)
        return best

    tau = lax.fori_loop(0, 32 // bits_per_pass, body, jnp.full((rows, 1), INT_MIN, I32))
    return tau, count_ge(tau)


def _select_with_ties(load, store, nchunks, width, tau, k_f):
    n_gt = None
    for c in range(nchunks):
        x = jnp.sum(jnp.where(load(c) > tau, 1.0, 0.0), axis=1, keepdims=True)
        n_gt = x if n_gt is None else n_gt + x
    need = k_f - n_gt
    r = lax.broadcasted_iota(I32, (width, width), 0)
    cc = lax.broadcasted_iota(I32, (width, width), 1)
    upper = jnp.where(r <= cc, 1.0, 0.0).astype(_MM)
    carry = jnp.zeros_like(need)
    for c in range(nchunks):
        key = load(c)
        eq = key == tau
        prefix = _dot(jnp.where(eq, 1.0, 0.0).astype(_MM), upper) + carry
        store(c, (key > tau) | (eq & (prefix <= need)))
        carry = prefix[:, width - 1:width]


def _dsa_p_kernel(qi_ref, qa_ref, mq_ref, mk_ref, ka_ref, va_ref, o_ref, key_sc, bias_sc,
                  *, tq, lk, q0, k_top, cw):
    i = pl.program_id(1)
    nch = lk // cw
    qpos = q0 + i * tq + lax.broadcasted_iota(I32, (tq, 1), 0)
    wi = mq_ref[0][:, MISC_WI:MISC_WI + H_IDX] * (H_IDX ** -0.5)
    for c in range(nch):
        kmat = mk_ref[0, c * cw:(c + 1) * cw, :].astype(_MM)
        score = jnp.zeros((tq, cw), F32)
        for hh in range(H_IDX):
            s = _dot_nt(qi_ref[0, :, hh * LANES:(hh + 1) * LANES], kmat)
            score = score + jnp.maximum(s, 0.0) * wi[:, hh:hh + 1]
        kpos = c * cw + lax.broadcasted_iota(I32, (tq, cw), 1)
        key_sc[:, c * cw:(c + 1) * cw] = jnp.where(kpos <= qpos, _order_key(score), INT_MIN)

    load = lambda c: key_sc[:, c * cw:(c + 1) * cw]
    k_f = jnp.minimum(k_top, qpos + 1).astype(F32)
    tau, cnt = _kth_largest(load, nch, k_f)
    has_tie = jnp.max(cnt - k_f) > 0.0

    def store(c, sel):
        bias_sc[:, c * cw:(c + 1) * cw] = jnp.where(sel, 0.0, NEG_INF)

    @pl.when(jnp.logical_not(has_tie))
    def _():
        for c in range(nch):
            store(c, load(c) >= tau)

    @pl.when(has_tie)
    def _():
        _select_with_ties(load, store, nch, cw, tau, k_f)

    ka = ka_ref[0].astype(_MM)
    va = va_ref[0].astype(_MM)
    bias = bias_sc[...]
    lane = lax.broadcasted_iota(I32, (tq, LANES), 1)
    for hh in range(H_A):
        grp = hh // (H_A // KV_A)
        lg = _dot_nt(qa_ref[0, :, hh * LANES:(hh + 1) * LANES], ka) + bias
        m = jnp.max(lg, axis=1, keepdims=True)
        pe = jnp.exp(lg - m)
        l = jnp.sum(pe, axis=1, keepdims=True)
        o = _dot(pe.astype(_MM), va) * (1.0 / l)
        o = jnp.where((lane >= grp * HEAD_DIM) & (lane < (grp + 1) * HEAD_DIM), o, 0.0)
        o_ref[0, :, hh * LANES:(hh + 1) * LANES] = o.astype(o_ref.dtype)


def _dsa_prompt(qi_s, qa_s, misc, ka, va, b, t, k_top):
    band = t // DSA_BANDS
    tq = min(DSA_TQ, band)
    cw = min(DSA_CHUNK, band)
    dq = qi_s.shape[1]
    qi3, qa3, misc3 = qi_s.reshape(b, t, dq), qa_s.reshape(b, t, dq), misc.reshape(b, t, LANES)
    ka3, va3 = ka.reshape(b, t, LANES), va.reshape(b, t, LANES)
    kmap = lambda bb, i: (bb, 0, 0)
    outs = []
    for c in range(DSA_BANDS):
        lk = (c + 1) * band
        q_first = c * (band // tq)
        qmap = lambda bb, i, q_first=q_first: (bb, q_first + i, 0)
        outs.append(pl.pallas_call(
            functools.partial(_dsa_p_kernel, tq=tq, lk=lk, q0=c * band, k_top=k_top, cw=cw),
            out_shape=SDS((b, band, dq), _MM),
            grid=(b, band // tq),
            in_specs=[BS((1, tq, dq), qmap), BS((1, tq, dq), qmap), BS((1, tq, LANES), qmap),
                      BS((1, lk, LANES), kmap), BS((1, lk, LANES), kmap), BS((1, lk, LANES), kmap)],
            out_specs=BS((1, tq, dq), lambda bb, i: (bb, i, 0)),
            scratch_shapes=[pltpu.VMEM((tq, lk), I32), pltpu.VMEM((tq, lk), F32)],
            compiler_params=_cp("arbitrary", "arbitrary"),
            name="dsa_prompt",
        )(qi3, qa3, misc3, misc3, ka3, va3))
    return jnp.concatenate(outs, axis=1)


def _outproj_kernel(oa_ref, ob_ref, wa_ref, wb_ref, x_ref, gate_ref, o_ref):
    y = _dot(oa_ref[...], wa_ref[...]) + _dot(ob_ref[...], wb_ref[...])
    o_ref[...] = x_ref[...] + gate_ref[0] * y


def _outproj(grp, layer, oa, ob, wa, wb, x, tm):
    n, d = x.shape
    g_arr, g_spec = grp.mod_arg(layer, tm, 2)
    da, db = oa.shape[1], ob.shape[1]
    return pl.pallas_call(
        _outproj_kernel,
        out_shape=SDS((n, d), F32),
        grid=(n // tm,),
        in_specs=[BS((tm, da), lambda i: (i, 0)), BS((tm, db), lambda i: (i, 0)),
                  BS((da, d), lambda i: (0, 0)), BS((db, d), lambda i: (0, 0)),
                  BS((tm, d), lambda i: (i, 0)), g_spec],
        out_specs=BS((tm, d), lambda i: (i, 0)),
        compiler_params=_cp("arbitrary"),
        name="even_outproj",
    )(oa, ob, wa, wb, x, g_arr)


def _ffn_kernel(x_ref, g_ref, sh_ref, sc_ref, gate_ref, wi_ref, wo_ref, gf_ref, o_ref, *, final, tf):
    ff = wo_ref.shape[0]
    x = x_ref[...]
    h = _modulate(x, g_ref[...], sh_ref[0], sc_ref[0]).astype(_MM)
    acc = None
    for c in range(ff // tf):
        gate_part = _dot(h, wi_ref[:, c * tf:(c + 1) * tf])
        up_part = _dot(h, wi_ref[:, ff + c * tf:ff + (c + 1) * tf])
        a = (_silu(gate_part) * up_part).astype(_MM)
        part = _dot(a, wo_ref[c * tf:(c + 1) * tf, :])
        acc = part if acc is None else acc + part
    y = x + gate_ref[0] * acc
    if final:
        y = _rms(y, gf_ref[...])
    o_ref[...] = y


def _ffn(grp, layer, x, norm_g, w_in, w_out, norm_final, final, tm, tf):
    n, d = x.shape
    ff = w_out.shape[0]
    sh_arr, sh_spec = grp.mod_arg(layer, tm, 3)
    sc_arr, sc_spec = grp.mod_arg(layer, tm, 4)
    g_arr, g_spec = grp.mod_arg(layer, tm, 5)
    resident = lambda shape: BS(shape, lambda i: (0, 0), pipeline_mode=pl.Buffered(1))
    return pl.pallas_call(
        functools.partial(_ffn_kernel, final=final, tf=tf),
        out_shape=SDS((n, d), F32),
        grid=(n // tm,),
        in_specs=[BS((tm, d), lambda i: (i, 0)), BS((1, d), lambda i: (0, 0)), sh_spec, sc_spec, g_spec,
                  resident((d, 2 * ff)), resident((ff, d)), BS((1, d), lambda i: (0, 0))],
        out_specs=BS((tm, d), lambda i: (i, 0)),
        compiler_params=_cp("arbitrary"),
        name="ffn",
    )(x, norm_g.reshape(1, d), sh_arr, sc_arr, g_arr, w_in, w_out, norm_final.reshape(1, d))


def _pw1_kernel(x_ref, g_ref, sh_ref, sc_ref, w_ref, b_ref, u_ref):
    h = _modulate(x_ref[...], g_ref[...], sh_ref[0], sc_ref[0]).astype(_MM)
    y = _dot(h, w_ref[...]) + b_ref[...]
    dc = y.shape[1] // 2
    u_ref[...] = y[:, :dc] * _sigmoid(y[:, dc:])


def _pw1(grp, layer, x, norm_g, w, bias, tm):
    n, d = x.shape
    dc2 = w.shape[1]
    sh_arr, sh_spec = grp.mod_arg(layer, tm, 0)
    sc_arr, sc_spec = grp.mod_arg(layer, tm, 1)
    return pl.pallas_call(
        _pw1_kernel,
        out_shape=SDS((n, dc2 // 2), F32),
        grid=(n // tm,),
        in_specs=[BS((tm, d), lambda i: (i, 0)), BS((1, d), lambda i: (0, 0)), sh_spec, sc_spec,
                  BS((d, dc2), lambda i: (0, 0)), BS((1, dc2), lambda i: (0, 0))],
        out_specs=BS((tm, dc2 // 2), lambda i: (i, 0)),
        compiler_params=_cp("arbitrary"),
        name="conv_pw1_glu",
    )(x, norm_g.reshape(1, d), sh_arr, sc_arr, w, bias.reshape(1, dc2))


def _ln_swish_pw2(z, lng, lnb, w2, b2):
    mu = jnp.mean(z, axis=-1, keepdims=True)
    zc = z - mu
    var = jnp.mean(zc * zc, axis=-1, keepdims=True)
    zn = zc * lax.rsqrt(var + EPS) * lng + lnb
    return _dot(_silu(zn).astype(_MM), w2) + b2


HALO = 32


SUBLANES = 8
CONV_ROWS = 32


def _conv_p_kernel(ucur_ref, uhalo_ref, wdw_ref, bdw_ref, lng_ref, lnb_ref, w2_ref, b2_ref, x_ref, gate_ref,
                   o_ref, full_sc, shift_sc, z_sc, *, tm):
    i = pl.program_id(1)
    dc = full_sc.shape[1]
    full_sc[0:HALO, :] = jnp.where(i > 0, uhalo_ref[0], 0.0)
    full_sc[HALO:HALO + tm, :] = ucur_ref[0]
    span = shift_sc.shape[1]
    for q in range(1, SUBLANES):
        shift_sc[q - 1] = full_sc[q:q + span, :]
    base = HALO - (CONV_WIDTH - 1)

    def chunk(c, carry):
        r0 = pl.multiple_of(c * CONV_ROWS, CONV_ROWS)
        groups = CONV_ROWS // SUBLANES
        z = jnp.zeros((groups, SUBLANES, dc), F32) + bdw_ref[...]
        for w in range(CONV_WIDTH):
            q, al = (base + w) % SUBLANES, ((base + w) // SUBLANES) * SUBLANES
            src = full_sc if q == 0 else shift_sc.at[q - 1]
            slab = src[pl.ds(r0 + al, CONV_ROWS), :].reshape(groups, SUBLANES, dc)
            z = z + slab * wdw_ref[w][None]
        z_sc[pl.ds(r0, CONV_ROWS), :] = z.reshape(CONV_ROWS, dc)
        return carry

    lax.fori_loop(0, tm // CONV_ROWS, chunk, 0)
    y = _ln_swish_pw2(z_sc[...], lng_ref[...], lnb_ref[...], w2_ref[...], b2_ref[...])
    o_ref[0] = x_ref[0] + gate_ref[0] * y


def _conv_prompt(grp, layer, u, x, wdw, bdw, lng, lnb, w2, b2, tm):
    b, t, d = grp.nb, grp.t, grp.d
    dc = u.shape[1]
    g_arr, _ = grp.mod_arg(layer, tm, 2)
    hb = tm // HALO
    wrep = jnp.broadcast_to(wdw[:, None, :], (CONV_WIDTH, SUBLANES, dc))
    vec = lambda bb, i: (0, 0)
    return pl.pallas_call(
        functools.partial(_conv_p_kernel, tm=tm),
        out_shape=SDS((b, t, d), F32),
        grid=(b, t // tm),
        in_specs=[BS((1, tm, dc), lambda bb, i: (bb, i, 0)),
                  BS((1, HALO, dc), lambda bb, i: (bb, jnp.maximum(i * hb - 1, 0), 0)),
                  BS((CONV_WIDTH, SUBLANES, dc), lambda bb, i: (0, 0, 0)),
                  BS((1, dc), vec), BS((1, dc), vec), BS((1, dc), vec),
                  BS((dc, d), vec), BS((1, d), vec),
                  BS((1, tm, d), lambda bb, i: (bb, i, 0)),
                  BS((1, 1, d), lambda bb, i: (bb, 0, 2))],
        out_specs=BS((1, tm, d), lambda bb, i: (bb, i, 0)),
        scratch_shapes=[pltpu.VMEM((HALO + tm, dc), F32),
                        pltpu.VMEM((SUBLANES - 1, HALO + tm - SUBLANES, dc), F32),
                        pltpu.VMEM((tm, dc), F32)],
        compiler_params=_cp("arbitrary", "arbitrary"),
        name="conv_prompt",
    )(u.reshape(b, t, dc), u.reshape(b, t, dc), wrep, bdw.reshape(1, dc), lng.reshape(1, dc), lnb.reshape(1, dc),
      w2, b2.reshape(1, d), x.reshape(b, t, d), g_arr).reshape(b * t, d)


def _conv_s_kernel(buf_ref, u_ref, wdw_ref, bdw_ref, lng_ref, lnb_ref, w2_ref, b2_ref, x_ref, gate_ref, o_ref):
    nbuf = buf_ref.shape[0]
    t = u_ref.shape[0]
    zs = []
    for tt in range(t):
        z = jnp.zeros(u_ref.shape[1:], F32) + bdw_ref[...]
        for w in range(CONV_WIDTH):
            src = tt + w
            row = buf_ref[src] if src < nbuf else u_ref[src - nbuf]
            z = z + row * wdw_ref[w:w + 1, :]
        zs.append(z)
    y = _ln_swish_pw2(jnp.concatenate(zs, axis=0), lng_ref[...], lnb_ref[...], w2_ref[...], b2_ref[...])
    bb = u_ref.shape[1]
    for tt in range(t):
        o_ref[tt] = x_ref[tt] + gate_ref[...] * y[tt * bb:(tt + 1) * bb, :]


def _conv_sample(buf_t, u_t, x_t, gate, wdw, bdw, lng, lnb, w2, b2):
    nbuf, db, dc = buf_t.shape
    t, _, d = x_t.shape
    bb = min(32, db)
    wpad = jnp.zeros((HALO, dc), F32).at[:CONV_WIDTH].set(wdw)
    vec = lambda j: (0, 0)
    return pl.pallas_call(
        _conv_s_kernel,
        out_shape=SDS((t, db, d), F32),
        grid=(db // bb,),
        in_specs=[BS((nbuf, bb, dc), lambda j: (0, j, 0)), BS((t, bb, dc), lambda j: (0, j, 0)),
                  BS((HALO, dc), vec), BS((1, dc), vec), BS((1, dc), vec), BS((1, dc), vec),
                  BS((dc, d), vec), BS((1, d), vec),
                  BS((t, bb, d), lambda j: (0, j, 0)), BS((bb, d), lambda j: (j, 2))],
        out_specs=BS((t, bb, d), lambda j: (0, j, 0)),
        compiler_params=_cp("arbitrary"),
        name="conv_sample",
    )(buf_t, u_t, wpad, bdw.reshape(1, dc), lng.reshape(1, dc), lnb.reshape(1, dc), w2, b2.reshape(1, d), x_t, gate)


def _per_head_rows(x, t):
    nh, w = x.shape
    row = lax.broadcasted_iota(I32, (nh * t, w), 0) // t
    out = jnp.zeros((nh * t, w), x.dtype)
    for hh in range(nh):
        out = jnp.where(row == hh, x[hh:hh + 1, :], out)
    return out


def _strict_lower(n):
    r = lax.broadcasted_iota(I32, (n, n), 0)
    c = lax.broadcasted_iota(I32, (n, n), 1)
    return jnp.where(r > c, 1.0, 0.0).astype(F32)


def _attn_s_kernel(pt_ref, qb_ref, qa_ref, qi_ref, wi_ref, cq_ref, cqt_ref, kbn_ref, vbn_ref, kan_ref, van_ref,
                   kin_ref, *rest, past, k_top, t, gp, nb):
    npage = 6 * gp * nb
    page_refs = rest[:npage]
    oa_ref, ob_ref = rest[npage:npage + 2]
    (m_sc, l_sc, acc_sc, r_sc, ka_sc, va_sc, key_sc, sel_sc,
     kbn_sc, vbn_sc, kan_sc, van_sc, kin_sc) = rest[npage + 2:]
    first = (pl.program_id(0) == 0) & (pl.program_id(1) == 0)
    s = pl.program_id(1)
    ns = ka_sc.shape[1]
    ps = kin_sc.shape[1]
    w = gp * ps
    rows = H_B * t
    nt8 = kbn_ref.shape[1]
    row_tok = lax.broadcasted_iota(I32, (rows, ps), 0) % t
    lane_r = lax.broadcasted_iota(I32, (rows, ps), 1)
    row8_tok = lax.broadcasted_iota(I32, (8, ps), 0) % t
    lane8 = lax.broadcasted_iota(I32, (8, ps), 1)
    chunk = ns - 1 - s

    def dup_scores(s32, bi):
        s32 = jnp.maximum(s32, 0.0) * (wi_ref[bi] * (H_IDX ** -0.5))
        sc = s32[0:t]
        for hh in range(1, H_IDX):
            sc = sc + s32[hh * t:(hh + 1) * t]
        return jnp.concatenate([sc] * (8 // t), axis=0)

    def key_rows(bi):
        return slice(8 * bi, 8 * (bi + 1))

    @pl.when(first)
    def _():
        for ref in (kbn_sc, vbn_sc, kan_sc, van_sc, kin_sc):
            ref[...] = jnp.zeros(ref.shape, F32)

    def new_rows(bi):
        kbn_sc[bi, 0:nt8, :] = kbn_ref[bi]
        vbn_sc[bi, 0:nt8, :] = vbn_ref[bi]
        kan_sc[bi, 0:nt8, :] = kan_ref[bi]
        van_sc[bi, 0:nt8, :] = van_ref[bi]
        kin_sc[bi, 0:nt8, :] = kin_ref[bi]
        r_sc[bi] = jnp.zeros(r_sc.shape[1:], F32)
        lg = _dot_nt(qb_ref[bi], kbn_sc[bi].astype(_MM)) + cq_ref[bi] - _per_head_rows(cqt_ref[bi], t)
        lg = jnp.where(lane_r <= row_tok, lg, NEG_INF)
        m = jnp.max(lg, axis=1, keepdims=True)
        pe = jnp.exp(lg - m)
        m_sc[bi] = m
        l_sc[bi] = jnp.sum(pe, axis=1, keepdims=True)
        acc_sc[bi] = _dot(pe.astype(_MM), vbn_sc[bi].astype(_MM))
        key_new = _order_key(dup_scores(_dot_nt(qi_ref[bi], kin_sc[bi].astype(_MM)), bi))
        key_new = jnp.where(lane8 <= row8_tok, key_new, INT_MIN)
        if gp > 1:
            key_new = jnp.concatenate([key_new, jnp.full((8, w - ps), INT_MIN, I32)], axis=1)
        key_sc[ns, key_rows(bi), :] = key_new

    @pl.when(s == 0)
    def _():
        for bi in range(nb):
            new_rows(bi)

    def pages(bi):
        refs = [page_refs[6 * (bi * gp + g):6 * (bi * gp + g) + 6] for g in range(gp)]
        cat = lambda parts: jnp.concatenate(parts, axis=1) if gp > 1 else parts[0]
        kt = cat([r[0][0, 0].reshape(H_B * HEAD_DIM, ps).astype(_MM) for r in refs])
        vt = cat([r[1][0, 0].reshape(H_B * HEAD_DIM, ps).astype(_MM) for r in refs])
        ka_sc[bi, chunk] = cat([r[2][0, 0].reshape(KV_A * HEAD_DIM, ps).astype(_MM) for r in refs])
        va_sc[bi, chunk] = cat([r[3][0, 0].reshape(KV_A * HEAD_DIM, ps).astype(_MM) for r in refs])
        kit = cat([r[4][0, 0].astype(_MM) for r in refs])
        lfts = [r[5][0, 0].astype(F32) for r in refs]

        suf_loc = _dot_f32(jnp.concatenate(lfts, axis=0), _strict_lower(ps))
        carry = r_sc[bi]
        sufs = [None] * gp
        for g in reversed(range(gp)):
            loc = suf_loc[g * H_B:(g + 1) * H_B]
            sufs[g] = loc + carry
            carry = carry + loc[:, 0:1] + lfts[g][:, 0:1]
        r_sc[bi] = carry

        lg = _dot(qb_ref[bi], kt) + (_per_head_rows(cat(sufs), t) + cq_ref[bi])
        m_prev = m_sc[bi]
        m_new = jnp.maximum(m_prev, jnp.max(lg, axis=1, keepdims=True))
        alpha = jnp.exp(m_prev - m_new)
        pe = jnp.exp(lg - m_new)
        l_sc[bi] = alpha * l_sc[bi] + jnp.sum(pe, axis=1, keepdims=True)
        acc_sc[bi] = alpha * acc_sc[bi] + _dot_nt(pe.astype(_MM), vt)
        m_sc[bi] = m_new
        key_sc[chunk, key_rows(bi), :] = _order_key(dup_scores(_dot(qi_ref[bi], kit), bi))

    for bi in range(nb):
        pages(bi)

    def fox_out(bi):
        o = acc_sc[bi] * (1.0 / l_sc[bi])
        lane_h = lax.broadcasted_iota(I32, o.shape, 1) // HEAD_DIM
        row_h = lax.broadcasted_iota(I32, o.shape, 0) // t
        o = jnp.where(lane_h == row_h, o, 0.0)
        ob = o[0:t]
        for hh in range(1, H_B):
            ob = ob + o[hh * t:(hh + 1) * t]
        ob_ref[bi] = ob

    def dsa_out(bi):
        nch = ns + 1
        qa = qa_ref[bi]
        lgs = []
        for c in range(nch):
            bias = sel_sc[c, key_rows(bi), :]
            if c < ns:
                prod = _dot(qa, ka_sc[bi, c])
            else:
                prod, bias = _dot_nt(qa, kan_sc[bi].astype(_MM)), bias[:, 0:ps]
            lgs.append(prod + jnp.concatenate([bias] * (rows // 8), axis=0))
        m = jnp.max(lgs[0], axis=1, keepdims=True)
        for c in range(1, nch):
            m = jnp.maximum(m, jnp.max(lgs[c], axis=1, keepdims=True))
        lsum = jnp.zeros((rows, 1), F32)
        out = jnp.zeros((rows, KV_A * HEAD_DIM), F32)
        for c in range(nch):
            pe = jnp.exp(lgs[c] - m)
            lsum = lsum + jnp.sum(pe, axis=1, keepdims=True)
            if c < ns:
                out = out + _dot_nt(pe.astype(_MM), va_sc[bi, c])
            else:
                out = out + _dot(pe.astype(_MM), van_sc[bi].astype(_MM))
        out = out * (1.0 / lsum)
        lane_j = lax.broadcasted_iota(I32, (t, KV_A * HEAD_DIM), 1) // HEAD_DIM
        for hh in range(H_A):
            piece = out[hh * t:(hh + 1) * t, :]
            oa_ref[bi, :, hh * LANES:(hh + 1) * LANES] = jnp.where(lane_j == hh // (H_A // KV_A), piece, 0.0)

    @pl.when(s == ns - 1)
    def _():
        for bi in range(nb):
            fox_out(bi)
        nch = ns + 1
        qpos = past + lax.broadcasted_iota(I32, (8 * nb, 1), 0) % t
        k_f = jnp.minimum(k_top, qpos + 1).astype(F32)
        load = lambda c: key_sc[c]
        tau, cnt = _kth_largest(load, nch, k_f, bits_per_pass=2)
        has_tie = jnp.max(cnt - k_f) > 0.0

        def store(c, sel):
            sel_sc[c] = jnp.where(sel, 0.0, NEG_INF)

        @pl.when(jnp.logical_not(has_tie))
        def _():
            for c in range(nch):
                store(c, load(c) >= tau)

        @pl.when(has_tie)
        def _():
            _select_with_ties(load, store, nch, w, tau, k_f)

        for bi in range(nb):
            dsa_out(bi)


def _attn_sample(layer, page_table, caches_t, qbd, qabd, qi32, wi32, cq32, cqt, kbn, vbn, kan, van, kin, k_top, t):
    kb_t, vb_t, ka_t, va_t, ki_t, lf_t = caches_t
    db, npg = page_table.shape
    ps = ki_t.shape[-1]
    rows = H_B * t
    gp = PAGES_PER_STEP if npg % PAGES_PER_STEP == 0 else 1
    nb = BATCHES_PER_STEP if db % BATCHES_PER_STEP == 0 else 1
    ns = npg // gp
    w = gp * ps
    const3 = lambda b, s, pt: (b, 0, 0)
    full = lambda a: BS((nb,) + a.shape[1:], const3)

    def page_specs(bi, g):
        def pg(b, s, pt):
            return pt[b * nb + bi, npg - (s + 1) * gp + g]
        five = lambda nh: BS((1, 1, nh, HEAD_DIM, ps), lambda b, s, pt: (layer, pg(b, s, pt), 0, 0, 0))
        return [five(H_B), five(H_B), five(KV_A), five(KV_A),
                BS((1, 1, D_IDX, ps), lambda b, s, pt: (layer, pg(b, s, pt), 0, 0)),
                BS((1, 1, H_B, ps), lambda b, s, pt: (layer, pg(b, s, pt), 0, 0))]

    small = [qbd, qabd, qi32, wi32, cq32, cqt, kbn, vbn, kan, van, kin]
    in_specs = [full(a) for a in small]
    pages = []
    for bi in range(nb):
        for g in range(gp):
            in_specs += page_specs(bi, g)
            pages += [kb_t, vb_t, ka_t, va_t, ki_t, lf_t]
    da = H_A * LANES
    db_ = H_B * HEAD_DIM
    dj = KV_A * HEAD_DIM
    return pl.pallas_call(
        functools.partial(_attn_s_kernel, past=npg * ps, k_top=k_top, t=t, gp=gp, nb=nb),
        out_shape=[SDS((db, t, da), F32), SDS((db, t, db_), F32)],
        grid_spec=pltpu.PrefetchScalarGridSpec(
            num_scalar_prefetch=1,
            grid=(db // nb, ns),
            in_specs=in_specs,
            out_specs=[BS((nb, t, da), const3), BS((nb, t, db_), const3)],
            scratch_shapes=[pltpu.VMEM((nb, rows, 1), F32), pltpu.VMEM((nb, rows, 1), F32),
                            pltpu.VMEM((nb, rows, db_), F32), pltpu.VMEM((nb, H_B, 1), F32),
                            pltpu.VMEM((nb, ns, dj, w), _MM), pltpu.VMEM((nb, ns, dj, w), _MM),
                            pltpu.VMEM((ns + 1, 8 * nb, w), I32), pltpu.VMEM((ns + 1, 8 * nb, w), F32),
                            pltpu.VMEM((nb, ps, db_), F32), pltpu.VMEM((nb, ps, db_), F32),
                            pltpu.VMEM((nb, ps, dj), F32), pltpu.VMEM((nb, ps, dj), F32),
                            pltpu.VMEM((nb, ps, D_IDX), F32)]),
        compiler_params=_cp("arbitrary", "arbitrary"),
        name="attn_sample",
    )(page_table, *small, *pages)


def _old2_attn_s_kernel(pt_ref, qb_ref, qa_ref, qi_ref, wi_ref, cq_ref, cqt_ref, kbn_ref, vbn_ref, kan_ref, van_ref,
                   kin_ref, *rest, past, k_top, t, gp):
    page_refs = rest[:6 * gp]
    oa_ref, ob_ref = rest[6 * gp:6 * gp + 2]
    (m_sc, l_sc, acc_sc, r_sc, kt_sc, vt_sc, ki_sc, ka_sc, va_sc, key_sc, sel_sc,
     kbn_sc, vbn_sc, kan_sc, van_sc, kin_sc) = rest[6 * gp + 2:]
    b = pl.program_id(0)
    s = pl.program_id(1)
    ns = ka_sc.shape[0]
    ps = kin_sc.shape[0]
    w = gp * ps
    rows = H_B * t
    nt8 = kbn_ref.shape[1]
    row_tok = lax.broadcasted_iota(I32, (rows, ps), 0) % t
    lane_r = lax.broadcasted_iota(I32, (rows, ps), 1)
    row8_tok = lax.broadcasted_iota(I32, (8, ps), 0) % t
    lane8 = lax.broadcasted_iota(I32, (8, ps), 1)
    wi = wi_ref[0] * (H_IDX ** -0.5)

    def dup_scores(s32):
        s32 = jnp.maximum(s32, 0.0) * wi
        sc = s32[0:t]
        for hh in range(1, H_IDX):
            sc = sc + s32[hh * t:(hh + 1) * t]
        return jnp.concatenate([sc] * (8 // t), axis=0)

    @pl.when((b == 0) & (s == 0))
    def _():
        for ref in (kbn_sc, vbn_sc, kan_sc, van_sc, kin_sc):
            ref[...] = jnp.zeros(ref.shape, F32)

    @pl.when(s == 0)
    def _():
        kbn_sc[0:nt8, :] = kbn_ref[0]
        vbn_sc[0:nt8, :] = vbn_ref[0]
        kan_sc[0:nt8, :] = kan_ref[0]
        van_sc[0:nt8, :] = van_ref[0]
        kin_sc[0:nt8, :] = kin_ref[0]
        r_sc[...] = jnp.zeros(r_sc.shape, F32)
        lg = _dot_nt(qb_ref[0], kbn_sc[...].astype(_MM)) + cq_ref[0] - _per_head_rows(cqt_ref[0], t)
        lg = jnp.where(lane_r <= row_tok, lg, NEG_INF)
        m = jnp.max(lg, axis=1, keepdims=True)
        pe = jnp.exp(lg - m)
        m_sc[...] = m
        l_sc[...] = jnp.sum(pe, axis=1, keepdims=True)
        acc_sc[...] = _dot(pe.astype(_MM), vbn_sc[...].astype(_MM))
        key_new = _order_key(dup_scores(_dot_nt(qi_ref[0], kin_sc[...].astype(_MM))))
        key_new = jnp.where(lane8 <= row8_tok, key_new, INT_MIN)
        if gp > 1:
            key_new = jnp.concatenate([key_new, jnp.full((8, w - ps), INT_MIN, I32)], axis=1)
        key_sc[ns] = key_new

    chunk = ns - 1 - s
    lfts = []
    for g in range(gp):
        kb_ref, vb_ref, ka_ref, va_ref, ki_ref, lf_ref = page_refs[6 * g:6 * g + 6]
        lanes = slice(g * ps, (g + 1) * ps)
        kt_sc[:, lanes] = kb_ref[0, 0].reshape(H_B * HEAD_DIM, ps).astype(_MM)
        vt_sc[:, lanes] = vb_ref[0, 0].reshape(H_B * HEAD_DIM, ps).astype(_MM)
        ki_sc[:, lanes] = ki_ref[0, 0].astype(_MM)
        ka_sc[chunk, :, lanes] = ka_ref[0, 0].reshape(KV_A * HEAD_DIM, ps).astype(_MM)
        va_sc[chunk, :, lanes] = va_ref[0, 0].reshape(KV_A * HEAD_DIM, ps).astype(_MM)
        lfts.append(lf_ref[0, 0].astype(F32))

    lf_all = jnp.concatenate(lfts, axis=0)
    suf_loc = _dot_f32(lf_all, _strict_lower(ps))
    carry = r_sc[...]
    sufs = [None] * gp
    for g in reversed(range(gp)):
        loc = suf_loc[g * H_B:(g + 1) * H_B]
        sufs[g] = loc + carry
        carry = carry + loc[:, 0:1] + lfts[g][:, 0:1]
    r_sc[...] = carry
    suf = jnp.concatenate(sufs, axis=1) if gp > 1 else sufs[0]

    lg = _dot(qb_ref[0], kt_sc[...]) + (_per_head_rows(suf, t) + cq_ref[0])
    m_prev = m_sc[...]
    m_new = jnp.maximum(m_prev, jnp.max(lg, axis=1, keepdims=True))
    alpha = jnp.exp(m_prev - m_new)
    pe = jnp.exp(lg - m_new)
    l_sc[...] = alpha * l_sc[...] + jnp.sum(pe, axis=1, keepdims=True)
    acc_sc[...] = alpha * acc_sc[...] + _dot_nt(pe.astype(_MM), vt_sc[...])
    m_sc[...] = m_new
    key_sc[chunk] = _order_key(dup_scores(_dot(qi_ref[0], ki_sc[...])))

    @pl.when(s == ns - 1)
    def _():
        o = acc_sc[...] * (1.0 / l_sc[...])
        lane_h = lax.broadcasted_iota(I32, o.shape, 1) // HEAD_DIM
        row_h = lax.broadcasted_iota(I32, o.shape, 0) // t
        o = jnp.where(lane_h == row_h, o, 0.0)
        ob = o[0:t]
        for hh in range(1, H_B):
            ob = ob + o[hh * t:(hh + 1) * t]
        ob_ref[0] = ob

        nch = ns + 1
        qpos = past + lax.broadcasted_iota(I32, (8, 1), 0) % t
        k_f = jnp.minimum(k_top, qpos + 1).astype(F32)
        load = lambda c: key_sc[c]
        tau, cnt = _kth_largest(load, nch, k_f, bits_per_pass=2)
        has_tie = jnp.max(cnt - k_f) > 0.0

        def store(c, sel):
            sel_sc[c] = jnp.where(sel, 0.0, NEG_INF)

        @pl.when(jnp.logical_not(has_tie))
        def _():
            for c in range(nch):
                store(c, load(c) >= tau)

        @pl.when(has_tie)
        def _():
            _select_with_ties(load, store, nch, w, tau, k_f)

        qa = qa_ref[0]
        lgs = []
        for c in range(nch):
            if c < ns:
                prod, bias = _dot(qa, ka_sc[c]), sel_sc[c]
            else:
                prod, bias = _dot_nt(qa, kan_sc[...].astype(_MM)), sel_sc[c][:, 0:ps]
            lgs.append(prod + jnp.concatenate([bias] * (rows // 8), axis=0))
        m = jnp.max(lgs[0], axis=1, keepdims=True)
        for c in range(1, nch):
            m = jnp.maximum(m, jnp.max(lgs[c], axis=1, keepdims=True))
        lsum = jnp.zeros((rows, 1), F32)
        out = jnp.zeros((rows, KV_A * HEAD_DIM), F32)
        for c in range(nch):
            pe = jnp.exp(lgs[c] - m)
            lsum = lsum + jnp.sum(pe, axis=1, keepdims=True)
            if c < ns:
                out = out + _dot_nt(pe.astype(_MM), va_sc[c])
            else:
                out = out + _dot(pe.astype(_MM), van_sc[...].astype(_MM))
        out = out * (1.0 / lsum)
        lane_j = lax.broadcasted_iota(I32, (t, KV_A * HEAD_DIM), 1) // HEAD_DIM
        for hh in range(H_A):
            piece = out[hh * t:(hh + 1) * t, :]
            oa_ref[0, :, hh * LANES:(hh + 1) * LANES] = jnp.where(lane_j == hh // (H_A // KV_A), piece, 0.0)


def _old2_attn_sample(layer, page_table, caches_t, qbd, qabd, qi32, wi32, cq32, cqt, kbn, vbn, kan, van, kin, k_top, t):
    kb_t, vb_t, ka_t, va_t, ki_t, lf_t = caches_t
    db, npg = page_table.shape
    ps = ki_t.shape[-1]
    rows = H_B * t
    gp = PAGES_PER_STEP if npg % PAGES_PER_STEP == 0 else 1
    ns = npg // gp
    w = gp * ps
    const3 = lambda b, s, pt: (b, 0, 0)
    full = lambda a: BS((1,) + a.shape[1:], const3)

    def page_specs(g):
        def pg(b, s, pt):
            return pt[b, npg - (s + 1) * gp + g]
        five = lambda nh: BS((1, 1, nh, HEAD_DIM, ps), lambda b, s, pt: (layer, pg(b, s, pt), 0, 0, 0))
        return [five(H_B), five(H_B), five(KV_A), five(KV_A),
                BS((1, 1, D_IDX, ps), lambda b, s, pt: (layer, pg(b, s, pt), 0, 0)),
                BS((1, 1, H_B, ps), lambda b, s, pt: (layer, pg(b, s, pt), 0, 0))]

    small = [qbd, qabd, qi32, wi32, cq32, cqt, kbn, vbn, kan, van, kin]
    in_specs = [full(a) for a in small]
    pages = []
    for g in range(gp):
        in_specs += page_specs(g)
        pages += [kb_t, vb_t, ka_t, va_t, ki_t, lf_t]
    da = H_A * LANES
    db_ = H_B * HEAD_DIM
    return pl.pallas_call(
        functools.partial(_attn_s_kernel, past=npg * ps, k_top=k_top, t=t, gp=gp),
        out_shape=[SDS((db, t, da), F32), SDS((db, t, db_), F32)],
        grid_spec=pltpu.PrefetchScalarGridSpec(
            num_scalar_prefetch=1,
            grid=(db, ns),
            in_specs=in_specs,
            out_specs=[BS((1, t, da), const3), BS((1, t, db_), const3)],
            scratch_shapes=[pltpu.VMEM((rows, 1), F32), pltpu.VMEM((rows, 1), F32), pltpu.VMEM((rows, db_), F32),
                            pltpu.VMEM((H_B, 1), F32),
                            pltpu.VMEM((db_, w), _MM), pltpu.VMEM((db_, w), _MM), pltpu.VMEM((D_IDX, w), _MM),
                            pltpu.VMEM((ns, KV_A * HEAD_DIM, w), _MM), pltpu.VMEM((ns, KV_A * HEAD_DIM, w), _MM),
                            pltpu.VMEM((ns + 1, 8, w), I32), pltpu.VMEM((ns + 1, 8, w), F32),
                            pltpu.VMEM((ps, db_), F32), pltpu.VMEM((ps, db_), F32),
                            pltpu.VMEM((ps, KV_A * HEAD_DIM), F32), pltpu.VMEM((ps, KV_A * HEAD_DIM), F32),
                            pltpu.VMEM((ps, D_IDX), F32)]),
        compiler_params=_cp("arbitrary", "arbitrary"),
        name="attn_sample",
    )(page_table, *small, *pages)


def _old_attn_s_kernel(pt_ref, qb_ref, qa_ref, qi_ref, wi_ref, fq_ref, kb_ref, vb_ref, ka_ref, va_ref, ki_ref,
                   fk_ref, kbn_ref, vbn_ref, kan_ref, van_ref, kin_ref, fkn_ref, oa_ref, ob_ref,
                   m_sc, l_sc, acc_sc, ka_sc, va_sc, key_sc, sel_sc, *, past, k_top, t):
    p = pl.program_id(1)
    npg = pl.num_programs(1)
    ps = ki_ref.shape[2]
    rows = H_B * t
    wb = ps * H_B
    wa = ps * KV_A

    @pl.when(p == 0)
    def _():
        m_sc[...] = jnp.full(m_sc.shape, NEG_INF, F32)
        l_sc[...] = jnp.zeros(l_sc.shape, F32)
        acc_sc[...] = jnp.zeros(acc_sc.shape, F32)

    k2 = kb_ref[0, 0].reshape(wb, HEAD_DIM).astype(_MM)
    v2 = vb_ref[0, 0].reshape(wb, HEAD_DIM).astype(_MM)
    lt = _dot_nt(qb_ref[0], k2) + fq_ref[0] - fk_ref[0]
    r_b = lax.broadcasted_iota(I32, (rows, wb), 0)
    c_b = lax.broadcasted_iota(I32, (rows, wb), 1)
    lt = jnp.where((c_b % H_B) == (r_b // t), lt, NEG_INF)
    _online_update(lt, v2, m_sc, l_sc, acc_sc)

    def dup_scores(ki2):
        s = _dot_nt(qi_ref[0], ki2)
        s = jnp.maximum(s, 0.0) * (wi_ref[0] * (H_IDX ** -0.5))
        sc = s[0:t]
        for hh in range(1, H_IDX):
            sc = sc + s[hh * t:(hh + 1) * t]
        return jnp.concatenate([sc] * (8 // t), axis=0)

    r_e = lax.broadcasted_iota(I32, (wa, ps), 0)
    c_e = lax.broadcasted_iota(I32, (wa, ps), 1)
    expand = jnp.where((r_e // KV_A) == c_e, 1.0, 0.0).astype(_MM)
    ki2 = _dot(expand, ki_ref[0, 0].astype(_MM)).astype(_MM)
    key_sc[p] = _order_key(dup_scores(ki2))
    ka_sc[p] = ka_ref[0, 0].reshape(wa, HEAD_DIM).astype(_MM)
    va_sc[p] = va_ref[0, 0].reshape(wa, HEAD_DIM).astype(_MM)

    @pl.when(p == npg - 1)
    def _():
        nn = kbn_ref.shape[1]
        ltn = _dot_nt(qb_ref[0], kbn_ref[0].astype(_MM)) + fq_ref[0] - fkn_ref[0]
        r_n = lax.broadcasted_iota(I32, (rows, nn), 0)
        c_n = lax.broadcasted_iota(I32, (rows, nn), 1)
        ok = ((c_n % H_B) == (r_n // t)) & ((c_n // H_B) <= (r_n % t))
        _online_update(jnp.where(ok, ltn, NEG_INF), vbn_ref[0].astype(_MM), m_sc, l_sc, acc_sc)
        ob_ref[0] = acc_sc[...] * (1.0 / l_sc[...])

        nch = key_sc.shape[0]
        r8 = lax.broadcasted_iota(I32, (8, wa), 0)
        c8 = lax.broadcasted_iota(I32, (8, wa), 1)
        ok_new = (c8 // KV_A) <= (r8 % t)
        key_new = _order_key(dup_scores(kin_ref[0].astype(_MM)))
        key_sc[nch - 1] = jnp.where(ok_new, key_new, INT_MIN)
        ka_sc[nch - 1] = kan_ref[0].astype(_MM)
        va_sc[nch - 1] = van_ref[0].astype(_MM)

        qpos = past + lax.broadcasted_iota(I32, (8, 1), 0) % t
        k_f = (KV_A * jnp.minimum(k_top, qpos + 1)).astype(F32)
        load = lambda c: key_sc[c]
        tau, cnt = _kth_largest(load, nch, k_f)
        has_tie = jnp.max(cnt - k_f) > 0.0

        def store(c, sel):
            sel_sc[c] = jnp.where(sel, 0.0, NEG_INF)

        @pl.when(jnp.logical_not(has_tie))
        def _():
            for c in range(nch):
                store(c, load(c) >= tau)

        @pl.when(has_tie)
        def _():
            _select_with_ties(load, store, nch, wa, tau, k_f)

        m_sc[...] = jnp.full(m_sc.shape, NEG_INF, F32)
        l_sc[...] = jnp.zeros(l_sc.shape, F32)
        acc_sc[...] = jnp.zeros(acc_sc.shape, F32)
        r_a = lax.broadcasted_iota(I32, (rows, wa), 0)
        c_a = lax.broadcasted_iota(I32, (rows, wa), 1)
        grp_ok = (c_a % KV_A) == (r_a // (t * (H_A // KV_A)))
        for c in range(nch):
            bias = jnp.concatenate([sel_sc[c]] * (rows // 8), axis=0)
            lg = _dot_nt(qa_ref[0], ka_sc[c]) + bias
            _online_update(jnp.where(grp_ok, lg, NEG_INF), va_sc[c], m_sc, l_sc, acc_sc)
        oa_ref[0] = acc_sc[...] * (1.0 / l_sc[...])


def _old_attn_sample(layer, page_table, caches, qb32, qa32, qi32, wi32, fq32, fk_row, kbn, vbn, kan, van, kin, fkn_row,
                 k_top, t):
    cache_a_k, cache_a_v, cache_a_idx_k, cache_b_k, cache_b_v = caches
    db, npg = page_table.shape
    ps = cache_a_idx_k.shape[2]
    rows = H_B * t
    wb, wa = ps * H_B, ps * KV_A
    past = npg * ps
    qspec = BS((1, rows, HEAD_DIM), lambda b, p, pt: (b, 0, 0))
    cspec = BS((1, rows, 1), lambda b, p, pt: (b, 0, 0))
    pool5 = lambda nh: BS((1, 1, ps, nh, HEAD_DIM), lambda b, p, pt: (layer, pt[b, p], 0, 0, 0))
    newspec = lambda r: BS((1, r, HEAD_DIM), lambda b, p, pt: (b, 0, 0))
    return pl.pallas_call(
        functools.partial(_attn_s_kernel, past=past, k_top=k_top, t=t),
        out_shape=[SDS((db, rows, HEAD_DIM), F32), SDS((db, rows, HEAD_DIM), F32)],
        grid_spec=pltpu.PrefetchScalarGridSpec(
            num_scalar_prefetch=1,
            grid=(db, npg),
            in_specs=[qspec, qspec, qspec, cspec, cspec,
                      pool5(H_B), pool5(H_B), pool5(KV_A), pool5(KV_A),
                      BS((1, 1, ps, D_IDX), lambda b, p, pt: (layer, pt[b, p], 0, 0)),
                      BS((1, 1, wb), lambda b, p, pt: (b, 0, p)),
                      newspec(rows), newspec(rows), newspec(wa), newspec(wa), newspec(wa),
                      BS((1, 1, rows), lambda b, p, pt: (b, 0, 0))],
            out_specs=[BS((1, rows, HEAD_DIM), lambda b, p, pt: (b, 0, 0)),
                       BS((1, rows, HEAD_DIM), lambda b, p, pt: (b, 0, 0))],
            scratch_shapes=[pltpu.VMEM((rows, 1), F32), pltpu.VMEM((rows, 1), F32),
                            pltpu.VMEM((rows, HEAD_DIM), F32),
                            pltpu.VMEM((npg + 1, wa, HEAD_DIM), _MM), pltpu.VMEM((npg + 1, wa, HEAD_DIM), _MM),
                            pltpu.VMEM((npg + 1, 8, wa), I32), pltpu.VMEM((npg + 1, 8, wa), F32)]),
        compiler_params=_cp("arbitrary", "arbitrary"),
        name="attn_sample",
    )(page_table, qb32, qa32, qi32, wi32, fq32, cache_b_k, cache_b_v, cache_a_k, cache_a_v, cache_a_idx_k,
      fk_row, kbn, vbn, kan, van, kin, fkn_row)


def _row_tile(n, pref):
    tm = min(pref, n)
    assert n % tm == 0, (n, tm)
    return tm


def _even_prompt(grp, layer, x, w, st):
    b, t = grp.nb, grp.t
    tm = _row_tile(t, ROW_TILE)
    outs = _inproj(grp, layer, x, w["norm_mix"], w["w_in_slots"], w["groups_slots"], w["b_f"], st["rope"], tm)
    qa_s, qi_s, qb_s, kb, vb, ka, va, misc = outs
    logf = misc[:, MISC_FB:MISC_FB + H_B]
    f = _cumsum_prompt(logf.reshape(b, t, H_B))
    o_b = _fox_prompt(qb_s, kb, vb, f, jnp.swapaxes(f, 1, 2), b, t)
    o_a = _dsa_prompt(qi_s, qa_s, misc, ka, va, b, t, min(TOPK_MAX, t // 4))
    x = _outproj(grp, layer, o_a.reshape(b * t, -1), o_b.reshape(b * t, -1), w["w_out_a_slots"], w["w_out_b"], x,
                 _row_tile(grp.n, FFN_ROW_TILE))
    return x, (ka, va, misc[:, :D_IDX], kb, vb, logf)


def _even_sample(grp, layer, x, w, st):
    db, t = grp.nb, grp.t
    n = grp.n
    tm = _row_tile(n, ROW_TILE)
    assert 8 % t == 0, t
    outs = _inproj(grp, layer, x, w["norm_mix"], w["w_in_plain"], w["groups_plain"], w["b_f"], st["rope"], tm,
                   cum_t=t)
    qa, qi, qb, kb, vb, ka, va, misc = outs
    ki, wi, logf = misc[:, :D_IDX], misc[:, MISC_WI:MISC_WI + H_IDX], misc[:, MISC_FB:MISC_FB + H_B]
    cq = misc[:, MISC_CQ:MISC_CQ + H_B]
    page_table = st["page_table"]
    npg = page_table.shape[1]
    ps = st["caches_t"][0].shape[-1]

    def head_major(a, nh):
        return a.reshape(db, t, nh, -1).transpose(0, 2, 1, 3)

    eye_b = jnp.eye(H_B, dtype=qb.dtype)
    qbd = (head_major(qb, H_B)[:, :, :, None, :] * eye_b[None, :, None, :, None]).reshape(db, H_B * t, -1)
    grp_hot = (jnp.arange(H_A)[:, None] // (H_A // KV_A) == jnp.arange(KV_A)[None, :]).astype(qa.dtype)
    qabd = (head_major(qa, H_A)[:, :, :, None, :] * grp_hot[None, :, None, :, None]).reshape(db, H_A * t, -1)
    qi32 = head_major(qi, H_IDX).reshape(db, H_IDX * t, D_IDX)
    wi32 = head_major(wi, H_IDX).reshape(db, H_IDX * t, 1)
    cq32 = head_major(cq, H_B).reshape(db, H_B * t, 1)
    cqt = jnp.pad(cq.reshape(db, t, H_B).transpose(0, 2, 1), ((0, 0), (0, 0), (0, ps - t)))
    pad8 = lambda a: jnp.pad(a.reshape(db, t, -1), ((0, 0), (0, 8 - t), (0, 0)))
    o_a, o_b = _attn_sample(st["att_layer"], page_table, st["caches_t"], qbd, qabd, qi32, wi32, cq32, cqt,
                            pad8(kb), pad8(vb), pad8(ka), pad8(va), pad8(ki),
                            min(TOPK_MAX, (npg * ps + t) // 4), t)
    x = _outproj(grp, layer, o_a.reshape(n, -1).astype(_MM), o_b.reshape(n, -1).astype(_MM), w["w_out_a_slots"],
                 w["w_out_b"], x, _row_tile(n, FFN_ROW_TILE))
    return x, (ka, va, ki, kb, vb, logf)


def _odd_prompt(grp, layer, x, w, st):
    tm = _row_tile(grp.t, ROW_TILE)
    u = _pw1(grp, layer, x, w["norm_mix"], w["w_pw1"], w["b_pw1"], _row_tile(grp.n, ROW_TILE))
    x = _conv_prompt(grp, layer, u, x, w["w_dw"], w["b_dw"], w["ln_g"], w["ln_b"], w["w_pw2"], w["b_pw2"], tm)
    nb, t = grp.nb, grp.t
    state = u.reshape(nb, t, -1)[:, t - (CONV_WIDTH - 1):]
    return x, state


def _odd_sample(grp, layer, x, w, st):
    db, t, d = grp.nb, grp.t, grp.d
    u = _pw1(grp, layer, x, w["norm_mix"], w["w_pw1"], w["b_pw1"], _row_tile(grp.n, ROW_TILE))
    buf = st["state_conv"][st["conv_layer"]]
    u3 = u.reshape(db, t, -1)
    x_t = _conv_sample(jnp.swapaxes(buf, 0, 1), jnp.swapaxes(u3, 0, 1), jnp.swapaxes(x.reshape(db, t, d), 0, 1),
                       st["mod_batch"][layer], w["w_dw"], w["b_dw"], w["ln_g"], w["ln_b"], w["w_pw2"], w["b_pw2"])
    x = jnp.swapaxes(x_t, 0, 1).reshape(db * t, d)
    state = jnp.concatenate([buf.astype(F32), u3], axis=1)[:, t:]
    return x, state


def _trunk(grp, x, even_fn, odd_fn, layer_w, st, norm_final):
    depth = len(layer_w)
    att, conv = [], []
    for i in range(depth):
        w = layer_w[i]
        if i % 2 == 0:
            x, s = even_fn(grp, i, x, w, dict(st, att_layer=i // 2))
            att.append(s)
        else:
            x, s = odd_fn(grp, i, x, w, dict(st, conv_layer=i // 2))
            conv.append(s)
        tm = _row_tile(grp.n, FFN_ROW_TILE)
        x = _ffn(grp, i, x, w["norm_ffn"], w["w_ffn_in"], w["w_ffn_out"], norm_final, i == depth - 1, tm,
                 w["tf"])
    return x, att, conv


def kernel(x_prompt, x_sample, cache_a_k, cache_a_v, cache_a_idx_k, cache_b_k, cache_b_v, cache_b_logf,
           state_conv, page_table, c_prompt, c_sample, w_in_att, b_fgate, w_out_att, w_pw1, b_pw1, w_dw,
           b_dw, ln_conv_g, ln_conv_b, w_pw2, b_pw2, w_ada, b_ada, norm_mix, norm_ffn, w_ffn_in,
           w_ffn_out, norm_final):
    bp, sp, d = x_prompt.shape
    db, ts, _ = x_sample.shape
    depth = w_ada.shape[0]
    npg, ps = page_table.shape[1], cache_a_idx_k.shape[2]
    past = npg * ps

    mod = _ada(jnp.concatenate([c_prompt, c_sample], axis=0), w_ada, b_ada)
    mod_p = mod[:, :bp].reshape(depth, bp, 1, 6 * d)
    mod_b = mod[:, bp:]
    mod_s = jnp.broadcast_to(mod_b[:, :, None, :], (depth, db, ts, 6 * d)).reshape(depth, db * ts, 6 * d)
    grp_p = _Group(bp, sp, d, mod_p, per_row=False)
    grp_s = _Group(db, ts, d, mod_s, per_row=True)

    ff = w_ffn_out.shape[1]
    tf = ff // 2 if ff % (2 * LANES) == 0 else ff
    half_a = H_A * HEAD_DIM
    layer_w = []
    for i in range(depth):
        w = {"norm_mix": norm_mix[i], "norm_ffn": norm_ffn[i], "w_ffn_in": w_ffn_in[i].astype(_MM),
             "w_ffn_out": w_ffn_out[i].astype(_MM), "tf": tf}
        l = i // 2
        if i % 2 == 0:
            w["w_in_slots"], w["groups_slots"] = _pack_w_in(w_in_att[l], True)
            w["w_in_plain"], w["groups_plain"] = _pack_w_in(w_in_att[l], False)
            w["b_f"] = b_fgate[l]
            wo = w_out_att[l]
            wa = wo[:half_a].reshape(H_A, HEAD_DIM, d)
            wa_slots = jnp.zeros((H_A, 2, HEAD_DIM, d), wo.dtype)
            for hh in range(H_A):
                wa_slots = wa_slots.at[hh, hh // (H_A // KV_A)].set(wa[hh])
            w["w_out_a_slots"] = wa_slots.reshape(H_A * LANES, d).astype(_MM)
            w["w_out_a"] = wo[:half_a].astype(_MM)
            w["w_out_b"] = wo[half_a:].astype(_MM)
        else:
            w.update(w_pw1=w_pw1[l].astype(_MM), b_pw1=b_pw1[l], w_dw=w_dw[l], b_dw=b_dw[l], ln_g=ln_conv_g[l],
                     ln_b=ln_conv_b[l], w_pw2=w_pw2[l].astype(_MM), b_pw2=b_pw2[l])
        layer_w.append(w)

    st_p = {"rope": _rope_tables(jnp.arange(sp, dtype=I32).astype(F32))}
    tm_s = _row_tile(db * ts, ROW_TILE)
    pos_s = (past + (jnp.arange(tm_s, dtype=I32) % ts)).astype(F32)
    pos_last5 = lambda a: jnp.transpose(a, (0, 1, 3, 4, 2))
    pos_last4 = lambda a: jnp.transpose(a, (0, 1, 3, 2))
    caches_t = (pos_last5(cache_b_k), pos_last5(cache_b_v), pos_last5(cache_a_k), pos_last5(cache_a_v),
                pos_last4(cache_a_idx_k), pos_last4(cache_b_logf))
    st_s = {"rope": _rope_tables(pos_s), "page_table": page_table, "caches_t": caches_t,
            "state_conv": state_conv, "mod_batch": mod_b}

    y_p, att_p, conv_p = _trunk(grp_p, x_prompt.reshape(bp * sp, d), _even_prompt, _odd_prompt, layer_w, st_p,
                                norm_final)
    y_s, att_s, conv_s = _trunk(grp_s, x_sample.reshape(db * ts, d), _even_sample, _odd_sample, layer_w, st_s,
                                norm_final)

    def stack_att(att, nb, t):
        shapes = ((KV_A, HEAD_DIM), (KV_A, HEAD_DIM), (D_IDX,), (H_B, HEAD_DIM), (H_B, HEAD_DIM), (H_B,))
        return [jnp.stack([s[j].reshape(nb, t, *shapes[j]) for s in att]) for j in range(6)]

    out_p = stack_att(att_p, bp, sp)
    out_s = stack_att(att_s, db, ts)
    return (y_p.reshape(bp, sp, d), y_s.reshape(db, ts, d), *out_p, jnp.stack(conv_p),
            *out_s, jnp.stack(conv_s))
```

```python
import functools

import jax
import jax.numpy as jnp
from jax import lax
from jax.experimental import pallas as pl
from jax.experimental.pallas import tpu as pltpu

F32 = jnp.float32
I32 = jnp.int32
_MM = jnp.bfloat16

HEAD_DIM = 64
H_A = 8
KV_A = 2
H_IDX = 8
D_IDX = 64
H_B = 8
ROT_DIM = HEAD_DIM // 4
ROPE_THETA = 500000.0
TOPK_MAX = 256
CONV_WIDTH = 31
EPS = 1e-6
LANES = 128
INT_MIN = -(2 ** 31)
NEG_INF = float("-inf")
Q_SCALE = HEAD_DIM ** -0.5
VMEM_LIMIT = 56 * 1024 * 1024

ROW_TILE = 256
FFN_ROW_TILE = 512
FOX_TQ, FOX_TK = 512, 512
DSA_TQ = 256
DSA_CHUNK = 256
COUNT_ROWS = 32
DSA_BANDS = 8
PAGES_PER_STEP = 8
BATCHES_PER_STEP = 2

_SPLITS = (H_A * HEAD_DIM, KV_A * HEAD_DIM, KV_A * HEAD_DIM, H_IDX * D_IDX, D_IDX, H_IDX,
           H_B * HEAD_DIM, H_B * HEAD_DIM, H_B * HEAD_DIM, H_B)
_NAMES = ("qa", "ka", "va", "qi", "ki", "wi", "qb", "kb", "vb", "fb")
_OFF = {}
_o = 0
for _n, _w in zip(_NAMES, _SPLITS):
    _OFF[_n] = (_o, _w)
    _o += _w
MISC_WI = D_IDX
MISC_FB = D_IDX + H_IDX
MISC_CQ = MISC_FB + H_B

SDS = jax.ShapeDtypeStruct
BS = pl.BlockSpec


def _cp(*sem):
    return pltpu.CompilerParams(dimension_semantics=sem, vmem_limit_bytes=VMEM_LIMIT)


def _dot(a, b):
    return jnp.dot(a, b, preferred_element_type=F32)


def _dot_nt(a, b):
    return lax.dot_general(a, b, (((1,), (1,)), ((), ())), preferred_element_type=F32)


def _sigmoid(x):
    return 1.0 / (1.0 + jnp.exp(-x))


def _silu(x):
    return x * _sigmoid(x)


def _rms(x, g):
    return x * lax.rsqrt(jnp.mean(x * x, axis=-1, keepdims=True) + EPS) * g


def _modulate(x, g, shift, scale):
    return _rms(x, g) * (1.0 + scale) + shift


def _ada_kernel(c_ref, w_ref, b_ref, o_ref):
    a = _silu(c_ref[...]).astype(_MM)
    o_ref[0] = _dot(a, w_ref[0].astype(_MM)) + b_ref[0]


def _ada(c_all, w_ada, b_ada):
    depth, d, d6 = w_ada.shape
    r = c_all.shape[0]
    tn = d6 // 4
    return pl.pallas_call(
        _ada_kernel,
        out_shape=SDS((depth, r, d6), F32),
        grid=(depth, d6 // tn),
        in_specs=[BS((r, d), lambda l, j: (0, 0)),
                  BS((1, d, tn), lambda l, j: (l, 0, j)),
                  BS((1, 1, tn), lambda l, j: (l, 0, j))],
        out_specs=BS((1, r, tn), lambda l, j: (l, 0, j)),
        compiler_params=_cp("arbitrary", "arbitrary"),
        name="ada_mod",
    )(c_all, w_ada, b_ada.reshape(depth, 1, d6))


class _Group:
    def __init__(self, nb, t, d, mod, batch_major):
        self.nb, self.t, self.d = nb, t, d
        self.n = nb * t
        self.mod = mod
        self.batch_major = batch_major

    def mod_arg(self, layer, tm, chunk):
        d = self.d
        if self.batch_major:
            tpb = self.t // tm
            return self.mod[layer], BS((1, 1, d), lambda i, *_: (i // tpb, 0, chunk))
        assert tm % self.nb == 0, (tm, self.nb)
        return self.mod[layer], BS((1, self.nb, d), lambda i, *_: (0, 0, chunk))


def _rows(v, n):
    r = v.shape[0]
    return v if r in (1, n) else jnp.concatenate([v] * (n // r), axis=0)


def _rope_tables(pos):
    half = ROT_DIM // 2
    inv = ROPE_THETA ** (-jnp.arange(half, dtype=F32) * 2.0 / ROT_DIM)
    ang = pos[:, None] * inv[None, :]
    cos, sin = jnp.cos(ang), jnp.sin(ang)
    n = pos.shape[0]
    one = jnp.ones((n, HEAD_DIM - ROT_DIM), F32)
    zero = jnp.zeros((n, HEAD_DIM - ROT_DIM), F32)
    z8 = jnp.zeros((n, half), F32)
    c = jnp.concatenate([cos, cos, one], axis=1)
    s1 = jnp.concatenate([-sin, z8, zero], axis=1)
    s2 = jnp.concatenate([z8, sin, zero], axis=1)
    rep = LANES // HEAD_DIM
    return jnp.tile(c, (1, rep)), jnp.tile(s1, (1, rep)), jnp.tile(s2, (1, rep))


def _rope(y, c, s1, s2):
    w = y.shape[1]
    rep = w // LANES
    if rep > 1:
        c, s1, s2 = (jnp.concatenate([t] * rep, axis=1) for t in (c, s1, s2))
    half = ROT_DIM // 2
    return y * c + pltpu.roll(y, w - half, 1) * s1 + pltpu.roll(y, half, 1) * s2


def _log_sigmoid(x):
    return jnp.minimum(x, 0.0) - jnp.log(1.0 + jnp.exp(-jnp.abs(x)))


def _inproj_kernel(x_ref, g_ref, sh_ref, sc_ref, w_ref, c_ref, s1_ref, s2_ref, bf_ref, *out_refs, groups, cum_t):
    x = x_ref[...]
    n_rows = x.shape[0]
    h = _modulate(x, g_ref[...], _rows(sh_ref[0], n_rows), _rows(sc_ref[0], n_rows)).astype(_MM)
    c, s1, s2 = c_ref[...], s1_ref[...], s2_ref[...]
    off = 0
    for (name, width, rope, scale), o_ref in zip(groups, out_refs):
        y = _dot(h, w_ref[:, off:off + width])
        off += width
        if name == "misc":
            lane = lax.broadcasted_iota(I32, y.shape, 1)
            yr = _rope(y, c, s1, s2)
            lf = _log_sigmoid(y + bf_ref[...])
            in_fb = (lane >= MISC_FB) & (lane < MISC_FB + H_B)
            if cum_t:
                stride = n_rows // cum_t
                tok = lax.broadcasted_iota(I32, y.shape, 0) // stride
                cum = lf
                for k in range(1, cum_t):
                    cum = cum + jnp.where(tok >= k, pltpu.roll(lf, k * stride, 0), 0.0)
                y = jnp.where((lane >= MISC_CQ) & (lane < MISC_CQ + H_B), pltpu.roll(cum, H_B, 1), y)
            y = jnp.where(lane < MISC_WI, yr, jnp.where(in_fb, lf, y))
        elif rope:
            y = _rope(y, c, s1, s2)
        if scale != 1.0:
            y = y * scale
        o_ref[...] = y.astype(o_ref.dtype)


def _pack_w_in(w_in, slots):
    d = w_in.shape[0]

    def cols(name):
        o, w = _OFF[name]
        return w_in[:, o:o + w]

    def slot(name, place):
        src = cols(name).reshape(d, -1, HEAD_DIM)
        zero = jnp.zeros((d, HEAD_DIM), w_in.dtype)
        parts = []
        for hh in range(src.shape[1]):
            parts += [src[:, hh], zero] if place(hh) == 0 else [zero, src[:, hh]]
        return jnp.concatenate(parts, axis=1)

    misc = jnp.concatenate([cols("ki"), cols("wi"), cols("fb"),
                            jnp.zeros((d, LANES - D_IDX - H_IDX - H_B), w_in.dtype)], axis=1)
    if slots:
        qa = slot("qa", lambda hh: hh // (H_A // KV_A))
        qi = slot("qi", lambda hh: 0)
        qb = slot("qb", lambda hh: hh % 2)
    else:
        qa, qi, qb = cols("qa"), cols("qi"), cols("qb")
    parts = [qa, qi, qb, cols("kb"), cols("vb"), cols("ka"), cols("va"), misc]
    groups = (("qa", qa.shape[1], True, Q_SCALE), ("qi", qi.shape[1], True, D_IDX ** -0.5),
              ("qb", qb.shape[1], False, Q_SCALE), ("kb", H_B * HEAD_DIM, False, 1.0),
              ("vb", H_B * HEAD_DIM, False, 1.0), ("ka", KV_A * HEAD_DIM, True, 1.0),
              ("va", KV_A * HEAD_DIM, False, 1.0), ("misc", LANES, False, 1.0))
    return jnp.concatenate(parts, axis=1).astype(_MM), groups


def _inproj(grp, layer, x, norm_g, w_packed, groups, b_f, tables, tm, cum_t=0):
    n, d = x.shape
    nt = n // tm
    sh_arr, sh_spec = grp.mod_arg(layer, tm, 0)
    sc_arr, sc_spec = grp.mod_arg(layer, tm, 1)
    c, s1, s2 = tables
    tr = c.shape[0] // tm
    tspec = BS((tm, LANES), lambda i: (i % tr, 0))
    bf = jnp.zeros((1, LANES), F32).at[0, MISC_FB:MISC_FB + H_B].set(b_f)
    out_dtypes = {"qa": _MM, "qi": _MM, "qb": _MM}
    out_shape = [SDS((n, w), out_dtypes.get(name, F32)) for name, w, _, _ in groups]
    out_specs = [BS((tm, w), lambda i: (i, 0)) for _, w, _, _ in groups]
    nc = w_packed.shape[1]
    return pl.pallas_call(
        functools.partial(_inproj_kernel, groups=groups, cum_t=cum_t),
        out_shape=out_shape,
        grid=(nt,),
        in_specs=[BS((tm, d), lambda i: (i, 0)), BS((1, d), lambda i: (0, 0)), sh_spec, sc_spec,
                  BS((d, nc), lambda i: (0, 0)), tspec, tspec, tspec, BS((1, LANES), lambda i: (0, 0))],
        out_specs=out_specs,
        compiler_params=_cp("arbitrary"),
        name="even_inproj",
    )(x, norm_g.reshape(1, d), sh_arr, sc_arr, w_packed, c, s1, s2, bf)


def _tri_lower(n):
    r = lax.broadcasted_iota(I32, (n, n), 0)
    c = lax.broadcasted_iota(I32, (n, n), 1)
    return jnp.where(r >= c, 1.0, 0.0).astype(F32)


def _dot_f32(a, b):
    return jnp.dot(a, b, preferred_element_type=F32, precision=lax.Precision.HIGHEST)


def _cumsum_p_kernel(lf_ref, f_ref, *, tc):
    t = lf_ref.shape[1]
    tri = _tri_lower(tc)
    carry = jnp.zeros((1, lf_ref.shape[2]), F32)
    for c in range(t // tc):
        fc = _dot_f32(tri, lf_ref[0, c * tc:(c + 1) * tc, :]) + carry
        f_ref[0, c * tc:(c + 1) * tc, :] = fc
        carry = fc[tc - 1:tc, :]


def _cumsum_prompt(logf):
    b, t, hb = logf.shape
    tc = min(256, t)
    return pl.pallas_call(
        functools.partial(_cumsum_p_kernel, tc=tc),
        out_shape=SDS((b, t, hb), F32),
        grid=(b,),
        in_specs=[BS((1, t, hb), lambda i: (i, 0, 0))],
        out_specs=BS((1, t, hb), lambda i: (i, 0, 0)),
        compiler_params=_cp("arbitrary"),
        name="cumsum_prompt",
    )(logf)


def _fox_p_kernel(q_ref, k_ref, v_ref, fq_ref, fk_ref, o_ref, m_sc, l_sc, acc_sc, fq_sc, *, tq, tk):
    i = pl.program_id(1)
    j = pl.program_id(2)
    nk = pl.num_programs(2)

    @pl.when(j == 0)
    def _():
        m_sc[...] = jnp.full(m_sc.shape, NEG_INF, F32)
        l_sc[...] = jnp.zeros(l_sc.shape, F32)
        acc_sc[...] = jnp.zeros(acc_sc.shape, F32)
        fq = fq_ref[0]
        for hh in range(H_B):
            fq_sc[hh] = jnp.broadcast_to(fq[:, hh:hh + 1], (tq, LANES))

    def step(masked):
        k = k_ref[0].astype(_MM)
        v = v_ref[0].astype(_MM)
        fk = fk_ref[0]
        if masked:
            rows = i * tq + lax.broadcasted_iota(I32, (tq, tk), 0)
            cols = j * tk + lax.broadcasted_iota(I32, (tq, tk), 1)
            causal = cols <= rows
        lane = lax.broadcasted_iota(I32, (tq, LANES), 1)
        low = lane < HEAD_DIM
        for p in range(H_B // 2):
            kp = k[:, p * LANES:(p + 1) * LANES]
            vp = v[:, p * LANES:(p + 1) * LANES]
            alphas, pvs = [], []
            for e in range(2):
                hh = 2 * p + e
                z = _dot_nt(q_ref[0, :, hh * LANES:(hh + 1) * LANES], kp) - fk[hh:hh + 1, :]
                if masked:
                    z = jnp.where(causal, z, NEG_INF)
                fq = fq_sc[hh]
                m_prev = m_sc[hh]
                m_new = jnp.maximum(m_prev, fq + jnp.max(z, axis=1, keepdims=True))
                alpha = jnp.exp(m_prev - m_new)
                pe = jnp.exp(z + jnp.concatenate([fq - m_new] * (tk // LANES), axis=1))
                l_sc[hh] = alpha * l_sc[hh] + jnp.sum(pe, axis=1, keepdims=True)
                m_sc[hh] = m_new
                alphas.append(alpha)
                pvs.append(_dot(pe.astype(_MM), vp))
            acc_sc[p] = jnp.where(low, alphas[0], alphas[1]) * acc_sc[p] + jnp.where(low, pvs[0], pvs[1])

    visible = (j + 1) * tk <= i * tq + 1

    @pl.when(visible)
    def _():
        step(False)

    @pl.when(jnp.logical_not(visible) & (j * tk < (i + 1) * tq))
    def _():
        step(True)

    @pl.when(j == nk - 1)
    def _():
        lane = lax.broadcasted_iota(I32, (tq, LANES), 1)
        low = lane < HEAD_DIM
        for p in range(H_B // 2):
            linv = jnp.where(low, 1.0 / l_sc[2 * p], 1.0 / l_sc[2 * p + 1])
            o_ref[0, :, p * LANES:(p + 1) * LANES] = (acc_sc[p] * linv).astype(o_ref.dtype)


def _fox_prompt(qb_s, kb, vb, f, ft, b, t):
    tq, tk = min(FOX_TQ, t), min(FOX_TK, t)
    nq, nk = t // tq, t // tk
    dq = qb_s.shape[1]
    dk = kb.shape[1]
    kmap = lambda bb, i, j: (bb, jnp.minimum(j, ((i + 1) * tq - 1) // tk), 0)
    return pl.pallas_call(
        functools.partial(_fox_p_kernel, tq=tq, tk=tk),
        out_shape=SDS((b, t, dk), _MM),
        grid=(b, nq, nk),
        in_specs=[BS((1, tq, dq), lambda bb, i, j: (bb, i, 0)),
                  BS((1, tk, dk), kmap), BS((1, tk, dk), kmap),
                  BS((1, tq, H_B), lambda bb, i, j: (bb, i, 0)),
                  BS((1, H_B, tk), lambda bb, i, j: (bb, 0, jnp.minimum(j, ((i + 1) * tq - 1) // tk)))],
        out_specs=BS((1, tq, dk), lambda bb, i, j: (bb, i, 0)),
        scratch_shapes=[pltpu.VMEM((H_B, tq, LANES), F32), pltpu.VMEM((H_B, tq, LANES), F32),
                        pltpu.VMEM((H_B // 2, tq, LANES), F32), pltpu.VMEM((H_B, tq, LANES), F32)],
        compiler_params=_cp("arbitrary", "arbitrary", "arbitrary"),
        name="fox_prompt",
    )(qb_s.reshape(b, t, dq), kb.reshape(b, t, dk), vb.reshape(b, t, dk), f, ft)


def _order_key(score):
    bits = pltpu.bitcast(score + 0.0, I32)
    return jnp.where(bits < 0, bits ^ jnp.int32(0x7FFFFFFF), bits)


def _kth_largest(load, nchunks, k_f, bits_per_pass=1):
    rows, width = load(0).shape
    rc = min(rows, COUNT_ROWS)

    def count_ge(cand):
        accs = []
        for r0 in range(0, rows, rc):
            acc = None
            for c in range(nchunks):
                x = jnp.where(load(c, r0, rc) >= cand[r0:r0 + rc], 1.0, 0.0)
                for j in range(width // LANES):
                    piece = x[:, j * LANES:(j + 1) * LANES]
                    acc = piece if acc is None else acc + piece
            accs.append(acc)
        folded = accs[0] if len(accs) == 1 else jnp.concatenate(accs, axis=0)
        return jnp.sum(folded, axis=1, keepdims=True)

    def body(it, tau):
        shift = 32 - bits_per_pass * (it + 1)
        best = tau
        for digit in range(1, 2 ** bits_per_pass):
            cand = tau ^ jnp.left_shift(jnp.int32(digit), shift)
            best = jnp.where(count_ge(cand) >= k_f, cand, best)
        return best

    tau = lax.fori_loop(0, 32 // bits_per_pass, body, jnp.full((rows, 1), INT_MIN, I32))
    return tau, count_ge(tau)


def _select_with_ties(load, store, nchunks, width, tau, k_f):
    n_gt = None
    for c in range(nchunks):
        x = jnp.sum(jnp.where(load(c) > tau, 1.0, 0.0), axis=1, keepdims=True)
        n_gt = x if n_gt is None else n_gt + x
    need = k_f - n_gt
    r = lax.broadcasted_iota(I32, (width, width), 0)
    cc = lax.broadcasted_iota(I32, (width, width), 1)
    upper = jnp.where(r <= cc, 1.0, 0.0).astype(_MM)
    carry = jnp.zeros_like(need)
    for c in range(nchunks):
        key = load(c)
        eq = key == tau
        prefix = _dot(jnp.where(eq, 1.0, 0.0).astype(_MM), upper) + carry
        store(c, (key > tau) | (eq & (prefix <= need)))
        carry = prefix[:, width - 1:width]


def _dsa_p_kernel(qi_ref, qa_ref, mq_ref, mk_ref, ka_ref, va_ref, o_ref, key_sc, bias_sc,
                  *, tq, lk, q0, k_top, cw):
    i = pl.program_id(1)
    nch = lk // cw
    qpos = q0 + i * tq + lax.broadcasted_iota(I32, (tq, 1), 0)
    wi = mq_ref[0][:, MISC_WI:MISC_WI + H_IDX] * (H_IDX ** -0.5)
    for c in range(nch):
        kmat = mk_ref[0, c * cw:(c + 1) * cw, :].astype(_MM)
        score = jnp.zeros((tq, cw), F32)
        for hh in range(H_IDX):
            s = _dot_nt(qi_ref[0, :, hh * LANES:(hh + 1) * LANES], kmat)
            score = score + jnp.maximum(s, 0.0) * wi[:, hh:hh + 1]
        kpos = c * cw + lax.broadcasted_iota(I32, (tq, cw), 1)
        key_sc[:, c * cw:(c + 1) * cw] = jnp.where(kpos <= qpos, _order_key(score), INT_MIN)

    load = lambda c, r0=0, nr=tq: key_sc[r0:r0 + nr, c * cw:(c + 1) * cw]
    k_f = jnp.minimum(k_top, qpos + 1).astype(F32)
    tau, cnt = _kth_largest(load, nch, k_f)
    has_tie = jnp.max(cnt - k_f) > 0.0

    def store(c, sel):
        bias_sc[:, c * cw:(c + 1) * cw] = jnp.where(sel, 0.0, NEG_INF)

    @pl.when(jnp.logical_not(has_tie))
    def _():
        for c in range(nch):
            store(c, load(c) >= tau)

    @pl.when(has_tie)
    def _():
        _select_with_ties(load, store, nch, cw, tau, k_f)

    ka = ka_ref[0].astype(_MM)
    va = va_ref[0].astype(_MM)
    bias = bias_sc[...]
    lane = lax.broadcasted_iota(I32, (tq, LANES), 1)
    for hh in range(H_A):
        grp = hh // (H_A // KV_A)
        lg = _dot_nt(qa_ref[0, :, hh * LANES:(hh + 1) * LANES], ka) + bias
        m = jnp.max(lg, axis=1, keepdims=True)
        pe = jnp.exp(lg - m)
        l = jnp.sum(pe, axis=1, keepdims=True)
        o = _dot(pe.astype(_MM), va) * (1.0 / l)
        o = jnp.where((lane >= grp * HEAD_DIM) & (lane < (grp + 1) * HEAD_DIM), o, 0.0)
        o_ref[0, :, hh * LANES:(hh + 1) * LANES] = o.astype(o_ref.dtype)


def _dsa_prompt(qi_s, qa_s, misc, ka, va, b, t, k_top):
    bands = min(DSA_BANDS, t // LANES)
    band = t // bands
    tq = min(DSA_TQ, band)
    cw = min(DSA_CHUNK, band)
    dq = qi_s.shape[1]
    qi3, qa3, misc3 = qi_s.reshape(b, t, dq), qa_s.reshape(b, t, dq), misc.reshape(b, t, LANES)
    ka3, va3 = ka.reshape(b, t, LANES), va.reshape(b, t, LANES)
    kmap = lambda bb, i: (bb, 0, 0)
    outs = []
    for c in range(bands):
        lk = (c + 1) * band
        q_first = c * (band // tq)
        qmap = lambda bb, i, q_first=q_first: (bb, q_first + i, 0)
        outs.append(pl.pallas_call(
            functools.partial(_dsa_p_kernel, tq=tq, lk=lk, q0=c * band, k_top=k_top, cw=cw),
            out_shape=SDS((b, band, dq), _MM),
            grid=(b, band // tq),
            in_specs=[BS((1, tq, dq), qmap), BS((1, tq, dq), qmap), BS((1, tq, LANES), qmap),
                      BS((1, lk, LANES), kmap), BS((1, lk, LANES), kmap), BS((1, lk, LANES), kmap)],
            out_specs=BS((1, tq, dq), lambda bb, i: (bb, i, 0)),
            scratch_shapes=[pltpu.VMEM((tq, lk), I32), pltpu.VMEM((tq, lk), F32)],
            compiler_params=_cp("arbitrary", "arbitrary"),
            name="dsa_prompt",
        )(qi3, qa3, misc3, misc3, ka3, va3))
    return jnp.concatenate(outs, axis=1)


def _outproj_kernel(oa_ref, ob_ref, wa_ref, wb_ref, x_ref, gate_ref, o_ref):
    y = _dot(oa_ref[...], wa_ref[...]) + _dot(ob_ref[...], wb_ref[...])
    o_ref[...] = x_ref[...] + _rows(gate_ref[0], y.shape[0]) * y


def _outproj(grp, layer, oa, ob, wa, wb, x, tm):
    n, d = x.shape
    g_arr, g_spec = grp.mod_arg(layer, tm, 2)
    da, db = oa.shape[1], ob.shape[1]
    return pl.pallas_call(
        _outproj_kernel,
        out_shape=SDS((n, d), F32),
        grid=(n // tm,),
        in_specs=[BS((tm, da), lambda i: (i, 0)), BS((tm, db), lambda i: (i, 0)),
                  BS((da, d), lambda i: (0, 0)), BS((db, d), lambda i: (0, 0)),
                  BS((tm, d), lambda i: (i, 0)), g_spec],
        out_specs=BS((tm, d), lambda i: (i, 0)),
        compiler_params=_cp("arbitrary"),
        name="even_outproj",
    )(oa, ob, wa, wb, x, g_arr)


def _ffn_kernel(x_ref, g_ref, sh_ref, sc_ref, gate_ref, wi_ref, wo_ref, gf_ref, o_ref, *, final, tf):
    ff = wo_ref.shape[0]
    x = x_ref[...]
    n_rows = x.shape[0]
    h = _modulate(x, g_ref[...], _rows(sh_ref[0], n_rows), _rows(sc_ref[0], n_rows)).astype(_MM)
    acc = None
    for c in range(ff // tf):
        gate_part = _dot(h, wi_ref[:, c * tf:(c + 1) * tf])
        up_part = _dot(h, wi_ref[:, ff + c * tf:ff + (c + 1) * tf])
        a = (_silu(gate_part) * up_part).astype(_MM)
        part = _dot(a, wo_ref[c * tf:(c + 1) * tf, :])
        acc = part if acc is None else acc + part
    y = x + _rows(gate_ref[0], n_rows) * acc
    if final:
        y = _rms(y, gf_ref[...])
    o_ref[...] = y


def _ffn(grp, layer, x, norm_g, w_in, w_out, norm_final, final, tm, tf):
    n, d = x.shape
    ff = w_out.shape[0]
    sh_arr, sh_spec = grp.mod_arg(layer, tm, 3)
    sc_arr, sc_spec = grp.mod_arg(layer, tm, 4)
    g_arr, g_spec = grp.mod_arg(layer, tm, 5)
    resident = lambda shape: BS(shape, lambda i: (0, 0), pipeline_mode=pl.Buffered(1))
    return pl.pallas_call(
        functools.partial(_ffn_kernel, final=final, tf=tf),
        out_shape=SDS((n, d), F32),
        grid=(n // tm,),
        in_specs=[BS((tm, d), lambda i: (i, 0)), BS((1, d), lambda i: (0, 0)), sh_spec, sc_spec, g_spec,
                  resident((d, 2 * ff)), resident((ff, d)), BS((1, d), lambda i: (0, 0))],
        out_specs=BS((tm, d), lambda i: (i, 0)),
        compiler_params=_cp("arbitrary"),
        name="ffn",
    )(x, norm_g.reshape(1, d), sh_arr, sc_arr, g_arr, w_in, w_out, norm_final.reshape(1, d))


def _pw1_kernel(x_ref, g_ref, sh_ref, sc_ref, w_ref, b_ref, u_ref):
    x = x_ref[...]
    n_rows = x.shape[0]
    h = _modulate(x, g_ref[...], _rows(sh_ref[0], n_rows), _rows(sc_ref[0], n_rows)).astype(_MM)
    y = _dot(h, w_ref[...]) + b_ref[...]
    dc = y.shape[1] // 2
    u_ref[...] = y[:, :dc] * _sigmoid(y[:, dc:])


def _pw1(grp, layer, x, norm_g, w, bias, tm):
    n, d = x.shape
    dc2 = w.shape[1]
    sh_arr, sh_spec = grp.mod_arg(layer, tm, 0)
    sc_arr, sc_spec = grp.mod_arg(layer, tm, 1)
    return pl.pallas_call(
        _pw1_kernel,
        out_shape=SDS((n, dc2 // 2), F32),
        grid=(n // tm,),
        in_specs=[BS((tm, d), lambda i: (i, 0)), BS((1, d), lambda i: (0, 0)), sh_spec, sc_spec,
                  BS((d, dc2), lambda i: (0, 0)), BS((1, dc2), lambda i: (0, 0))],
        out_specs=BS((tm, dc2 // 2), lambda i: (i, 0)),
        compiler_params=_cp("arbitrary"),
        name="conv_pw1_glu",
    )(x, norm_g.reshape(1, d), sh_arr, sc_arr, w, bias.reshape(1, dc2))


def _ln_swish_pw2(z, lng, lnb, w2, b2):
    mu = jnp.mean(z, axis=-1, keepdims=True)
    zc = z - mu
    var = jnp.mean(zc * zc, axis=-1, keepdims=True)
    zn = zc * lax.rsqrt(var + EPS) * lng + lnb
    return _dot(_silu(zn).astype(_MM), w2) + b2


HALO = 32


SUBLANES = 8
CONV_ROWS = 32


def _conv_p_kernel(ucur_ref, uhalo_ref, wdw_ref, bdw_ref, lng_ref, lnb_ref, w2_ref, b2_ref, x_ref, gate_ref,
                   o_ref, full_sc, shift_sc, z_sc, *, tm):
    i = pl.program_id(1)
    dc = full_sc.shape[1]
    full_sc[0:HALO, :] = jnp.where(i > 0, uhalo_ref[0], 0.0)
    full_sc[HALO:HALO + tm, :] = ucur_ref[0]
    span = shift_sc.shape[1]
    for q in range(1, SUBLANES):
        shift_sc[q - 1] = full_sc[q:q + span, :]
    base = HALO - (CONV_WIDTH - 1)

    def chunk(c, carry):
        r0 = pl.multiple_of(c * CONV_ROWS, CONV_ROWS)
        groups = CONV_ROWS // SUBLANES
        z = jnp.zeros((groups, SUBLANES, dc), F32) + bdw_ref[...]
        for w in range(CONV_WIDTH):
            q, al = (base + w) % SUBLANES, ((base + w) // SUBLANES) * SUBLANES
            src = full_sc if q == 0 else shift_sc.at[q - 1]
            slab = src[pl.ds(r0 + al, CONV_ROWS), :].reshape(groups, SUBLANES, dc)
            z = z + slab * wdw_ref[w][None]
        z_sc[pl.ds(r0, CONV_ROWS), :] = z.reshape(CONV_ROWS, dc)
        return carry

    lax.fori_loop(0, tm // CONV_ROWS, chunk, 0)
    y = _ln_swish_pw2(z_sc[...], lng_ref[...], lnb_ref[...], w2_ref[...], b2_ref[...])
    o_ref[0] = x_ref[0] + gate_ref[0] * y


def _conv_prompt(grp, layer, u, x, wdw, bdw, lng, lnb, w2, b2, tm):
    b, t, d = grp.nb, grp.t, grp.d
    dc = u.shape[1]
    g_arr, _ = grp.mod_arg(layer, tm, 2)
    hb = tm // HALO
    wrep = jnp.broadcast_to(wdw[:, None, :], (CONV_WIDTH, SUBLANES, dc))
    vec = lambda bb, i: (0, 0)
    return pl.pallas_call(
        functools.partial(_conv_p_kernel, tm=tm),
        out_shape=SDS((b, t, d), F32),
        grid=(b, t // tm),
        in_specs=[BS((1, tm, dc), lambda bb, i: (bb, i, 0)),
                  BS((1, HALO, dc), lambda bb, i: (bb, jnp.maximum(i * hb - 1, 0), 0)),
                  BS((CONV_WIDTH, SUBLANES, dc), lambda bb, i: (0, 0, 0)),
                  BS((1, dc), vec), BS((1, dc), vec), BS((1, dc), vec),
                  BS((dc, d), vec), BS((1, d), vec),
                  BS((1, tm, d), lambda bb, i: (bb, i, 0)),
                  BS((1, 1, d), lambda bb, i: (bb, 0, 2))],
        out_specs=BS((1, tm, d), lambda bb, i: (bb, i, 0)),
        scratch_shapes=[pltpu.VMEM((HALO + tm, dc), F32),
                        pltpu.VMEM((SUBLANES - 1, HALO + tm - SUBLANES, dc), F32),
                        pltpu.VMEM((tm, dc), F32)],
        compiler_params=_cp("arbitrary", "arbitrary"),
        name="conv_prompt",
    )(u.reshape(b, t, dc), u.reshape(b, t, dc), wrep, bdw.reshape(1, dc), lng.reshape(1, dc), lnb.reshape(1, dc),
      w2, b2.reshape(1, d), x.reshape(b, t, d), g_arr).reshape(b * t, d)


def _conv_s_kernel(buf_ref, u_ref, wdw_ref, bdw_ref, lng_ref, lnb_ref, w2_ref, b2_ref, x_ref, gate_ref, o_ref):
    nbuf = buf_ref.shape[0]
    t = u_ref.shape[0]
    zs = []
    for tt in range(t):
        z = jnp.zeros(u_ref.shape[1:], F32) + bdw_ref[...]
        for w in range(CONV_WIDTH):
            src = tt + w
            row = buf_ref[src] if src < nbuf else u_ref[src - nbuf]
            z = z + row * wdw_ref[w:w + 1, :]
        zs.append(z)
    y = _ln_swish_pw2(jnp.concatenate(zs, axis=0), lng_ref[...], lnb_ref[...], w2_ref[...], b2_ref[...])
    bb = u_ref.shape[1]
    for tt in range(t):
        o_ref[tt] = x_ref[tt] + gate_ref[...] * y[tt * bb:(tt + 1) * bb, :]


def _conv_sample(buf_t, u_t, x_t, gate, wdw, bdw, lng, lnb, w2, b2):
    nbuf, db, dc = buf_t.shape
    t, _, d = x_t.shape
    bb = min(32, db)
    wpad = jnp.zeros((HALO, dc), F32).at[:CONV_WIDTH].set(wdw)
    vec = lambda j: (0, 0)
    return pl.pallas_call(
        _conv_s_kernel,
        out_shape=SDS((t, db, d), F32),
        grid=(db // bb,),
        in_specs=[BS((nbuf, bb, dc), lambda j: (0, j, 0)), BS((t, bb, dc), lambda j: (0, j, 0)),
                  BS((HALO, dc), vec), BS((1, dc), vec), BS((1, dc), vec), BS((1, dc), vec),
                  BS((dc, d), vec), BS((1, d), vec),
                  BS((t, bb, d), lambda j: (0, j, 0)), BS((bb, d), lambda j: (j, 2))],
        out_specs=BS((t, bb, d), lambda j: (0, j, 0)),
        compiler_params=_cp("arbitrary"),
        name="conv_sample",
    )(buf_t, u_t, wpad, bdw.reshape(1, dc), lng.reshape(1, dc), lnb.reshape(1, dc), w2, b2.reshape(1, d), x_t, gate)


def _per_head_rows(x, t):
    nh, w = x.shape
    row = lax.broadcasted_iota(I32, (nh * t, w), 0) // t
    out = jnp.zeros((nh * t, w), x.dtype)
    for hh in range(nh):
        out = jnp.where(row == hh, x[hh:hh + 1, :], out)
    return out


def _strict_lower(n):
    r = lax.broadcasted_iota(I32, (n, n), 0)
    c = lax.broadcasted_iota(I32, (n, n), 1)
    return jnp.where(r > c, 1.0, 0.0).astype(F32)


def _attn_s_kernel(pt_ref, qb_ref, qa_ref, qi_ref, wi_ref, cq_ref, cqt_ref, kbn_ref, vbn_ref, kan_ref, van_ref,
                   kin_ref, *rest, past, k_top, t, gp, nb):
    npage = 6 * gp * nb
    page_refs = rest[:npage]
    oa_ref, ob_ref = rest[npage:npage + 2]
    (m_sc, l_sc, acc_sc, r_sc, ka_sc, va_sc, key_sc, sel_sc,
     kbn_sc, vbn_sc, kan_sc, van_sc, kin_sc) = rest[npage + 2:]
    first = (pl.program_id(0) == 0) & (pl.program_id(1) == 0)
    s = pl.program_id(1)
    ns = ka_sc.shape[1]
    ps = kin_sc.shape[1]
    w = gp * ps
    rows = H_B * t
    nt8 = kbn_ref.shape[1]
    row_tok = lax.broadcasted_iota(I32, (rows, ps), 0) % t
    lane_r = lax.broadcasted_iota(I32, (rows, ps), 1)
    row8_tok = lax.broadcasted_iota(I32, (8, ps), 0) % t
    lane8 = lax.broadcasted_iota(I32, (8, ps), 1)
    chunk = ns - 1 - s

    def dup_scores(s32, bi):
        s32 = jnp.maximum(s32, 0.0) * (wi_ref[bi] * (H_IDX ** -0.5))
        sc = s32[0:t]
        for hh in range(1, H_IDX):
            sc = sc + s32[hh * t:(hh + 1) * t]
        return jnp.concatenate([sc] * (8 // t), axis=0)

    def key_rows(bi):
        return slice(8 * bi, 8 * (bi + 1))

    @pl.when(first)
    def _():
        for ref in (kbn_sc, vbn_sc, kan_sc, van_sc, kin_sc):
            ref[...] = jnp.zeros(ref.shape, F32)

    def new_rows(bi):
        kbn_sc[bi, 0:nt8, :] = kbn_ref[bi]
        vbn_sc[bi, 0:nt8, :] = vbn_ref[bi]
        kan_sc[bi, 0:nt8, :] = kan_ref[bi]
        van_sc[bi, 0:nt8, :] = van_ref[bi]
        kin_sc[bi, 0:nt8, :] = kin_ref[bi]
        r_sc[bi] = jnp.zeros(r_sc.shape[1:], F32)
        lg = _dot_nt(qb_ref[bi], kbn_sc[bi].astype(_MM)) + cq_ref[bi] - _per_head_rows(cqt_ref[bi], t)
        lg = jnp.where(lane_r <= row_tok, lg, NEG_INF)
        m = jnp.max(lg, axis=1, keepdims=True)
        pe = jnp.exp(lg - m)
        m_sc[bi] = m
        l_sc[bi] = jnp.sum(pe, axis=1, keepdims=True)
        acc_sc[bi] = _dot(pe.astype(_MM), vbn_sc[bi].astype(_MM))
        key_new = _order_key(dup_scores(_dot_nt(qi_ref[bi], kin_sc[bi].astype(_MM)), bi))
        key_new = jnp.where(lane8 <= row8_tok, key_new, INT_MIN)
        if gp > 1:
            key_new = jnp.concatenate([key_new, jnp.full((8, w - ps), INT_MIN, I32)], axis=1)
        key_sc[ns, key_rows(bi), :] = key_new

    @pl.when(s == 0)
    def _():
        for bi in range(nb):
            new_rows(bi)

    def pages(bi):
        refs = [page_refs[6 * (bi * gp + g):6 * (bi * gp + g) + 6] for g in range(gp)]
        cat = lambda parts: jnp.concatenate(parts, axis=1) if gp > 1 else parts[0]
        kt = cat([r[0][0, 0].reshape(H_B * HEAD_DIM, ps).astype(_MM) for r in refs])
        vt = cat([r[1][0, 0].reshape(H_B * HEAD_DIM, ps).astype(_MM) for r in refs])
        ka_sc[bi, chunk] = cat([r[2][0, 0].reshape(KV_A * HEAD_DIM, ps).astype(_MM) for r in refs])
        va_sc[bi, chunk] = cat([r[3][0, 0].reshape(KV_A * HEAD_DIM, ps).astype(_MM) for r in refs])
        kit = cat([r[4][0, 0].astype(_MM) for r in refs])
        lfts = [r[5][0, 0].astype(F32) for r in refs]

        suf_loc = _dot_f32(jnp.concatenate(lfts, axis=0), _strict_lower(ps))
        carry = r_sc[bi]
        sufs = [None] * gp
        for g in reversed(range(gp)):
            loc = suf_loc[g * H_B:(g + 1) * H_B]
            sufs[g] = loc + carry
            carry = carry + loc[:, 0:1] + lfts[g][:, 0:1]
        r_sc[bi] = carry

        lg = _dot(qb_ref[bi], kt) + (_per_head_rows(cat(sufs), t) + cq_ref[bi])
        m_prev = m_sc[bi]
        m_new = jnp.maximum(m_prev, jnp.max(lg, axis=1, keepdims=True))
        alpha = jnp.exp(m_prev - m_new)
        pe = jnp.exp(lg - m_new)
        l_sc[bi] = alpha * l_sc[bi] + jnp.sum(pe, axis=1, keepdims=True)
        acc_sc[bi] = alpha * acc_sc[bi] + _dot_nt(pe.astype(_MM), vt)
        m_sc[bi] = m_new
        key_sc[chunk, key_rows(bi), :] = _order_key(dup_scores(_dot(qi_ref[bi], kit), bi))

    for bi in range(nb):
        pages(bi)

    def fox_out(bi):
        o = acc_sc[bi] * (1.0 / l_sc[bi])
        lane_h = lax.broadcasted_iota(I32, o.shape, 1) // HEAD_DIM
        row_h = lax.broadcasted_iota(I32, o.shape, 0) // t
        o = jnp.where(lane_h == row_h, o, 0.0)
        ob = o[0:t]
        for hh in range(1, H_B):
            ob = ob + o[hh * t:(hh + 1) * t]
        ob_ref[bi] = ob

    def dsa_out(bi):
        nch = ns + 1
        qa = qa_ref[bi]
        lgs = []
        for c in range(nch):
            bias = sel_sc[c, key_rows(bi), :]
            if c < ns:
                prod = _dot(qa, ka_sc[bi, c])
            else:
                prod, bias = _dot_nt(qa, kan_sc[bi].astype(_MM)), bias[:, 0:ps]
            lgs.append(prod + jnp.concatenate([bias] * (rows // 8), axis=0))
        m = jnp.max(lgs[0], axis=1, keepdims=True)
        for c in range(1, nch):
            m = jnp.maximum(m, jnp.max(lgs[c], axis=1, keepdims=True))
        lsum = jnp.zeros((rows, 1), F32)
        out = jnp.zeros((rows, KV_A * HEAD_DIM), F32)
        for c in range(nch):
            pe = jnp.exp(lgs[c] - m)
            lsum = lsum + jnp.sum(pe, axis=1, keepdims=True)
            if c < ns:
                out = out + _dot_nt(pe.astype(_MM), va_sc[bi, c])
            else:
                out = out + _dot(pe.astype(_MM), van_sc[bi].astype(_MM))
        out = out * (1.0 / lsum)
        lane_j = lax.broadcasted_iota(I32, (t, KV_A * HEAD_DIM), 1) // HEAD_DIM
        for hh in range(H_A):
            piece = out[hh * t:(hh + 1) * t, :]
            oa_ref[bi, :, hh * LANES:(hh + 1) * LANES] = jnp.where(lane_j == hh // (H_A // KV_A), piece, 0.0)

    @pl.when(s == ns - 1)
    def _():
        for bi in range(nb):
            fox_out(bi)
        nch = ns + 1
        qpos = past + lax.broadcasted_iota(I32, (8 * nb, 1), 0) % t
        k_f = jnp.minimum(k_top, qpos + 1).astype(F32)
        load = lambda c, r0=0, nr=8 * nb: key_sc[c, r0:r0 + nr, :]
        tau, cnt = _kth_largest(load, nch, k_f, bits_per_pass=2)
        has_tie = jnp.max(cnt - k_f) > 0.0

        def store(c, sel):
            sel_sc[c] = jnp.where(sel, 0.0, NEG_INF)

        @pl.when(jnp.logical_not(has_tie))
        def _():
            for c in range(nch):
                store(c, load(c) >= tau)

        @pl.when(has_tie)
        def _():
            _select_with_ties(load, store, nch, w, tau, k_f)

        for bi in range(nb):
            dsa_out(bi)


def _attn_sample(layer, page_table, caches_t, qbd, qabd, qi32, wi32, cq32, cqt, kbn, vbn, kan, van, kin, k_top, t):
    kb_t, vb_t, ka_t, va_t, ki_t, lf_t = caches_t
    db, npg = page_table.shape
    ps = ki_t.shape[-1]
    rows = H_B * t
    gp = PAGES_PER_STEP if npg % PAGES_PER_STEP == 0 else 1
    nb = BATCHES_PER_STEP if db % BATCHES_PER_STEP == 0 else 1
    ns = npg // gp
    w = gp * ps
    const3 = lambda b, s, pt: (b, 0, 0)
    full = lambda a: BS((nb,) + a.shape[1:], const3)

    def page_specs(bi, g):
        def pg(b, s, pt):
            return pt[b * nb + bi, npg - (s + 1) * gp + g]
        five = lambda nh: BS((1, 1, nh, HEAD_DIM, ps), lambda b, s, pt: (layer, pg(b, s, pt), 0, 0, 0))
        return [five(H_B), five(H_B), five(KV_A), five(KV_A),
                BS((1, 1, D_IDX, ps), lambda b, s, pt: (layer, pg(b, s, pt), 0, 0)),
                BS((1, 1, H_B, ps), lambda b, s, pt: (layer, pg(b, s, pt), 0, 0))]

    small = [qbd, qabd, qi32, wi32, cq32, cqt, kbn, vbn, kan, van, kin]
    in_specs = [full(a) for a in small]
    pages = []
    for bi in range(nb):
        for g in range(gp):
            in_specs += page_specs(bi, g)
            pages += [kb_t, vb_t, ka_t, va_t, ki_t, lf_t]
    da = H_A * LANES
    db_ = H_B * HEAD_DIM
    dj = KV_A * HEAD_DIM
    return pl.pallas_call(
        functools.partial(_attn_s_kernel, past=npg * ps, k_top=k_top, t=t, gp=gp, nb=nb),
        out_shape=[SDS((db, t, da), F32), SDS((db, t, db_), F32)],
        grid_spec=pltpu.PrefetchScalarGridSpec(
            num_scalar_prefetch=1,
            grid=(db // nb, ns),
            in_specs=in_specs,
            out_specs=[BS((nb, t, da), const3), BS((nb, t, db_), const3)],
            scratch_shapes=[pltpu.VMEM((nb, rows, 1), F32), pltpu.VMEM((nb, rows, 1), F32),
                            pltpu.VMEM((nb, rows, db_), F32), pltpu.VMEM((nb, H_B, 1), F32),
                            pltpu.VMEM((nb, ns, dj, w), _MM), pltpu.VMEM((nb, ns, dj, w), _MM),
                            pltpu.VMEM((ns + 1, 8 * nb, w), I32), pltpu.VMEM((ns + 1, 8 * nb, w), F32),
                            pltpu.VMEM((nb, ps, db_), F32), pltpu.VMEM((nb, ps, db_), F32),
                            pltpu.VMEM((nb, ps, dj), F32), pltpu.VMEM((nb, ps, dj), F32),
                            pltpu.VMEM((nb, ps, D_IDX), F32)]),
        compiler_params=_cp("arbitrary", "arbitrary"),
        name="attn_sample",
    )(page_table, *small, *pages)


def _old2_attn_s_kernel(pt_ref, qb_ref, qa_ref, qi_ref, wi_ref, cq_ref, cqt_ref, kbn_ref, vbn_ref, kan_ref, van_ref,
                   kin_ref, *rest, past, k_top, t, gp):
    page_refs = rest[:6 * gp]
    oa_ref, ob_ref = rest[6 * gp:6 * gp + 2]
    (m_sc, l_sc, acc_sc, r_sc, kt_sc, vt_sc, ki_sc, ka_sc, va_sc, key_sc, sel_sc,
     kbn_sc, vbn_sc, kan_sc, van_sc, kin_sc) = rest[6 * gp + 2:]
    b = pl.program_id(0)
    s = pl.program_id(1)
    ns = ka_sc.shape[0]
    ps = kin_sc.shape[0]
    w = gp * ps
    rows = H_B * t
    nt8 = kbn_ref.shape[1]
    row_tok = lax.broadcasted_iota(I32, (rows, ps), 0) % t
    lane_r = lax.broadcasted_iota(I32, (rows, ps), 1)
    row8_tok = lax.broadcasted_iota(I32, (8, ps), 0) % t
    lane8 = lax.broadcasted_iota(I32, (8, ps), 1)
    wi = wi_ref[0] * (H_IDX ** -0.5)

    def dup_scores(s32):
        s32 = jnp.maximum(s32, 0.0) * wi
        sc = s32[0:t]
        for hh in range(1, H_IDX):
            sc = sc + s32[hh * t:(hh + 1) * t]
        return jnp.concatenate([sc] * (8 // t), axis=0)

    @pl.when((b == 0) & (s == 0))
    def _():
        for ref in (kbn_sc, vbn_sc, kan_sc, van_sc, kin_sc):
            ref[...] = jnp.zeros(ref.shape, F32)

    @pl.when(s == 0)
    def _():
        kbn_sc[0:nt8, :] = kbn_ref[0]
        vbn_sc[0:nt8, :] = vbn_ref[0]
        kan_sc[0:nt8, :] = kan_ref[0]
        van_sc[0:nt8, :] = van_ref[0]
        kin_sc[0:nt8, :] = kin_ref[0]
        r_sc[...] = jnp.zeros(r_sc.shape, F32)
        lg = _dot_nt(qb_ref[0], kbn_sc[...].astype(_MM)) + cq_ref[0] - _per_head_rows(cqt_ref[0], t)
        lg = jnp.where(lane_r <= row_tok, lg, NEG_INF)
        m = jnp.max(lg, axis=1, keepdims=True)
        pe = jnp.exp(lg - m)
        m_sc[...] = m
        l_sc[...] = jnp.sum(pe, axis=1, keepdims=True)
        acc_sc[...] = _dot(pe.astype(_MM), vbn_sc[...].astype(_MM))
        key_new = _order_key(dup_scores(_dot_nt(qi_ref[0], kin_sc[...].astype(_MM))))
        key_new = jnp.where(lane8 <= row8_tok, key_new, INT_MIN)
        if gp > 1:
            key_new = jnp.concatenate([key_new, jnp.full((8, w - ps), INT_MIN, I32)], axis=1)
        key_sc[ns] = key_new

    chunk = ns - 1 - s
    lfts = []
    for g in range(gp):
        kb_ref, vb_ref, ka_ref, va_ref, ki_ref, lf_ref = page_refs[6 * g:6 * g + 6]
        lanes = slice(g * ps, (g + 1) * ps)
        kt_sc[:, lanes] = kb_ref[0, 0].reshape(H_B * HEAD_DIM, ps).astype(_MM)
        vt_sc[:, lanes] = vb_ref[0, 0].reshape(H_B * HEAD_DIM, ps).astype(_MM)
        ki_sc[:, lanes] = ki_ref[0, 0].astype(_MM)
        ka_sc[chunk, :, lanes] = ka_ref[0, 0].reshape(KV_A * HEAD_DIM, ps).astype(_MM)
        va_sc[chunk, :, lanes] = va_ref[0, 0].reshape(KV_A * HEAD_DIM, ps).astype(_MM)
        lfts.append(lf_ref[0, 0].astype(F32))

    lf_all = jnp.concatenate(lfts, axis=0)
    suf_loc = _dot_f32(lf_all, _strict_lower(ps))
    carry = r_sc[...]
    sufs = [None] * gp
    for g in reversed(range(gp)):
        loc = suf_loc[g * H_B:(g + 1) * H_B]
        sufs[g] = loc + carry
        carry = carry + loc[:, 0:1] + lfts[g][:, 0:1]
    r_sc[...] = carry
    suf = jnp.concatenate(sufs, axis=1) if gp > 1 else sufs[0]

    lg = _dot(qb_ref[0], kt_sc[...]) + (_per_head_rows(suf, t) + cq_ref[0])
    m_prev = m_sc[...]
    m_new = jnp.maximum(m_prev, jnp.max(lg, axis=1, keepdims=True))
    alpha = jnp.exp(m_prev - m_new)
    pe = jnp.exp(lg - m_new)
    l_sc[...] = alpha * l_sc[...] + jnp.sum(pe, axis=1, keepdims=True)
    acc_sc[...] = alpha * acc_sc[...] + _dot_nt(pe.astype(_MM), vt_sc[...])
    m_sc[...] = m_new
    key_sc[chunk] = _order_key(dup_scores(_dot(qi_ref[0], ki_sc[...])))

    @pl.when(s == ns - 1)
    def _():
        o = acc_sc[...] * (1.0 / l_sc[...])
        lane_h = lax.broadcasted_iota(I32, o.shape, 1) // HEAD_DIM
        row_h = lax.broadcasted_iota(I32, o.shape, 0) // t
        o = jnp.where(lane_h == row_h, o, 0.0)
        ob = o[0:t]
        for hh in range(1, H_B):
            ob = ob + o[hh * t:(hh + 1) * t]
        ob_ref[0] = ob

        nch = ns + 1
        qpos = past + lax.broadcasted_iota(I32, (8, 1), 0) % t
        k_f = jnp.minimum(k_top, qpos + 1).astype(F32)
        load = lambda c: key_sc[c]
        tau, cnt = _kth_largest(load, nch, k_f, bits_per_pass=2)
        has_tie = jnp.max(cnt - k_f) > 0.0

        def store(c, sel):
            sel_sc[c] = jnp.where(sel, 0.0, NEG_INF)

        @pl.when(jnp.logical_not(has_tie))
        def _():
            for c in range(nch):
                store(c, load(c) >= tau)

        @pl.when(has_tie)
        def _():
            _select_with_ties(load, store, nch, w, tau, k_f)

        qa = qa_ref[0]
        lgs = []
        for c in range(nch):
            if c < ns:
                prod, bias = _dot(qa, ka_sc[c]), sel_sc[c]
            else:
                prod, bias = _dot_nt(qa, kan_sc[...].astype(_MM)), sel_sc[c][:, 0:ps]
            lgs.append(prod + jnp.concatenate([bias] * (rows // 8), axis=0))
        m = jnp.max(lgs[0], axis=1, keepdims=True)
        for c in range(1, nch):
            m = jnp.maximum(m, jnp.max(lgs[c], axis=1, keepdims=True))
        lsum = jnp.zeros((rows, 1), F32)
        out = jnp.zeros((rows, KV_A * HEAD_DIM), F32)
        for c in range(nch):
            pe = jnp.exp(lgs[c] - m)
            lsum = lsum + jnp.sum(pe, axis=1, keepdims=True)
            if c < ns:
                out = out + _dot_nt(pe.astype(_MM), va_sc[c])
            else:
                out = out + _dot(pe.astype(_MM), van_sc[...].astype(_MM))
        out = out * (1.0 / lsum)
        lane_j = lax.broadcasted_iota(I32, (t, KV_A * HEAD_DIM), 1) // HEAD_DIM
        for hh in range(H_A):
            piece = out[hh * t:(hh + 1) * t, :]
            oa_ref[0, :, hh * LANES:(hh + 1) * LANES] = jnp.where(lane_j == hh // (H_A // KV_A), piece, 0.0)


def _old2_attn_sample(layer, page_table, caches_t, qbd, qabd, qi32, wi32, cq32, cqt, kbn, vbn, kan, van, kin, k_top, t):
    kb_t, vb_t, ka_t, va_t, ki_t, lf_t = caches_t
    db, npg = page_table.shape
    ps = ki_t.shape[-1]
    rows = H_B * t
    gp = PAGES_PER_STEP if npg % PAGES_PER_STEP == 0 else 1
    ns = npg // gp
    w = gp * ps
    const3 = lambda b, s, pt: (b, 0, 0)
    full = lambda a: BS((1,) + a.shape[1:], const3)

    def page_specs(g):
        def pg(b, s, pt):
            return pt[b, npg - (s + 1) * gp + g]
        five = lambda nh: BS((1, 1, nh, HEAD_DIM, ps), lambda b, s, pt: (layer, pg(b, s, pt), 0, 0, 0))
        return [five(H_B), five(H_B), five(KV_A), five(KV_A),
                BS((1, 1, D_IDX, ps), lambda b, s, pt: (layer, pg(b, s, pt), 0, 0)),
                BS((1, 1, H_B, ps), lambda b, s, pt: (layer, pg(b, s, pt), 0, 0))]

    small = [qbd, qabd, qi32, wi32, cq32, cqt, kbn, vbn, kan, van, kin]
    in_specs = [full(a) for a in small]
    pages = []
    for g in range(gp):
        in_specs += page_specs(g)
        pages += [kb_t, vb_t, ka_t, va_t, ki_t, lf_t]
    da = H_A * LANES
    db_ = H_B * HEAD_DIM
    return pl.pallas_call(
        functools.partial(_attn_s_kernel, past=npg * ps, k_top=k_top, t=t, gp=gp),
        out_shape=[SDS((db, t, da), F32), SDS((db, t, db_), F32)],
        grid_spec=pltpu.PrefetchScalarGridSpec(
            num_scalar_prefetch=1,
            grid=(db, ns),
            in_specs=in_specs,
            out_specs=[BS((1, t, da), const3), BS((1, t, db_), const3)],
            scratch_shapes=[pltpu.VMEM((rows, 1), F32), pltpu.VMEM((rows, 1), F32), pltpu.VMEM((rows, db_), F32),
                            pltpu.VMEM((H_B, 1), F32),
                            pltpu.VMEM((db_, w), _MM), pltpu.VMEM((db_, w), _MM), pltpu.VMEM((D_IDX, w), _MM),
                            pltpu.VMEM((ns, KV_A * HEAD_DIM, w), _MM), pltpu.VMEM((ns, KV_A * HEAD_DIM, w), _MM),
                            pltpu.VMEM((ns + 1, 8, w), I32), pltpu.VMEM((ns + 1, 8, w), F32),
                            pltpu.VMEM((ps, db_), F32), pltpu.VMEM((ps, db_), F32),
                            pltpu.VMEM((ps, KV_A * HEAD_DIM), F32), pltpu.VMEM((ps, KV_A * HEAD_DIM), F32),
                            pltpu.VMEM((ps, D_IDX), F32)]),
        compiler_params=_cp("arbitrary", "arbitrary"),
        name="attn_sample",
    )(page_table, *small, *pages)


def _old_attn_s_kernel(pt_ref, qb_ref, qa_ref, qi_ref, wi_ref, fq_ref, kb_ref, vb_ref, ka_ref, va_ref, ki_ref,
                   fk_ref, kbn_ref, vbn_ref, kan_ref, van_ref, kin_ref, fkn_ref, oa_ref, ob_ref,
                   m_sc, l_sc, acc_sc, ka_sc, va_sc, key_sc, sel_sc, *, past, k_top, t):
    p = pl.program_id(1)
    npg = pl.num_programs(1)
    ps = ki_ref.shape[2]
    rows = H_B * t
    wb = ps * H_B
    wa = ps * KV_A

    @pl.when(p == 0)
    def _():
        m_sc[...] = jnp.full(m_sc.shape, NEG_INF, F32)
        l_sc[...] = jnp.zeros(l_sc.shape, F32)
        acc_sc[...] = jnp.zeros(acc_sc.shape, F32)

    k2 = kb_ref[0, 0].reshape(wb, HEAD_DIM).astype(_MM)
    v2 = vb_ref[0, 0].reshape(wb, HEAD_DIM).astype(_MM)
    lt = _dot_nt(qb_ref[0], k2) + fq_ref[0] - fk_ref[0]
    r_b = lax.broadcasted_iota(I32, (rows, wb), 0)
    c_b = lax.broadcasted_iota(I32, (rows, wb), 1)
    lt = jnp.where((c_b % H_B) == (r_b // t), lt, NEG_INF)
    _online_update(lt, v2, m_sc, l_sc, acc_sc)

    def dup_scores(ki2):
        s = _dot_nt(qi_ref[0], ki2)
        s = jnp.maximum(s, 0.0) * (wi_ref[0] * (H_IDX ** -0.5))
        sc = s[0:t]
        for hh in range(1, H_IDX):
            sc = sc + s[hh * t:(hh + 1) * t]
        return jnp.concatenate([sc] * (8 // t), axis=0)

    r_e = lax.broadcasted_iota(I32, (wa, ps), 0)
    c_e = lax.broadcasted_iota(I32, (wa, ps), 1)
    expand = jnp.where((r_e // KV_A) == c_e, 1.0, 0.0).astype(_MM)
    ki2 = _dot(expand, ki_ref[0, 0].astype(_MM)).astype(_MM)
    key_sc[p] = _order_key(dup_scores(ki2))
    ka_sc[p] = ka_ref[0, 0].reshape(wa, HEAD_DIM).astype(_MM)
    va_sc[p] = va_ref[0, 0].reshape(wa, HEAD_DIM).astype(_MM)

    @pl.when(p == npg - 1)
    def _():
        nn = kbn_ref.shape[1]
        ltn = _dot_nt(qb_ref[0], kbn_ref[0].astype(_MM)) + fq_ref[0] - fkn_ref[0]
        r_n = lax.broadcasted_iota(I32, (rows, nn), 0)
        c_n = lax.broadcasted_iota(I32, (rows, nn), 1)
        ok = ((c_n % H_B) == (r_n // t)) & ((c_n // H_B) <= (r_n % t))
        _online_update(jnp.where(ok, ltn, NEG_INF), vbn_ref[0].astype(_MM), m_sc, l_sc, acc_sc)
        ob_ref[0] = acc_sc[...] * (1.0 / l_sc[...])

        nch = key_sc.shape[0]
        r8 = lax.broadcasted_iota(I32, (8, wa), 0)
        c8 = lax.broadcasted_iota(I32, (8, wa), 1)
        ok_new = (c8 // KV_A) <= (r8 % t)
        key_new = _order_key(dup_scores(kin_ref[0].astype(_MM)))
        key_sc[nch - 1] = jnp.where(ok_new, key_new, INT_MIN)
        ka_sc[nch - 1] = kan_ref[0].astype(_MM)
        va_sc[nch - 1] = van_ref[0].astype(_MM)

        qpos = past + lax.broadcasted_iota(I32, (8, 1), 0) % t
        k_f = (KV_A * jnp.minimum(k_top, qpos + 1)).astype(F32)
        load = lambda c: key_sc[c]
        tau, cnt = _kth_largest(load, nch, k_f)
        has_tie = jnp.max(cnt - k_f) > 0.0

        def store(c, sel):
            sel_sc[c] = jnp.where(sel, 0.0, NEG_INF)

        @pl.when(jnp.logical_not(has_tie))
        def _():
            for c in range(nch):
                store(c, load(c) >= tau)

        @pl.when(has_tie)
        def _():
            _select_with_ties(load, store, nch, wa, tau, k_f)

        m_sc[...] = jnp.full(m_sc.shape, NEG_INF, F32)
        l_sc[...] = jnp.zeros(l_sc.shape, F32)
        acc_sc[...] = jnp.zeros(acc_sc.shape, F32)
        r_a = lax.broadcasted_iota(I32, (rows, wa), 0)
        c_a = lax.broadcasted_iota(I32, (rows, wa), 1)
        grp_ok = (c_a % KV_A) == (r_a // (t * (H_A // KV_A)))
        for c in range(nch):
            bias = jnp.concatenate([sel_sc[c]] * (rows // 8), axis=0)
            lg = _dot_nt(qa_ref[0], ka_sc[c]) + bias
            _online_update(jnp.where(grp_ok, lg, NEG_INF), va_sc[c], m_sc, l_sc, acc_sc)
        oa_ref[0] = acc_sc[...] * (1.0 / l_sc[...])


def _old_attn_sample(layer, page_table, caches, qb32, qa32, qi32, wi32, fq32, fk_row, kbn, vbn, kan, van, kin, fkn_row,
                 k_top, t):
    cache_a_k, cache_a_v, cache_a_idx_k, cache_b_k, cache_b_v = caches
    db, npg = page_table.shape
    ps = cache_a_idx_k.shape[2]
    rows = H_B * t
    wb, wa = ps * H_B, ps * KV_A
    past = npg * ps
    qspec = BS((1, rows, HEAD_DIM), lambda b, p, pt: (b, 0, 0))
    cspec = BS((1, rows, 1), lambda b, p, pt: (b, 0, 0))
    pool5 = lambda nh: BS((1, 1, ps, nh, HEAD_DIM), lambda b, p, pt: (layer, pt[b, p], 0, 0, 0))
    newspec = lambda r: BS((1, r, HEAD_DIM), lambda b, p, pt: (b, 0, 0))
    return pl.pallas_call(
        functools.partial(_attn_s_kernel, past=past, k_top=k_top, t=t),
        out_shape=[SDS((db, rows, HEAD_DIM), F32), SDS((db, rows, HEAD_DIM), F32)],
        grid_spec=pltpu.PrefetchScalarGridSpec(
            num_scalar_prefetch=1,
            grid=(db, npg),
            in_specs=[qspec, qspec, qspec, cspec, cspec,
                      pool5(H_B), pool5(H_B), pool5(KV_A), pool5(KV_A),
                      BS((1, 1, ps, D_IDX), lambda b, p, pt: (layer, pt[b, p], 0, 0)),
                      BS((1, 1, wb), lambda b, p, pt: (b, 0, p)),
                      newspec(rows), newspec(rows), newspec(wa), newspec(wa), newspec(wa),
                      BS((1, 1, rows), lambda b, p, pt: (b, 0, 0))],
            out_specs=[BS((1, rows, HEAD_DIM), lambda b, p, pt: (b, 0, 0)),
                       BS((1, rows, HEAD_DIM), lambda b, p, pt: (b, 0, 0))],
            scratch_shapes=[pltpu.VMEM((rows, 1), F32), pltpu.VMEM((rows, 1), F32),
                            pltpu.VMEM((rows, HEAD_DIM), F32),
                            pltpu.VMEM((npg + 1, wa, HEAD_DIM), _MM), pltpu.VMEM((npg + 1, wa, HEAD_DIM), _MM),
                            pltpu.VMEM((npg + 1, 8, wa), I32), pltpu.VMEM((npg + 1, 8, wa), F32)]),
        compiler_params=_cp("arbitrary", "arbitrary"),
        name="attn_sample",
    )(page_table, qb32, qa32, qi32, wi32, fq32, cache_b_k, cache_b_v, cache_a_k, cache_a_v, cache_a_idx_k,
      fk_row, kbn, vbn, kan, van, kin, fkn_row)


def _row_tile(n, pref):
    tm = min(pref, n)
    assert n % tm == 0, (n, tm)
    return tm


def _even_prompt(grp, layer, x, w, st):
    b, t = grp.nb, grp.t
    tm = _row_tile(t, ROW_TILE)
    outs = _inproj(grp, layer, x, w["norm_mix"], w["w_in_slots"], w["groups_slots"], w["b_f"], st["rope"], tm)
    qa_s, qi_s, qb_s, kb, vb, ka, va, misc = outs
    logf = misc[:, MISC_FB:MISC_FB + H_B]
    f = _cumsum_prompt(logf.reshape(b, t, H_B))
    o_b = _fox_prompt(qb_s, kb, vb, f, jnp.swapaxes(f, 1, 2), b, t)
    o_a = _dsa_prompt(qi_s, qa_s, misc, ka, va, b, t, min(TOPK_MAX, t // 4))
    x = _outproj(grp, layer, o_a.reshape(b * t, -1), o_b.reshape(b * t, -1), w["w_out_a_slots"], w["w_out_b"], x,
                 _row_tile(grp.n, FFN_ROW_TILE))
    return x, (ka, va, misc[:, :D_IDX], kb, vb, logf)


def _even_sample(grp, layer, x, w, st):
    db, t = grp.nb, grp.t
    n = grp.n
    assert 8 % t == 0, t
    outs = _inproj(grp, layer, x, w["norm_mix"], w["w_in_plain"], w["groups_plain"], w["b_f"], st["rope"], n,
                   cum_t=t)
    qa, qi, qb, kb, vb, ka, va, misc = outs
    ki, wi, logf = misc[:, :D_IDX], misc[:, MISC_WI:MISC_WI + H_IDX], misc[:, MISC_FB:MISC_FB + H_B]
    cq = misc[:, MISC_CQ:MISC_CQ + H_B]
    page_table = st["page_table"]
    npg = page_table.shape[1]
    ps = st["caches_t"][0].shape[-1]

    def batch_major(a):
        return a.reshape(t, db, -1).transpose(1, 0, 2)

    def head_major(a, nh):
        return a.reshape(t, db, nh, -1).transpose(1, 2, 0, 3)

    eye_b = jnp.eye(H_B, dtype=qb.dtype)
    qbd = (head_major(qb, H_B)[:, :, :, None, :] * eye_b[None, :, None, :, None]).reshape(db, H_B * t, -1)
    grp_hot = (jnp.arange(H_A)[:, None] // (H_A // KV_A) == jnp.arange(KV_A)[None, :]).astype(qa.dtype)
    qabd = (head_major(qa, H_A)[:, :, :, None, :] * grp_hot[None, :, None, :, None]).reshape(db, H_A * t, -1)
    qi32 = head_major(qi, H_IDX).reshape(db, H_IDX * t, D_IDX)
    wi32 = head_major(wi, H_IDX).reshape(db, H_IDX * t, 1)
    cq32 = head_major(cq, H_B).reshape(db, H_B * t, 1)
    cqt = jnp.pad(batch_major(cq).transpose(0, 2, 1), ((0, 0), (0, 0), (0, ps - t)))
    pad8 = lambda a: jnp.pad(batch_major(a), ((0, 0), (0, 8 - t), (0, 0)))
    o_a, o_b = _attn_sample(st["att_layer"], page_table, st["caches_t"], qbd, qabd, qi32, wi32, cq32, cqt,
                            pad8(kb), pad8(vb), pad8(ka), pad8(va), pad8(ki),
                            min(TOPK_MAX, (npg * ps + t) // 4), t)
    token_major = lambda o: o.transpose(1, 0, 2).reshape(n, -1).astype(_MM)
    x = _outproj(grp, layer, token_major(o_a), token_major(o_b), w["w_out_a_slots"], w["w_out_b"], x,
                 _row_tile(n, FFN_ROW_TILE))
    return x, tuple(batch_major(a).reshape(n, -1) for a in (ka, va, ki, kb, vb, logf))


def _odd_prompt(grp, layer, x, w, st):
    tm = _row_tile(grp.t, ROW_TILE)
    u = _pw1(grp, layer, x, w["norm_mix"], w["w_pw1"], w["b_pw1"], _row_tile(grp.n, ROW_TILE))
    x = _conv_prompt(grp, layer, u, x, w["w_dw"], w["b_dw"], w["ln_g"], w["ln_b"], w["w_pw2"], w["b_pw2"], tm)
    nb, t = grp.nb, grp.t
    state = u.reshape(nb, t, -1)[:, t - (CONV_WIDTH - 1):]
    return x, state


def _odd_sample(grp, layer, x, w, st):
    db, t, d = grp.nb, grp.t, grp.d
    u = _pw1(grp, layer, x, w["norm_mix"], w["w_pw1"], w["b_pw1"], _row_tile(grp.n, ROW_TILE))
    buf = st["state_conv"][st["conv_layer"]]
    u_t = u.reshape(t, db, -1)
    x_t = _conv_sample(jnp.swapaxes(buf, 0, 1), u_t, x.reshape(t, db, d), st["mod_batch"][layer], w["w_dw"],
                       w["b_dw"], w["ln_g"], w["ln_b"], w["w_pw2"], w["b_pw2"])
    state = jnp.concatenate([buf.astype(F32), jnp.swapaxes(u_t, 0, 1)], axis=1)[:, t:]
    return x_t.reshape(db * t, d), state


def _trunk(grp, x, even_fn, odd_fn, layer_w, st, norm_final):
    depth = len(layer_w)
    att, conv = [], []
    for i in range(depth):
        w = layer_w[i]
        if i % 2 == 0:
            x, s = even_fn(grp, i, x, w, dict(st, att_layer=i // 2))
            att.append(s)
        else:
            x, s = odd_fn(grp, i, x, w, dict(st, conv_layer=i // 2))
            conv.append(s)
        tm = _row_tile(grp.n, FFN_ROW_TILE)
        x = _ffn(grp, i, x, w["norm_ffn"], w["w_ffn_in"], w["w_ffn_out"], norm_final, i == depth - 1, tm,
                 w["tf"])
    return x, att, conv


def kernel(x_prompt, x_sample, cache_a_k, cache_a_v, cache_a_idx_k, cache_b_k, cache_b_v, cache_b_logf,
           state_conv, page_table, c_prompt, c_sample, w_in_att, b_fgate, w_out_att, w_pw1, b_pw1, w_dw,
           b_dw, ln_conv_g, ln_conv_b, w_pw2, b_pw2, w_ada, b_ada, norm_mix, norm_ffn, w_ffn_in,
           w_ffn_out, norm_final):
    bp, sp, d = x_prompt.shape
    db, ts, _ = x_sample.shape
    depth = w_ada.shape[0]
    npg, ps = page_table.shape[1], cache_a_idx_k.shape[2]
    past = npg * ps

    mod = _ada(jnp.concatenate([c_prompt, c_sample], axis=0), w_ada, b_ada)
    mod_p = mod[:, :bp].reshape(depth, bp, 1, 6 * d)
    mod_b = mod[:, bp:]
    grp_p = _Group(bp, sp, d, mod_p, batch_major=True)
    grp_s = _Group(db, ts, d, mod_b.reshape(depth, 1, db, 6 * d), batch_major=False)

    ff = w_ffn_out.shape[1]
    tf = ff // 2 if ff % (2 * LANES) == 0 else ff
    half_a = H_A * HEAD_DIM
    layer_w = []
    for i in range(depth):
        w = {"norm_mix": norm_mix[i], "norm_ffn": norm_ffn[i], "w_ffn_in": w_ffn_in[i].astype(_MM),
             "w_ffn_out": w_ffn_out[i].astype(_MM), "tf": tf}
        l = i // 2
        if i % 2 == 0:
            w["w_in_slots"], w["groups_slots"] = _pack_w_in(w_in_att[l], True)
            w["w_in_plain"], w["groups_plain"] = _pack_w_in(w_in_att[l], False)
            w["b_f"] = b_fgate[l]
            wo = w_out_att[l]
            wa = wo[:half_a].reshape(H_A, HEAD_DIM, d)
            zero = jnp.zeros((HEAD_DIM, d), wo.dtype)
            parts = []
            for hh in range(H_A):
                parts += [wa[hh], zero] if hh // (H_A // KV_A) == 0 else [zero, wa[hh]]
            w["w_out_a_slots"] = jnp.concatenate(parts, axis=0).astype(_MM)
            w["w_out_a"] = wo[:half_a].astype(_MM)
            w["w_out_b"] = wo[half_a:].astype(_MM)
        else:
            w.update(w_pw1=w_pw1[l].astype(_MM), b_pw1=b_pw1[l], w_dw=w_dw[l], b_dw=b_dw[l], ln_g=ln_conv_g[l],
                     ln_b=ln_conv_b[l], w_pw2=w_pw2[l].astype(_MM), b_pw2=b_pw2[l])
        layer_w.append(w)

    st_p = {"rope": _rope_tables(jnp.arange(sp, dtype=I32).astype(F32))}
    pos_s = (past + jnp.arange(db * ts, dtype=I32) // db).astype(F32)
    pos_last5 = lambda a: jnp.transpose(a, (0, 1, 3, 4, 2))
    pos_last4 = lambda a: jnp.transpose(a, (0, 1, 3, 2))
    caches_t = (pos_last5(cache_b_k), pos_last5(cache_b_v), pos_last5(cache_a_k), pos_last5(cache_a_v),
                pos_last4(cache_a_idx_k), pos_last4(cache_b_logf))
    st_s = {"rope": _rope_tables(pos_s), "page_table": page_table, "caches_t": caches_t,
            "state_conv": state_conv, "mod_batch": mod_b}

    y_p, att_p, conv_p = _trunk(grp_p, x_prompt.reshape(bp * sp, d), _even_prompt, _odd_prompt, layer_w, st_p,
                                norm_final)
    y_s, att_s, conv_s = _trunk(grp_s, jnp.swapaxes(x_sample, 0, 1).reshape(db * ts, d), _even_sample, _odd_sample,
                                layer_w, st_s, norm_final)
    y_s = jnp.swapaxes(y_s.reshape(ts, db, d), 0, 1)

    def stack_att(att, nb, t):
        shapes = ((KV_A, HEAD_DIM), (KV_A, HEAD_DIM), (D_IDX,), (H_B, HEAD_DIM), (H_B, HEAD_DIM), (H_B,))
        return [jnp.stack([s[j].reshape(nb, t, *shapes[j]) for s in att]) for j in range(6)]

    out_p = stack_att(att_p, bp, sp)
    out_s = stack_att(att_s, db, ts)
    return (y_p.reshape(bp, sp, d), y_s.reshape(db, ts, d), *out_p, jnp.stack(conv_p),
            *out_s, jnp.stack(conv_s))
```

```python
import functools

import jax
import jax.numpy as jnp
from jax import lax
from jax.experimental import pallas as pl
from jax.experimental.pallas import tpu as pltpu

F32 = jnp.float32
I32 = jnp.int32
_MM = jnp.bfloat16

HEAD_DIM = 64
H_A = 8
KV_A = 2
H_IDX = 8
D_IDX = 64
H_B = 8
ROT_DIM = HEAD_DIM // 4
ROPE_THETA = 500000.0
TOPK_MAX = 256
CONV_WIDTH = 31
EPS = 1e-6
LANES = 128
INT_MIN = -(2 ** 31)
NEG_INF = float("-inf")
Q_SCALE = HEAD_DIM ** -0.5
VMEM_LIMIT = 56 * 1024 * 1024

ROW_TILE = 256
FFN_ROW_TILE = 512
FOX_TQ, FOX_TK = 512, 512
DSA_TQ = 256
DSA_CHUNK = 256
COUNT_ROWS = 32
DSA_BANDS = 8
PAGES_PER_STEP = 8
BATCHES_PER_STEP = 2

_SPLITS = (H_A * HEAD_DIM, KV_A * HEAD_DIM, KV_A * HEAD_DIM, H_IDX * D_IDX, D_IDX, H_IDX,
           H_B * HEAD_DIM, H_B * HEAD_DIM, H_B * HEAD_DIM, H_B)
_NAMES = ("qa", "ka", "va", "qi", "ki", "wi", "qb", "kb", "vb", "fb")
_OFF = {}
_o = 0
for _n, _w in zip(_NAMES, _SPLITS):
    _OFF[_n] = (_o, _w)
    _o += _w
MISC_WI = D_IDX
MISC_FB = D_IDX + H_IDX
MISC_CQ = MISC_FB + H_B

SDS = jax.ShapeDtypeStruct
BS = pl.BlockSpec


def _cp(*sem):
    return pltpu.CompilerParams(dimension_semantics=sem, vmem_limit_bytes=VMEM_LIMIT)


def _dot(a, b):
    return jnp.dot(a, b, preferred_element_type=F32)


def _dot_nt(a, b):
    return lax.dot_general(a, b, (((1,), (1,)), ((), ())), preferred_element_type=F32)


def _sigmoid(x):
    return 1.0 / (1.0 + jnp.exp(-x))


def _silu(x):
    return x * _sigmoid(x)


def _rms(x, g):
    return x * lax.rsqrt(jnp.mean(x * x, axis=-1, keepdims=True) + EPS) * g


def _modulate(x, g, shift, scale):
    return _rms(x, g) * (1.0 + scale) + shift


def _ada_kernel(c_ref, w_ref, b_ref, o_ref):
    a = _silu(c_ref[...]).astype(_MM)
    o_ref[0] = _dot(a, w_ref[0].astype(_MM)) + b_ref[0]


def _ada(c_all, w_ada, b_ada):
    depth, d, d6 = w_ada.shape
    r = c_all.shape[0]
    tn = d6 // 4
    return pl.pallas_call(
        _ada_kernel,
        out_shape=SDS((depth, r, d6), F32),
        grid=(depth, d6 // tn),
        in_specs=[BS((r, d), lambda l, j: (0, 0)),
                  BS((1, d, tn), lambda l, j: (l, 0, j)),
                  BS((1, 1, tn), lambda l, j: (l, 0, j))],
        out_specs=BS((1, r, tn), lambda l, j: (l, 0, j)),
        compiler_params=_cp("arbitrary", "arbitrary"),
        name="ada_mod",
    )(c_all, w_ada, b_ada.reshape(depth, 1, d6))


class _Group:
    def __init__(self, nb, t, d, mod, batch_major):
        self.nb, self.t, self.d = nb, t, d
        self.n = nb * t
        self.mod = mod
        self.batch_major = batch_major

    def mod_arg(self, layer, tm, chunk):
        d = self.d
        if self.batch_major:
            tpb = self.t // tm
            return self.mod[layer], BS((1, 1, d), lambda i, *_: (i // tpb, 0, chunk))
        assert tm % self.nb == 0, (tm, self.nb)
        return self.mod[layer], BS((1, self.nb, d), lambda i, *_: (0, 0, chunk))


def _rows(v, n):
    r = v.shape[0]
    return v if r in (1, n) else jnp.concatenate([v] * (n // r), axis=0)


def _rope_tables(pos):
    half = ROT_DIM // 2
    inv = ROPE_THETA ** (-jnp.arange(half, dtype=F32) * 2.0 / ROT_DIM)
    ang = pos[:, None] * inv[None, :]
    cos, sin = jnp.cos(ang), jnp.sin(ang)
    n = pos.shape[0]
    one = jnp.ones((n, HEAD_DIM - ROT_DIM), F32)
    zero = jnp.zeros((n, HEAD_DIM - ROT_DIM), F32)
    z8 = jnp.zeros((n, half), F32)
    c = jnp.concatenate([cos, cos, one], axis=1)
    s1 = jnp.concatenate([-sin, z8, zero], axis=1)
    s2 = jnp.concatenate([z8, sin, zero], axis=1)
    rep = LANES // HEAD_DIM
    return jnp.tile(c, (1, rep)), jnp.tile(s1, (1, rep)), jnp.tile(s2, (1, rep))


def _rope(y, c, s1, s2):
    w = y.shape[1]
    rep = w // LANES
    if rep > 1:
        c, s1, s2 = (jnp.concatenate([t] * rep, axis=1) for t in (c, s1, s2))
    half = ROT_DIM // 2
    return y * c + pltpu.roll(y, w - half, 1) * s1 + pltpu.roll(y, half, 1) * s2


def _log_sigmoid(x):
    return jnp.minimum(x, 0.0) - jnp.log(1.0 + jnp.exp(-jnp.abs(x)))


STATE_GROUPS = ("kb", "vb", "ka", "va", "misc")


def _inproj_kernel(x_ref, g_ref, sh_ref, sc_ref, w_ref, c_ref, s1_ref, s2_ref, bf_ref, *refs, groups, cum_t,
                   n_prev):
    out_refs = refs[n_prev:]
    x = x_ref[...]
    n_rows = x.shape[0]
    h = _modulate(x, g_ref[...], _rows(sh_ref[0], n_rows), _rows(sc_ref[0], n_rows)).astype(_MM)
    c, s1, s2 = c_ref[...], s1_ref[...], s2_ref[...]
    off = 0
    for (name, width, rope, scale), o_ref in zip(groups, out_refs):
        y = _dot(h, w_ref[:, off:off + width])
        off += width
        if name == "misc":
            lane = lax.broadcasted_iota(I32, y.shape, 1)
            yr = _rope(y, c, s1, s2)
            lf = _log_sigmoid(y + bf_ref[...])
            in_fb = (lane >= MISC_FB) & (lane < MISC_FB + H_B)
            if cum_t:
                stride = n_rows // cum_t
                tok = lax.broadcasted_iota(I32, y.shape, 0) // stride
                cum = lf
                for k in range(1, cum_t):
                    cum = cum + jnp.where(tok >= k, pltpu.roll(lf, k * stride, 0), 0.0)
                y = jnp.where((lane >= MISC_CQ) & (lane < MISC_CQ + H_B), pltpu.roll(cum, H_B, 1), y)
            y = jnp.where(lane < MISC_WI, yr, jnp.where(in_fb, lf, y))
        elif rope:
            y = _rope(y, c, s1, s2)
        if scale != 1.0:
            y = y * scale
        if name in STATE_GROUPS:
            o_ref[0] = y
        else:
            o_ref[...] = y.astype(o_ref.dtype)


def _pack_w_in(w_in, slots):
    d = w_in.shape[0]

    def cols(name):
        o, w = _OFF[name]
        return w_in[:, o:o + w]

    def slot(name, place):
        src = cols(name).reshape(d, -1, HEAD_DIM)
        zero = jnp.zeros((d, HEAD_DIM), w_in.dtype)
        parts = []
        for hh in range(src.shape[1]):
            parts += [src[:, hh], zero] if place(hh) == 0 else [zero, src[:, hh]]
        return jnp.concatenate(parts, axis=1)

    misc = jnp.concatenate([cols("ki"), cols("wi"), cols("fb"),
                            jnp.zeros((d, LANES - D_IDX - H_IDX - H_B), w_in.dtype)], axis=1)
    if slots:
        qa = slot("qa", lambda hh: hh // (H_A // KV_A))
        qi = slot("qi", lambda hh: 0)
        qb = slot("qb", lambda hh: hh % 2)
    else:
        qa, qi, qb = cols("qa"), cols("qi"), cols("qb")
    parts = [qa, qi, qb, cols("kb"), cols("vb"), cols("ka"), cols("va"), misc]
    groups = (("qa", qa.shape[1], True, Q_SCALE), ("qi", qi.shape[1], True, D_IDX ** -0.5),
              ("qb", qb.shape[1], False, Q_SCALE), ("kb", H_B * HEAD_DIM, False, 1.0),
              ("vb", H_B * HEAD_DIM, False, 1.0), ("ka", KV_A * HEAD_DIM, True, 1.0),
              ("va", KV_A * HEAD_DIM, False, 1.0), ("misc", LANES, False, 1.0))
    return jnp.concatenate(parts, axis=1).astype(_MM), groups


def _inproj(grp, layer, x, norm_g, w_packed, groups, b_f, tables, tm, att, n_att, prev, cum_t=0):
    n, d = x.shape
    nt = n // tm
    sh_arr, sh_spec = grp.mod_arg(layer, tm, 0)
    sc_arr, sc_spec = grp.mod_arg(layer, tm, 1)
    c, s1, s2 = tables
    tr = c.shape[0] // tm
    tspec = BS((tm, LANES), lambda i: (i % tr, 0))
    bf = jnp.zeros((1, LANES), F32).at[0, MISC_FB:MISC_FB + H_B].set(b_f)
    out_shape, out_specs = [], []
    for name, w, _, _ in groups:
        if name in STATE_GROUPS:
            out_shape.append(SDS((n_att, n, w), F32))
            out_specs.append(BS((1, tm, w), lambda i: (att, i, 0)))
        else:
            out_shape.append(SDS((n, w), _MM))
            out_specs.append(BS((tm, w), lambda i: (i, 0)))
    nc = w_packed.shape[1]
    in_specs = [BS((tm, d), lambda i: (i, 0)), BS((1, d), lambda i: (0, 0)), sh_spec, sc_spec,
                BS((d, nc), lambda i: (0, 0)), tspec, tspec, tspec, BS((1, LANES), lambda i: (0, 0))]
    args = [x, norm_g.reshape(1, d), sh_arr, sc_arr, w_packed, c, s1, s2, bf]
    aliases = {}
    if prev is not None:
        state_out = [k for k, g in enumerate(groups) if g[0] in STATE_GROUPS]
        for arr, k in zip(prev, state_out):
            aliases[len(args)] = k
            in_specs.append(BS(memory_space=pl.ANY))
            args.append(arr)
    return pl.pallas_call(
        functools.partial(_inproj_kernel, groups=groups, cum_t=cum_t, n_prev=len(aliases)),
        out_shape=out_shape,
        grid=(nt,),
        in_specs=in_specs,
        out_specs=out_specs,
        input_output_aliases=aliases,
        compiler_params=_cp("arbitrary"),
        name="even_inproj",
    )(*args)


def _tri_lower(n):
    r = lax.broadcasted_iota(I32, (n, n), 0)
    c = lax.broadcasted_iota(I32, (n, n), 1)
    return jnp.where(r >= c, 1.0, 0.0).astype(F32)


def _dot_f32(a, b):
    return jnp.dot(a, b, preferred_element_type=F32, precision=lax.Precision.HIGHEST)


def _cumsum_p_kernel(lf_ref, f_ref, *, tc):
    t = lf_ref.shape[1]
    tri = _tri_lower(tc)
    carry = jnp.zeros((1, lf_ref.shape[2]), F32)
    for c in range(t // tc):
        fc = _dot_f32(tri, lf_ref[0, c * tc:(c + 1) * tc, :]) + carry
        f_ref[0, c * tc:(c + 1) * tc, :] = fc
        carry = fc[tc - 1:tc, :]


def _cumsum_prompt(logf):
    b, t, hb = logf.shape
    tc = min(256, t)
    return pl.pallas_call(
        functools.partial(_cumsum_p_kernel, tc=tc),
        out_shape=SDS((b, t, hb), F32),
        grid=(b,),
        in_specs=[BS((1, t, hb), lambda i: (i, 0, 0))],
        out_specs=BS((1, t, hb), lambda i: (i, 0, 0)),
        compiler_params=_cp("arbitrary"),
        name="cumsum_prompt",
    )(logf)


def _fox_p_kernel(q_ref, k_ref, v_ref, fq_ref, fk_ref, o_ref, m_sc, l_sc, acc_sc, fq_sc, *, tq, tk):
    i = pl.program_id(1)
    j = pl.program_id(2)
    nk = pl.num_programs(2)

    @pl.when(j == 0)
    def _():
        m_sc[...] = jnp.full(m_sc.shape, NEG_INF, F32)
        l_sc[...] = jnp.zeros(l_sc.shape, F32)
        acc_sc[...] = jnp.zeros(acc_sc.shape, F32)
        fq = fq_ref[0]
        for hh in range(H_B):
            fq_sc[hh] = jnp.broadcast_to(fq[:, hh:hh + 1], (tq, LANES))

    def step(masked):
        k = k_ref[0, 0].astype(_MM)
        v = v_ref[0, 0].astype(_MM)
        fk = fk_ref[0]
        if masked:
            rows = i * tq + lax.broadcasted_iota(I32, (tq, tk), 0)
            cols = j * tk + lax.broadcasted_iota(I32, (tq, tk), 1)
            causal = cols <= rows
        lane = lax.broadcasted_iota(I32, (tq, LANES), 1)
        low = lane < HEAD_DIM
        for p in range(H_B // 2):
            kp = k[:, p * LANES:(p + 1) * LANES]
            vp = v[:, p * LANES:(p + 1) * LANES]
            alphas, pvs = [], []
            for e in range(2):
                hh = 2 * p + e
                z = _dot_nt(q_ref[0, :, hh * LANES:(hh + 1) * LANES], kp) - fk[hh:hh + 1, :]
                if masked:
                    z = jnp.where(causal, z, NEG_INF)
                fq = fq_sc[hh]
                m_prev = m_sc[hh]
                m_new = jnp.maximum(m_prev, fq + jnp.max(z, axis=1, keepdims=True))
                alpha = jnp.exp(m_prev - m_new)
                pe = jnp.exp(z + jnp.concatenate([fq - m_new] * (tk // LANES), axis=1))
                l_sc[hh] = alpha * l_sc[hh] + jnp.sum(pe, axis=1, keepdims=True)
                m_sc[hh] = m_new
                alphas.append(alpha)
                pvs.append(_dot(pe.astype(_MM), vp))
            acc_sc[p] = jnp.where(low, alphas[0], alphas[1]) * acc_sc[p] + jnp.where(low, pvs[0], pvs[1])

    visible = (j + 1) * tk <= i * tq + 1

    @pl.when(visible)
    def _():
        step(False)

    @pl.when(jnp.logical_not(visible) & (j * tk < (i + 1) * tq))
    def _():
        step(True)

    @pl.when(j == nk - 1)
    def _():
        lane = lax.broadcasted_iota(I32, (tq, LANES), 1)
        low = lane < HEAD_DIM
        for p in range(H_B // 2):
            linv = jnp.where(low, 1.0 / l_sc[2 * p], 1.0 / l_sc[2 * p + 1])
            o_ref[0, :, p * LANES:(p + 1) * LANES] = (acc_sc[p] * linv).astype(o_ref.dtype)


def _fox_prompt(qb_s, kb, vb, f, ft, b, t, att):
    tq, tk = min(FOX_TQ, t), min(FOX_TK, t)
    nq, nk = t // tq, t // tk
    dq = qb_s.shape[1]
    n_att, _, dk = kb.shape
    kmap = lambda bb, i, j: (att, bb, jnp.minimum(j, ((i + 1) * tq - 1) // tk), 0)
    return pl.pallas_call(
        functools.partial(_fox_p_kernel, tq=tq, tk=tk),
        out_shape=SDS((b, t, dk), _MM),
        grid=(b, nq, nk),
        in_specs=[BS((1, tq, dq), lambda bb, i, j: (bb, i, 0)),
                  BS((1, 1, tk, dk), kmap), BS((1, 1, tk, dk), kmap),
                  BS((1, tq, H_B), lambda bb, i, j: (bb, i, 0)),
                  BS((1, H_B, tk), lambda bb, i, j: (bb, 0, jnp.minimum(j, ((i + 1) * tq - 1) // tk)))],
        out_specs=BS((1, tq, dk), lambda bb, i, j: (bb, i, 0)),
        scratch_shapes=[pltpu.VMEM((H_B, tq, LANES), F32), pltpu.VMEM((H_B, tq, LANES), F32),
                        pltpu.VMEM((H_B // 2, tq, LANES), F32), pltpu.VMEM((H_B, tq, LANES), F32)],
        compiler_params=_cp("arbitrary", "arbitrary", "arbitrary"),
        name="fox_prompt",
    )(qb_s.reshape(b, t, dq), kb.reshape(n_att, b, t, dk), vb.reshape(n_att, b, t, dk), f, ft)


def _order_key(score):
    bits = pltpu.bitcast(score + 0.0, I32)
    return jnp.where(bits < 0, bits ^ jnp.int32(0x7FFFFFFF), bits)


def _kth_largest(load, nchunks, k_f, bits_per_pass=1):
    rows, width = load(0).shape
    rc = min(rows, COUNT_ROWS)

    def count_ge(cand):
        accs = []
        for r0 in range(0, rows, rc):
            acc = None
            for c in range(nchunks):
                x = jnp.where(load(c, r0, rc) >= cand[r0:r0 + rc], 1.0, 0.0)
                for j in range(width // LANES):
                    piece = x[:, j * LANES:(j + 1) * LANES]
                    acc = piece if acc is None else acc + piece
            accs.append(acc)
        folded = accs[0] if len(accs) == 1 else jnp.concatenate(accs, axis=0)
        return jnp.sum(folded, axis=1, keepdims=True)

    def body(it, tau):
        shift = 32 - bits_per_pass * (it + 1)
        best = tau
        for digit in range(1, 2 ** bits_per_pass):
            cand = tau ^ jnp.left_shift(jnp.int32(digit), shift)
            best = jnp.where(count_ge(cand) >= k_f, cand, best)
        return best

    tau = lax.fori_loop(0, 32 // bits_per_pass, body, jnp.full((rows, 1), INT_MIN, I32))
    return tau, count_ge(tau)


def _select_with_ties(load, store, nchunks, width, tau, k_f):
    n_gt = None
    for c in range(nchunks):
        x = jnp.sum(jnp.where(load(c) > tau, 1.0, 0.0), axis=1, keepdims=True)
        n_gt = x if n_gt is None else n_gt + x
    need = k_f - n_gt
    r = lax.broadcasted_iota(I32, (width, width), 0)
    cc = lax.broadcasted_iota(I32, (width, width), 1)
    upper = jnp.where(r <= cc, 1.0, 0.0).astype(_MM)
    carry = jnp.zeros_like(need)
    for c in range(nchunks):
        key = load(c)
        eq = key == tau
        prefix = _dot(jnp.where(eq, 1.0, 0.0).astype(_MM), upper) + carry
        store(c, (key > tau) | (eq & (prefix <= need)))
        carry = prefix[:, width - 1:width]


def _dsa_p_kernel(qi_ref, qa_ref, mq_ref, mk_ref, ka_ref, va_ref, *refs, tq, lk, q0, k_top, cw, n_prev):
    o_ref, key_sc, bias_sc = refs[n_prev:]
    i = pl.program_id(1)
    nch = lk // cw
    qpos = q0 + i * tq + lax.broadcasted_iota(I32, (tq, 1), 0)
    wi = mq_ref[0, 0][:, MISC_WI:MISC_WI + H_IDX] * (H_IDX ** -0.5)
    for c in range(nch):
        kmat = mk_ref[0, 0, c * cw:(c + 1) * cw, :].astype(_MM)
        score = jnp.zeros((tq, cw), F32)
        for hh in range(H_IDX):
            s = _dot_nt(qi_ref[0, :, hh * LANES:(hh + 1) * LANES], kmat)
            score = score + jnp.maximum(s, 0.0) * wi[:, hh:hh + 1]
        kpos = c * cw + lax.broadcasted_iota(I32, (tq, cw), 1)
        key_sc[:, c * cw:(c + 1) * cw] = jnp.where(kpos <= qpos, _order_key(score), INT_MIN)

    load = lambda c, r0=0, nr=tq: key_sc[r0:r0 + nr, c * cw:(c + 1) * cw]
    k_f = jnp.minimum(k_top, qpos + 1).astype(F32)
    tau, cnt = _kth_largest(load, nch, k_f)
    has_tie = jnp.max(cnt - k_f) > 0.0

    def store(c, sel):
        bias_sc[:, c * cw:(c + 1) * cw] = jnp.where(sel, 0.0, NEG_INF)

    @pl.when(jnp.logical_not(has_tie))
    def _():
        for c in range(nch):
            store(c, load(c) >= tau)

    @pl.when(has_tie)
    def _():
        _select_with_ties(load, store, nch, cw, tau, k_f)

    ka = ka_ref[0, 0].astype(_MM)
    va = va_ref[0, 0].astype(_MM)
    bias = bias_sc[...]
    lane = lax.broadcasted_iota(I32, (tq, LANES), 1)
    for hh in range(H_A):
        grp = hh // (H_A // KV_A)
        lg = _dot_nt(qa_ref[0, :, hh * LANES:(hh + 1) * LANES], ka) + bias
        m = jnp.max(lg, axis=1, keepdims=True)
        pe = jnp.exp(lg - m)
        l = jnp.sum(pe, axis=1, keepdims=True)
        o = _dot(pe.astype(_MM), va) * (1.0 / l)
        o = jnp.where((lane >= grp * HEAD_DIM) & (lane < (grp + 1) * HEAD_DIM), o, 0.0)
        o_ref[0, :, hh * LANES:(hh + 1) * LANES] = o.astype(o_ref.dtype)


def _dsa_prompt(qi_s, qa_s, misc, ka, va, b, t, k_top, att):
    bands = min(DSA_BANDS, t // LANES)
    band = t // bands
    tq = min(DSA_TQ, band)
    cw = min(DSA_CHUNK, band)
    dq = qi_s.shape[1]
    n_att = misc.shape[0]
    qi3, qa3 = qi_s.reshape(b, t, dq), qa_s.reshape(b, t, dq)
    misc4, ka4, va4 = (a.reshape(n_att, b, t, LANES) for a in (misc, ka, va))
    kmap = lambda bb, i: (att, bb, 0, 0)
    out = None
    for c in range(bands):
        lk = (c + 1) * band
        q_first = c * (band // tq)
        qmap = lambda bb, i, q_first=q_first: (bb, q_first + i, 0)
        qmap4 = lambda bb, i, q_first=q_first: (att, bb, q_first + i, 0)
        in_specs = [BS((1, tq, dq), qmap), BS((1, tq, dq), qmap), BS((1, 1, tq, LANES), qmap4),
                    BS((1, 1, lk, LANES), kmap), BS((1, 1, lk, LANES), kmap), BS((1, 1, lk, LANES), kmap)]
        args = [qi3, qa3, misc4, misc4, ka4, va4]
        aliases = {}
        if out is not None:
            aliases = {len(args): 0}
            in_specs.append(BS(memory_space=pl.ANY))
            args.append(out)
        out = pl.pallas_call(
            functools.partial(_dsa_p_kernel, tq=tq, lk=lk, q0=c * band, k_top=k_top, cw=cw, n_prev=len(aliases)),
            out_shape=SDS((b, t, dq), _MM),
            grid=(b, band // tq),
            in_specs=in_specs,
            out_specs=BS((1, tq, dq), qmap),
            scratch_shapes=[pltpu.VMEM((tq, lk), I32), pltpu.VMEM((tq, lk), F32)],
            input_output_aliases=aliases,
            compiler_params=_cp("arbitrary", "arbitrary"),
            name="dsa_prompt",
        )(*args)
    return out


def _outproj_kernel(oa_ref, ob_ref, wa_ref, wb_ref, x_ref, gate_ref, o_ref):
    y = _dot(oa_ref[...], wa_ref[...]) + _dot(ob_ref[...], wb_ref[...])
    o_ref[...] = x_ref[...] + _rows(gate_ref[0], y.shape[0]) * y


def _outproj(grp, layer, oa, ob, wa, wb, x, tm):
    n, d = x.shape
    g_arr, g_spec = grp.mod_arg(layer, tm, 2)
    da, db = oa.shape[1], ob.shape[1]
    return pl.pallas_call(
        _outproj_kernel,
        out_shape=SDS((n, d), F32),
        grid=(n // tm,),
        in_specs=[BS((tm, da), lambda i: (i, 0)), BS((tm, db), lambda i: (i, 0)),
                  BS((da, d), lambda i: (0, 0)), BS((db, d), lambda i: (0, 0)),
                  BS((tm, d), lambda i: (i, 0)), g_spec],
        out_specs=BS((tm, d), lambda i: (i, 0)),
        compiler_params=_cp("arbitrary"),
        name="even_outproj",
    )(oa, ob, wa, wb, x, g_arr)


def _ffn_kernel(x_ref, g_ref, sh_ref, sc_ref, gate_ref, wi_ref, wo_ref, gf_ref, o_ref, *, final, tf):
    ff = wo_ref.shape[1]
    x = x_ref[...]
    n_rows = x.shape[0]
    h = _modulate(x, g_ref[...], _rows(sh_ref[0], n_rows), _rows(sc_ref[0], n_rows)).astype(_MM)
    acc = None
    for c in range(ff // tf):
        gate_part = _dot(h, wi_ref[0, :, c * tf:(c + 1) * tf])
        up_part = _dot(h, wi_ref[0, :, ff + c * tf:ff + (c + 1) * tf])
        a = (_silu(gate_part) * up_part).astype(_MM)
        part = _dot(a, wo_ref[0, c * tf:(c + 1) * tf, :])
        acc = part if acc is None else acc + part
    y = x + _rows(gate_ref[0], n_rows) * acc
    if final:
        y = _rms(y, gf_ref[...])
    o_ref[...] = y


def _ffn(grp, layer, x, norm_g, w_in, w_out, norm_final, final, tm, tf):
    n, d = x.shape
    ff = w_out.shape[1]
    sh_arr, sh_spec = grp.mod_arg(layer, tm, 3)
    sc_arr, sc_spec = grp.mod_arg(layer, tm, 4)
    g_arr, g_spec = grp.mod_arg(layer, tm, 5)
    resident = lambda shape: BS((1,) + shape, lambda i: (layer, 0, 0), pipeline_mode=pl.Buffered(1))
    return pl.pallas_call(
        functools.partial(_ffn_kernel, final=final, tf=tf),
        out_shape=SDS((n, d), F32),
        grid=(n // tm,),
        in_specs=[BS((tm, d), lambda i: (i, 0)), BS((1, d), lambda i: (0, 0)), sh_spec, sc_spec, g_spec,
                  resident((d, 2 * ff)), resident((ff, d)), BS((1, d), lambda i: (0, 0))],
        out_specs=BS((tm, d), lambda i: (i, 0)),
        compiler_params=_cp("arbitrary"),
        name="ffn",
    )(x, norm_g.reshape(1, d), sh_arr, sc_arr, g_arr, w_in, w_out, norm_final.reshape(1, d))


def _pw1_kernel(x_ref, g_ref, sh_ref, sc_ref, w_ref, b_ref, u_ref):
    x = x_ref[...]
    n_rows = x.shape[0]
    h = _modulate(x, g_ref[...], _rows(sh_ref[0], n_rows), _rows(sc_ref[0], n_rows)).astype(_MM)
    y = _dot(h, w_ref[...]) + b_ref[...]
    dc = y.shape[1] // 2
    u_ref[...] = y[:, :dc] * _sigmoid(y[:, dc:])


def _pw1(grp, layer, x, norm_g, w, bias, tm):
    n, d = x.shape
    dc2 = w.shape[1]
    sh_arr, sh_spec = grp.mod_arg(layer, tm, 0)
    sc_arr, sc_spec = grp.mod_arg(layer, tm, 1)
    return pl.pallas_call(
        _pw1_kernel,
        out_shape=SDS((n, dc2 // 2), F32),
        grid=(n // tm,),
        in_specs=[BS((tm, d), lambda i: (i, 0)), BS((1, d), lambda i: (0, 0)), sh_spec, sc_spec,
                  BS((d, dc2), lambda i: (0, 0)), BS((1, dc2), lambda i: (0, 0))],
        out_specs=BS((tm, dc2 // 2), lambda i: (i, 0)),
        compiler_params=_cp("arbitrary"),
        name="conv_pw1_glu",
    )(x, norm_g.reshape(1, d), sh_arr, sc_arr, w, bias.reshape(1, dc2))


def _ln_swish_pw2(z, lng, lnb, w2, b2):
    mu = jnp.mean(z, axis=-1, keepdims=True)
    zc = z - mu
    var = jnp.mean(zc * zc, axis=-1, keepdims=True)
    zn = zc * lax.rsqrt(var + EPS) * lng + lnb
    return _dot(_silu(zn).astype(_MM), w2) + b2


HALO = 32


SUBLANES = 8
CONV_ROWS = 32


def _conv_p_kernel(ucur_ref, uhalo_ref, wdw_ref, bdw_ref, lng_ref, lnb_ref, w2_ref, b2_ref, x_ref, gate_ref,
                   o_ref, full_sc, shift_sc, z_sc, *, tm):
    i = pl.program_id(1)
    dc = full_sc.shape[1]
    full_sc[0:HALO, :] = jnp.where(i > 0, uhalo_ref[0], 0.0)
    full_sc[HALO:HALO + tm, :] = ucur_ref[0]
    span = shift_sc.shape[1]
    for q in range(1, SUBLANES):
        shift_sc[q - 1] = full_sc[q:q + span, :]
    base = HALO - (CONV_WIDTH - 1)

    def chunk(c, carry):
        r0 = pl.multiple_of(c * CONV_ROWS, CONV_ROWS)
        groups = CONV_ROWS // SUBLANES
        z = jnp.zeros((groups, SUBLANES, dc), F32) + bdw_ref[...]
        for w in range(CONV_WIDTH):
            q, al = (base + w) % SUBLANES, ((base + w) // SUBLANES) * SUBLANES
            src = full_sc if q == 0 else shift_sc.at[q - 1]
            slab = src[pl.ds(r0 + al, CONV_ROWS), :].reshape(groups, SUBLANES, dc)
            z = z + slab * wdw_ref[w][None]
        z_sc[pl.ds(r0, CONV_ROWS), :] = z.reshape(CONV_ROWS, dc)
        return carry

    lax.fori_loop(0, tm // CONV_ROWS, chunk, 0)
    y = _ln_swish_pw2(z_sc[...], lng_ref[...], lnb_ref[...], w2_ref[...], b2_ref[...])
    o_ref[0] = x_ref[0] + gate_ref[0] * y


def _conv_prompt(grp, layer, u, x, wdw, bdw, lng, lnb, w2, b2, tm):
    b, t, d = grp.nb, grp.t, grp.d
    dc = u.shape[1]
    g_arr, _ = grp.mod_arg(layer, tm, 2)
    hb = tm // HALO
    wrep = jnp.broadcast_to(wdw[:, None, :], (CONV_WIDTH, SUBLANES, dc))
    vec = lambda bb, i: (0, 0)
    return pl.pallas_call(
        functools.partial(_conv_p_kernel, tm=tm),
        out_shape=SDS((b, t, d), F32),
        grid=(b, t // tm),
        in_specs=[BS((1, tm, dc), lambda bb, i: (bb, i, 0)),
                  BS((1, HALO, dc), lambda bb, i: (bb, jnp.maximum(i * hb - 1, 0), 0)),
                  BS((CONV_WIDTH, SUBLANES, dc), lambda bb, i: (0, 0, 0)),
                  BS((1, dc), vec), BS((1, dc), vec), BS((1, dc), vec),
                  BS((dc, d), vec), BS((1, d), vec),
                  BS((1, tm, d), lambda bb, i: (bb, i, 0)),
                  BS((1, 1, d), lambda bb, i: (bb, 0, 2))],
        out_specs=BS((1, tm, d), lambda bb, i: (bb, i, 0)),
        scratch_shapes=[pltpu.VMEM((HALO + tm, dc), F32),
                        pltpu.VMEM((SUBLANES - 1, HALO + tm - SUBLANES, dc), F32),
                        pltpu.VMEM((tm, dc), F32)],
        compiler_params=_cp("arbitrary", "arbitrary"),
        name="conv_prompt",
    )(u.reshape(b, t, dc), u.reshape(b, t, dc), wrep, bdw.reshape(1, dc), lng.reshape(1, dc), lnb.reshape(1, dc),
      w2, b2.reshape(1, d), x.reshape(b, t, d), g_arr).reshape(b * t, d)


def _conv_s_kernel(buf_ref, u_ref, wdw_ref, bdw_ref, lng_ref, lnb_ref, w2_ref, b2_ref, x_ref, gate_ref, o_ref):
    nbuf = buf_ref.shape[0]
    t = u_ref.shape[0]
    zs = []
    for tt in range(t):
        z = jnp.zeros(u_ref.shape[1:], F32) + bdw_ref[...]
        for w in range(CONV_WIDTH):
            src = tt + w
            row = buf_ref[src] if src < nbuf else u_ref[src - nbuf]
            z = z + row * wdw_ref[w:w + 1, :]
        zs.append(z)
    y = _ln_swish_pw2(jnp.concatenate(zs, axis=0), lng_ref[...], lnb_ref[...], w2_ref[...], b2_ref[...])
    bb = u_ref.shape[1]
    for tt in range(t):
        o_ref[tt] = x_ref[tt] + gate_ref[...] * y[tt * bb:(tt + 1) * bb, :]


def _conv_sample(buf_t, u_t, x_t, gate, wdw, bdw, lng, lnb, w2, b2):
    nbuf, db, dc = buf_t.shape
    t, _, d = x_t.shape
    bb = min(32, db)
    wpad = jnp.zeros((HALO, dc), F32).at[:CONV_WIDTH].set(wdw)
    vec = lambda j: (0, 0)
    return pl.pallas_call(
        _conv_s_kernel,
        out_shape=SDS((t, db, d), F32),
        grid=(db // bb,),
        in_specs=[BS((nbuf, bb, dc), lambda j: (0, j, 0)), BS((t, bb, dc), lambda j: (0, j, 0)),
                  BS((HALO, dc), vec), BS((1, dc), vec), BS((1, dc), vec), BS((1, dc), vec),
                  BS((dc, d), vec), BS((1, d), vec),
                  BS((t, bb, d), lambda j: (0, j, 0)), BS((bb, d), lambda j: (j, 2))],
        out_specs=BS((t, bb, d), lambda j: (0, j, 0)),
        compiler_params=_cp("arbitrary"),
        name="conv_sample",
    )(buf_t, u_t, wpad, bdw.reshape(1, dc), lng.reshape(1, dc), lnb.reshape(1, dc), w2, b2.reshape(1, d), x_t, gate)


def _per_head_rows(x, t):
    nh, w = x.shape
    row = lax.broadcasted_iota(I32, (nh * t, w), 0) // t
    out = jnp.zeros((nh * t, w), x.dtype)
    for hh in range(nh):
        out = jnp.where(row == hh, x[hh:hh + 1, :], out)
    return out


def _strict_lower(n):
    r = lax.broadcasted_iota(I32, (n, n), 0)
    c = lax.broadcasted_iota(I32, (n, n), 1)
    return jnp.where(r > c, 1.0, 0.0).astype(F32)


def _attn_s_kernel(pt_ref, qb_ref, qa_ref, qi_ref, wi_ref, cq_ref, cqt_ref, kbn_ref, vbn_ref, kan_ref, van_ref,
                   kin_ref, *rest, past, k_top, t, gp, nb):
    npage = 6 * gp * nb
    page_refs = rest[:npage]
    oa_ref, ob_ref = rest[npage:npage + 2]
    (m_sc, l_sc, acc_sc, r_sc, ka_sc, va_sc, key_sc, sel_sc,
     kbn_sc, vbn_sc, kan_sc, van_sc, kin_sc) = rest[npage + 2:]
    first = (pl.program_id(0) == 0) & (pl.program_id(1) == 0)
    s = pl.program_id(1)
    ns = ka_sc.shape[1]
    ps = kin_sc.shape[1]
    w = gp * ps
    rows = H_B * t
    nt8 = kbn_ref.shape[1]
    row_tok = lax.broadcasted_iota(I32, (rows, ps), 0) % t
    lane_r = lax.broadcasted_iota(I32, (rows, ps), 1)
    row8_tok = lax.broadcasted_iota(I32, (8, ps), 0) % t
    lane8 = lax.broadcasted_iota(I32, (8, ps), 1)
    chunk = ns - 1 - s

    def dup_scores(s32, bi):
        s32 = jnp.maximum(s32, 0.0) * (wi_ref[bi] * (H_IDX ** -0.5))
        sc = s32[0:t]
        for hh in range(1, H_IDX):
            sc = sc + s32[hh * t:(hh + 1) * t]
        return jnp.concatenate([sc] * (8 // t), axis=0)

    def key_rows(bi):
        return slice(8 * bi, 8 * (bi + 1))

    @pl.when(first)
    def _():
        for ref in (kbn_sc, vbn_sc, kan_sc, van_sc, kin_sc):
            ref[...] = jnp.zeros(ref.shape, F32)

    def new_rows(bi):
        kbn_sc[bi, 0:nt8, :] = kbn_ref[bi]
        vbn_sc[bi, 0:nt8, :] = vbn_ref[bi]
        kan_sc[bi, 0:nt8, :] = kan_ref[bi]
        van_sc[bi, 0:nt8, :] = van_ref[bi]
        kin_sc[bi, 0:nt8, :] = kin_ref[bi]
        r_sc[bi] = jnp.zeros(r_sc.shape[1:], F32)
        lg = _dot_nt(qb_ref[bi], kbn_sc[bi].astype(_MM)) + cq_ref[bi] - _per_head_rows(cqt_ref[bi], t)
        lg = jnp.where(lane_r <= row_tok, lg, NEG_INF)
        m = jnp.max(lg, axis=1, keepdims=True)
        pe = jnp.exp(lg - m)
        m_sc[bi] = m
        l_sc[bi] = jnp.sum(pe, axis=1, keepdims=True)
        acc_sc[bi] = _dot(pe.astype(_MM), vbn_sc[bi].astype(_MM))
        key_new = _order_key(dup_scores(_dot_nt(qi_ref[bi], kin_sc[bi].astype(_MM)), bi))
        key_new = jnp.where(lane8 <= row8_tok, key_new, INT_MIN)
        if gp > 1:
            key_new = jnp.concatenate([key_new, jnp.full((8, w - ps), INT_MIN, I32)], axis=1)
        key_sc[ns, key_rows(bi), :] = key_new

    @pl.when(s == 0)
    def _():
        for bi in range(nb):
            new_rows(bi)

    def pages(bi):
        refs = [page_refs[6 * (bi * gp + g):6 * (bi * gp + g) + 6] for g in range(gp)]
        cat = lambda parts: jnp.concatenate(parts, axis=1) if gp > 1 else parts[0]
        kt = cat([r[0][0, 0].reshape(H_B * HEAD_DIM, ps).astype(_MM) for r in refs])
        vt = cat([r[1][0, 0].reshape(H_B * HEAD_DIM, ps).astype(_MM) for r in refs])
        ka_sc[bi, chunk] = cat([r[2][0, 0].reshape(KV_A * HEAD_DIM, ps).astype(_MM) for r in refs])
        va_sc[bi, chunk] = cat([r[3][0, 0].reshape(KV_A * HEAD_DIM, ps).astype(_MM) for r in refs])
        kit = cat([r[4][0, 0].astype(_MM) for r in refs])
        lfts = [r[5][0, 0].astype(F32) for r in refs]

        suf_loc = _dot_f32(jnp.concatenate(lfts, axis=0), _strict_lower(ps))
        carry = r_sc[bi]
        sufs = [None] * gp
        for g in reversed(range(gp)):
            loc = suf_loc[g * H_B:(g + 1) * H_B]
            sufs[g] = loc + carry
            carry = carry + loc[:, 0:1] + lfts[g][:, 0:1]
        r_sc[bi] = carry

        lg = _dot(qb_ref[bi], kt) + (_per_head_rows(cat(sufs), t) + cq_ref[bi])
        m_prev = m_sc[bi]
        m_new = jnp.maximum(m_prev, jnp.max(lg, axis=1, keepdims=True))
        alpha = jnp.exp(m_prev - m_new)
        pe = jnp.exp(lg - m_new)
        l_sc[bi] = alpha * l_sc[bi] + jnp.sum(pe, axis=1, keepdims=True)
        acc_sc[bi] = alpha * acc_sc[bi] + _dot_nt(pe.astype(_MM), vt)
        m_sc[bi] = m_new
        key_sc[chunk, key_rows(bi), :] = _order_key(dup_scores(_dot(qi_ref[bi], kit), bi))

    for bi in range(nb):
        pages(bi)

    def fox_out(bi):
        o = acc_sc[bi] * (1.0 / l_sc[bi])
        lane_h = lax.broadcasted_iota(I32, o.shape, 1) // HEAD_DIM
        row_h = lax.broadcasted_iota(I32, o.shape, 0) // t
        o = jnp.where(lane_h == row_h, o, 0.0)
        ob = o[0:t]
        for hh in range(1, H_B):
            ob = ob + o[hh * t:(hh + 1) * t]
        ob_ref[bi] = ob

    def dsa_out(bi):
        nch = ns + 1
        qa = qa_ref[bi]
        lgs = []
        for c in range(nch):
            bias = sel_sc[c, key_rows(bi), :]
            if c < ns:
                prod = _dot(qa, ka_sc[bi, c])
            else:
                prod, bias = _dot_nt(qa, kan_sc[bi].astype(_MM)), bias[:, 0:ps]
            lgs.append(prod + jnp.concatenate([bias] * (rows // 8), axis=0))
        m = jnp.max(lgs[0], axis=1, keepdims=True)
        for c in range(1, nch):
            m = jnp.maximum(m, jnp.max(lgs[c], axis=1, keepdims=True))
        lsum = jnp.zeros((rows, 1), F32)
        out = jnp.zeros((rows, KV_A * HEAD_DIM), F32)
        for c in range(nch):
            pe = jnp.exp(lgs[c] - m)
            lsum = lsum + jnp.sum(pe, axis=1, keepdims=True)
            if c < ns:
                out = out + _dot_nt(pe.astype(_MM), va_sc[bi, c])
            else:
                out = out + _dot(pe.astype(_MM), van_sc[bi].astype(_MM))
        out = out * (1.0 / lsum)
        lane_j = lax.broadcasted_iota(I32, (t, KV_A * HEAD_DIM), 1) // HEAD_DIM
        for hh in range(H_A):
            piece = out[hh * t:(hh + 1) * t, :]
            oa_ref[bi, :, hh * LANES:(hh + 1) * LANES] = jnp.where(lane_j == hh // (H_A // KV_A), piece, 0.0)

    @pl.when(s == ns - 1)
    def _():
        for bi in range(nb):
            fox_out(bi)
        nch = ns + 1
        qpos = past + lax.broadcasted_iota(I32, (8 * nb, 1), 0) % t
        k_f = jnp.minimum(k_top, qpos + 1).astype(F32)
        load = lambda c, r0=0, nr=8 * nb: key_sc[c, r0:r0 + nr, :]
        tau, cnt = _kth_largest(load, nch, k_f, bits_per_pass=2)
        has_tie = jnp.max(cnt - k_f) > 0.0

        def store(c, sel):
            sel_sc[c] = jnp.where(sel, 0.0, NEG_INF)

        @pl.when(jnp.logical_not(has_tie))
        def _():
            for c in range(nch):
                store(c, load(c) >= tau)

        @pl.when(has_tie)
        def _():
            _select_with_ties(load, store, nch, w, tau, k_f)

        for bi in range(nb):
            dsa_out(bi)


def _attn_sample(layer, page_table, caches_t, qbd, qabd, qi32, wi32, cq32, cqt, kbn, vbn, kan, van, kin, k_top, t):
    kb_t, vb_t, ka_t, va_t, ki_t, lf_t = caches_t
    db, npg = page_table.shape
    ps = ki_t.shape[-1]
    rows = H_B * t
    gp = PAGES_PER_STEP if npg % PAGES_PER_STEP == 0 else 1
    nb = BATCHES_PER_STEP if db % BATCHES_PER_STEP == 0 else 1
    ns = npg // gp
    w = gp * ps
    const3 = lambda b, s, pt: (b, 0, 0)
    full = lambda a: BS((nb,) + a.shape[1:], const3)

    def page_specs(bi, g):
        def pg(b, s, pt):
            return pt[b * nb + bi, npg - (s + 1) * gp + g]
        five = lambda nh: BS((1, 1, nh, HEAD_DIM, ps), lambda b, s, pt: (layer, pg(b, s, pt), 0, 0, 0))
        return [five(H_B), five(H_B), five(KV_A), five(KV_A),
                BS((1, 1, D_IDX, ps), lambda b, s, pt: (layer, pg(b, s, pt), 0, 0)),
                BS((1, 1, H_B, ps), lambda b, s, pt: (layer, pg(b, s, pt), 0, 0))]

    small = [qbd, qabd, qi32, wi32, cq32, cqt, kbn, vbn, kan, van, kin]
    in_specs = [full(a) for a in small]
    pages = []
    for bi in range(nb):
        for g in range(gp):
            in_specs += page_specs(bi, g)
            pages += [kb_t, vb_t, ka_t, va_t, ki_t, lf_t]
    da = H_A * LANES
    db_ = H_B * HEAD_DIM
    dj = KV_A * HEAD_DIM
    return pl.pallas_call(
        functools.partial(_attn_s_kernel, past=npg * ps, k_top=k_top, t=t, gp=gp, nb=nb),
        out_shape=[SDS((db, t, da), F32), SDS((db, t, db_), F32)],
        grid_spec=pltpu.PrefetchScalarGridSpec(
            num_scalar_prefetch=1,
            grid=(db // nb, ns),
            in_specs=in_specs,
            out_specs=[BS((nb, t, da), const3), BS((nb, t, db_), const3)],
            scratch_shapes=[pltpu.VMEM((nb, rows, 1), F32), pltpu.VMEM((nb, rows, 1), F32),
                            pltpu.VMEM((nb, rows, db_), F32), pltpu.VMEM((nb, H_B, 1), F32),
                            pltpu.VMEM((nb, ns, dj, w), _MM), pltpu.VMEM((nb, ns, dj, w), _MM),
                            pltpu.VMEM((ns + 1, 8 * nb, w), I32), pltpu.VMEM((ns + 1, 8 * nb, w), F32),
                            pltpu.VMEM((nb, ps, db_), F32), pltpu.VMEM((nb, ps, db_), F32),
                            pltpu.VMEM((nb, ps, dj), F32), pltpu.VMEM((nb, ps, dj), F32),
                            pltpu.VMEM((nb, ps, D_IDX), F32)]),
        compiler_params=_cp("arbitrary", "arbitrary"),
        name="attn_sample",
    )(page_table, *small, *pages)


def _old2_attn_s_kernel(pt_ref, qb_ref, qa_ref, qi_ref, wi_ref, cq_ref, cqt_ref, kbn_ref, vbn_ref, kan_ref, van_ref,
                   kin_ref, *rest, past, k_top, t, gp):
    page_refs = rest[:6 * gp]
    oa_ref, ob_ref = rest[6 * gp:6 * gp + 2]
    (m_sc, l_sc, acc_sc, r_sc, kt_sc, vt_sc, ki_sc, ka_sc, va_sc, key_sc, sel_sc,
     kbn_sc, vbn_sc, kan_sc, van_sc, kin_sc) = rest[6 * gp + 2:]
    b = pl.program_id(0)
    s = pl.program_id(1)
    ns = ka_sc.shape[0]
    ps = kin_sc.shape[0]
    w = gp * ps
    rows = H_B * t
    nt8 = kbn_ref.shape[1]
    row_tok = lax.broadcasted_iota(I32, (rows, ps), 0) % t
    lane_r = lax.broadcasted_iota(I32, (rows, ps), 1)
    row8_tok = lax.broadcasted_iota(I32, (8, ps), 0) % t
    lane8 = lax.broadcasted_iota(I32, (8, ps), 1)
    wi = wi_ref[0] * (H_IDX ** -0.5)

    def dup_scores(s32):
        s32 = jnp.maximum(s32, 0.0) * wi
        sc = s32[0:t]
        for hh in range(1, H_IDX):
            sc = sc + s32[hh * t:(hh + 1) * t]
        return jnp.concatenate([sc] * (8 // t), axis=0)

    @pl.when((b == 0) & (s == 0))
    def _():
        for ref in (kbn_sc, vbn_sc, kan_sc, van_sc, kin_sc):
            ref[...] = jnp.zeros(ref.shape, F32)

    @pl.when(s == 0)
    def _():
        kbn_sc[0:nt8, :] = kbn_ref[0]
        vbn_sc[0:nt8, :] = vbn_ref[0]
        kan_sc[0:nt8, :] = kan_ref[0]
        van_sc[0:nt8, :] = van_ref[0]
        kin_sc[0:nt8, :] = kin_ref[0]
        r_sc[...] = jnp.zeros(r_sc.shape, F32)
        lg = _dot_nt(qb_ref[0], kbn_sc[...].astype(_MM)) + cq_ref[0] - _per_head_rows(cqt_ref[0], t)
        lg = jnp.where(lane_r <= row_tok, lg, NEG_INF)
        m = jnp.max(lg, axis=1, keepdims=True)
        pe = jnp.exp(lg - m)
        m_sc[...] = m
        l_sc[...] = jnp.sum(pe, axis=1, keepdims=True)
        acc_sc[...] = _dot(pe.astype(_MM), vbn_sc[...].astype(_MM))
        key_new = _order_key(dup_scores(_dot_nt(qi_ref[0], kin_sc[...].astype(_MM))))
        key_new = jnp.where(lane8 <= row8_tok, key_new, INT_MIN)
        if gp > 1:
            key_new = jnp.concatenate([key_new, jnp.full((8, w - ps), INT_MIN, I32)], axis=1)
        key_sc[ns] = key_new

    chunk = ns - 1 - s
    lfts = []
    for g in range(gp):
        kb_ref, vb_ref, ka_ref, va_ref, ki_ref, lf_ref = page_refs[6 * g:6 * g + 6]
        lanes = slice(g * ps, (g + 1) * ps)
        kt_sc[:, lanes] = kb_ref[0, 0].reshape(H_B * HEAD_DIM, ps).astype(_MM)
        vt_sc[:, lanes] = vb_ref[0, 0].reshape(H_B * HEAD_DIM, ps).astype(_MM)
        ki_sc[:, lanes] = ki_ref[0, 0].astype(_MM)
        ka_sc[chunk, :, lanes] = ka_ref[0, 0].reshape(KV_A * HEAD_DIM, ps).astype(_MM)
        va_sc[chunk, :, lanes] = va_ref[0, 0].reshape(KV_A * HEAD_DIM, ps).astype(_MM)
        lfts.append(lf_ref[0, 0].astype(F32))

    lf_all = jnp.concatenate(lfts, axis=0)
    suf_loc = _dot_f32(lf_all, _strict_lower(ps))
    carry = r_sc[...]
    sufs = [None] * gp
    for g in reversed(range(gp)):
        loc = suf_loc[g * H_B:(g + 1) * H_B]
        sufs[g] = loc + carry
        carry = carry + loc[:, 0:1] + lfts[g][:, 0:1]
    r_sc[...] = carry
    suf = jnp.concatenate(sufs, axis=1) if gp > 1 else sufs[0]

    lg = _dot(qb_ref[0], kt_sc[...]) + (_per_head_rows(suf, t) + cq_ref[0])
    m_prev = m_sc[...]
    m_new = jnp.maximum(m_prev, jnp.max(lg, axis=1, keepdims=True))
    alpha = jnp.exp(m_prev - m_new)
    pe = jnp.exp(lg - m_new)
    l_sc[...] = alpha * l_sc[...] + jnp.sum(pe, axis=1, keepdims=True)
    acc_sc[...] = alpha * acc_sc[...] + _dot_nt(pe.astype(_MM), vt_sc[...])
    m_sc[...] = m_new
    key_sc[chunk] = _order_key(dup_scores(_dot(qi_ref[0], ki_sc[...])))

    @pl.when(s == ns - 1)
    def _():
        o = acc_sc[...] * (1.0 / l_sc[...])
        lane_h = lax.broadcasted_iota(I32, o.shape, 1) // HEAD_DIM
        row_h = lax.broadcasted_iota(I32, o.shape, 0) // t
        o = jnp.where(lane_h == row_h, o, 0.0)
        ob = o[0:t]
        for hh in range(1, H_B):
            ob = ob + o[hh * t:(hh + 1) * t]
        ob_ref[0] = ob

        nch = ns + 1
        qpos = past + lax.broadcasted_iota(I32, (8, 1), 0) % t
        k_f = jnp.minimum(k_top, qpos + 1).astype(F32)
        load = lambda c: key_sc[c]
        tau, cnt = _kth_largest(load, nch, k_f, bits_per_pass=2)
        has_tie = jnp.max(cnt - k_f) > 0.0

        def store(c, sel):
            sel_sc[c] = jnp.where(sel, 0.0, NEG_INF)

        @pl.when(jnp.logical_not(has_tie))
        def _():
            for c in range(nch):
                store(c, load(c) >= tau)

        @pl.when(has_tie)
        def _():
            _select_with_ties(load, store, nch, w, tau, k_f)

        qa = qa_ref[0]
        lgs = []
        for c in range(nch):
            if c < ns:
                prod, bias = _dot(qa, ka_sc[c]), sel_sc[c]
            else:
                prod, bias = _dot_nt(qa, kan_sc[...].astype(_MM)), sel_sc[c][:, 0:ps]
            lgs.append(prod + jnp.concatenate([bias] * (rows // 8), axis=0))
        m = jnp.max(lgs[0], axis=1, keepdims=True)
        for c in range(1, nch):
            m = jnp.maximum(m, jnp.max(lgs[c], axis=1, keepdims=True))
        lsum = jnp.zeros((rows, 1), F32)
        out = jnp.zeros((rows, KV_A * HEAD_DIM), F32)
        for c in range(nch):
            pe = jnp.exp(lgs[c] - m)
            lsum = lsum + jnp.sum(pe, axis=1, keepdims=True)
            if c < ns:
                out = out + _dot_nt(pe.astype(_MM), va_sc[c])
            else:
                out = out + _dot(pe.astype(_MM), van_sc[...].astype(_MM))
        out = out * (1.0 / lsum)
        lane_j = lax.broadcasted_iota(I32, (t, KV_A * HEAD_DIM), 1) // HEAD_DIM
        for hh in range(H_A):
            piece = out[hh * t:(hh + 1) * t, :]
            oa_ref[0, :, hh * LANES:(hh + 1) * LANES] = jnp.where(lane_j == hh // (H_A // KV_A), piece, 0.0)


def _old2_attn_sample(layer, page_table, caches_t, qbd, qabd, qi32, wi32, cq32, cqt, kbn, vbn, kan, van, kin, k_top, t):
    kb_t, vb_t, ka_t, va_t, ki_t, lf_t = caches_t
    db, npg = page_table.shape
    ps = ki_t.shape[-1]
    rows = H_B * t
    gp = PAGES_PER_STEP if npg % PAGES_PER_STEP == 0 else 1
    ns = npg // gp
    w = gp * ps
    const3 = lambda b, s, pt: (b, 0, 0)
    full = lambda a: BS((1,) + a.shape[1:], const3)

    def page_specs(g):
        def pg(b, s, pt):
            return pt[b, npg - (s + 1) * gp + g]
        five = lambda nh: BS((1, 1, nh, HEAD_DIM, ps), lambda b, s, pt: (layer, pg(b, s, pt), 0, 0, 0))
        return [five(H_B), five(H_B), five(KV_A), five(KV_A),
                BS((1, 1, D_IDX, ps), lambda b, s, pt: (layer, pg(b, s, pt), 0, 0)),
                BS((1, 1, H_B, ps), lambda b, s, pt: (layer, pg(b, s, pt), 0, 0))]

    small = [qbd, qabd, qi32, wi32, cq32, cqt, kbn, vbn, kan, van, kin]
    in_specs = [full(a) for a in small]
    pages = []
    for g in range(gp):
        in_specs += page_specs(g)
        pages += [kb_t, vb_t, ka_t, va_t, ki_t, lf_t]
    da = H_A * LANES
    db_ = H_B * HEAD_DIM
    return pl.pallas_call(
        functools.partial(_attn_s_kernel, past=npg * ps, k_top=k_top, t=t, gp=gp),
        out_shape=[SDS((db, t, da), F32), SDS((db, t, db_), F32)],
        grid_spec=pltpu.PrefetchScalarGridSpec(
            num_scalar_prefetch=1,
            grid=(db, ns),
            in_specs=in_specs,
            out_specs=[BS((1, t, da), const3), BS((1, t, db_), const3)],
            scratch_shapes=[pltpu.VMEM((rows, 1), F32), pltpu.VMEM((rows, 1), F32), pltpu.VMEM((rows, db_), F32),
                            pltpu.VMEM((H_B, 1), F32),
                            pltpu.VMEM((db_, w), _MM), pltpu.VMEM((db_, w), _MM), pltpu.VMEM((D_IDX, w), _MM),
                            pltpu.VMEM((ns, KV_A * HEAD_DIM, w), _MM), pltpu.VMEM((ns, KV_A * HEAD_DIM, w), _MM),
                            pltpu.VMEM((ns + 1, 8, w), I32), pltpu.VMEM((ns + 1, 8, w), F32),
                            pltpu.VMEM((ps, db_), F32), pltpu.VMEM((ps, db_), F32),
                            pltpu.VMEM((ps, KV_A * HEAD_DIM), F32), pltpu.VMEM((ps, KV_A * HEAD_DIM), F32),
                            pltpu.VMEM((ps, D_IDX), F32)]),
        compiler_params=_cp("arbitrary", "arbitrary"),
        name="attn_sample",
    )(page_table, *small, *pages)


def _old_attn_s_kernel(pt_ref, qb_ref, qa_ref, qi_ref, wi_ref, fq_ref, kb_ref, vb_ref, ka_ref, va_ref, ki_ref,
                   fk_ref, kbn_ref, vbn_ref, kan_ref, van_ref, kin_ref, fkn_ref, oa_ref, ob_ref,
                   m_sc, l_sc, acc_sc, ka_sc, va_sc, key_sc, sel_sc, *, past, k_top, t):
    p = pl.program_id(1)
    npg = pl.num_programs(1)
    ps = ki_ref.shape[2]
    rows = H_B * t
    wb = ps * H_B
    wa = ps * KV_A

    @pl.when(p == 0)
    def _():
        m_sc[...] = jnp.full(m_sc.shape, NEG_INF, F32)
        l_sc[...] = jnp.zeros(l_sc.shape, F32)
        acc_sc[...] = jnp.zeros(acc_sc.shape, F32)

    k2 = kb_ref[0, 0].reshape(wb, HEAD_DIM).astype(_MM)
    v2 = vb_ref[0, 0].reshape(wb, HEAD_DIM).astype(_MM)
    lt = _dot_nt(qb_ref[0], k2) + fq_ref[0] - fk_ref[0]
    r_b = lax.broadcasted_iota(I32, (rows, wb), 0)
    c_b = lax.broadcasted_iota(I32, (rows, wb), 1)
    lt = jnp.where((c_b % H_B) == (r_b // t), lt, NEG_INF)
    _online_update(lt, v2, m_sc, l_sc, acc_sc)

    def dup_scores(ki2):
        s = _dot_nt(qi_ref[0], ki2)
        s = jnp.maximum(s, 0.0) * (wi_ref[0] * (H_IDX ** -0.5))
        sc = s[0:t]
        for hh in range(1, H_IDX):
            sc = sc + s[hh * t:(hh + 1) * t]
        return jnp.concatenate([sc] * (8 // t), axis=0)

    r_e = lax.broadcasted_iota(I32, (wa, ps), 0)
    c_e = lax.broadcasted_iota(I32, (wa, ps), 1)
    expand = jnp.where((r_e // KV_A) == c_e, 1.0, 0.0).astype(_MM)
    ki2 = _dot(expand, ki_ref[0, 0].astype(_MM)).astype(_MM)
    key_sc[p] = _order_key(dup_scores(ki2))
    ka_sc[p] = ka_ref[0, 0].reshape(wa, HEAD_DIM).astype(_MM)
    va_sc[p] = va_ref[0, 0].reshape(wa, HEAD_DIM).astype(_MM)

    @pl.when(p == npg - 1)
    def _():
        nn = kbn_ref.shape[1]
        ltn = _dot_nt(qb_ref[0], kbn_ref[0].astype(_MM)) + fq_ref[0] - fkn_ref[0]
        r_n = lax.broadcasted_iota(I32, (rows, nn), 0)
        c_n = lax.broadcasted_iota(I32, (rows, nn), 1)
        ok = ((c_n % H_B) == (r_n // t)) & ((c_n // H_B) <= (r_n % t))
        _online_update(jnp.where(ok, ltn, NEG_INF), vbn_ref[0].astype(_MM), m_sc, l_sc, acc_sc)
        ob_ref[0] = acc_sc[...] * (1.0 / l_sc[...])

        nch = key_sc.shape[0]
        r8 = lax.broadcasted_iota(I32, (8, wa), 0)
        c8 = lax.broadcasted_iota(I32, (8, wa), 1)
        ok_new = (c8 // KV_A) <= (r8 % t)
        key_new = _order_key(dup_scores(kin_ref[0].astype(_MM)))
        key_sc[nch - 1] = jnp.where(ok_new, key_new, INT_MIN)
        ka_sc[nch - 1] = kan_ref[0].astype(_MM)
        va_sc[nch - 1] = van_ref[0].astype(_MM)

        qpos = past + lax.broadcasted_iota(I32, (8, 1), 0) % t
        k_f = (KV_A * jnp.minimum(k_top, qpos + 1)).astype(F32)
        load = lambda c: key_sc[c]
        tau, cnt = _kth_largest(load, nch, k_f)
        has_tie = jnp.max(cnt - k_f) > 0.0

        def store(c, sel):
            sel_sc[c] = jnp.where(sel, 0.0, NEG_INF)

        @pl.when(jnp.logical_not(has_tie))
        def _():
            for c in range(nch):
                store(c, load(c) >= tau)

        @pl.when(has_tie)
        def _():
            _select_with_ties(load, store, nch, wa, tau, k_f)

        m_sc[...] = jnp.full(m_sc.shape, NEG_INF, F32)
        l_sc[...] = jnp.zeros(l_sc.shape, F32)
        acc_sc[...] = jnp.zeros(acc_sc.shape, F32)
        r_a = lax.broadcasted_iota(I32, (rows, wa), 0)
        c_a = lax.broadcasted_iota(I32, (rows, wa), 1)
        grp_ok = (c_a % KV_A) == (r_a // (t * (H_A // KV_A)))
        for c in range(nch):
            bias = jnp.concatenate([sel_sc[c]] * (rows // 8), axis=0)
            lg = _dot_nt(qa_ref[0], ka_sc[c]) + bias
            _online_update(jnp.where(grp_ok, lg, NEG_INF), va_sc[c], m_sc, l_sc, acc_sc)
        oa_ref[0] = acc_sc[...] * (1.0 / l_sc[...])


def _old_attn_sample(layer, page_table, caches, qb32, qa32, qi32, wi32, fq32, fk_row, kbn, vbn, kan, van, kin, fkn_row,
                 k_top, t):
    cache_a_k, cache_a_v, cache_a_idx_k, cache_b_k, cache_b_v = caches
    db, npg = page_table.shape
    ps = cache_a_idx_k.shape[2]
    rows = H_B * t
    wb, wa = ps * H_B, ps * KV_A
    past = npg * ps
    qspec = BS((1, rows, HEAD_DIM), lambda b, p, pt: (b, 0, 0))
    cspec = BS((1, rows, 1), lambda b, p, pt: (b, 0, 0))
    pool5 = lambda nh: BS((1, 1, ps, nh, HEAD_DIM), lambda b, p, pt: (layer, pt[b, p], 0, 0, 0))
    newspec = lambda r: BS((1, r, HEAD_DIM), lambda b, p, pt: (b, 0, 0))
    return pl.pallas_call(
        functools.partial(_attn_s_kernel, past=past, k_top=k_top, t=t),
        out_shape=[SDS((db, rows, HEAD_DIM), F32), SDS((db, rows, HEAD_DIM), F32)],
        grid_spec=pltpu.PrefetchScalarGridSpec(
            num_scalar_prefetch=1,
            grid=(db, npg),
            in_specs=[qspec, qspec, qspec, cspec, cspec,
                      pool5(H_B), pool5(H_B), pool5(KV_A), pool5(KV_A),
                      BS((1, 1, ps, D_IDX), lambda b, p, pt: (layer, pt[b, p], 0, 0)),
                      BS((1, 1, wb), lambda b, p, pt: (b, 0, p)),
                      newspec(rows), newspec(rows), newspec(wa), newspec(wa), newspec(wa),
                      BS((1, 1, rows), lambda b, p, pt: (b, 0, 0))],
            out_specs=[BS((1, rows, HEAD_DIM), lambda b, p, pt: (b, 0, 0)),
                       BS((1, rows, HEAD_DIM), lambda b, p, pt: (b, 0, 0))],
            scratch_shapes=[pltpu.VMEM((rows, 1), F32), pltpu.VMEM((rows, 1), F32),
                            pltpu.VMEM((rows, HEAD_DIM), F32),
                            pltpu.VMEM((npg + 1, wa, HEAD_DIM), _MM), pltpu.VMEM((npg + 1, wa, HEAD_DIM), _MM),
                            pltpu.VMEM((npg + 1, 8, wa), I32), pltpu.VMEM((npg + 1, 8, wa), F32)]),
        compiler_params=_cp("arbitrary", "arbitrary"),
        name="attn_sample",
    )(page_table, qb32, qa32, qi32, wi32, fq32, cache_b_k, cache_b_v, cache_a_k, cache_a_v, cache_a_idx_k,
      fk_row, kbn, vbn, kan, van, kin, fkn_row)


def _row_tile(n, pref):
    tm = min(pref, n)
    assert n % tm == 0, (n, tm)
    return tm


def _even_prompt(grp, layer, x, w, st):
    b, t = grp.nb, grp.t
    tm = _row_tile(t, ROW_TILE)
    att = st["att_layer"]
    outs = _inproj(grp, layer, x, w["norm_mix"], w["w_in_slots"], w["groups_slots"], w["b_f"], st["rope"], tm,
                   att, st["n_att"], st["att_prev"])
    qa_s, qi_s, qb_s, kb, vb, ka, va, misc = outs
    f = _cumsum_prompt(misc[att][:, MISC_FB:MISC_FB + H_B].reshape(b, t, H_B))
    o_b = _fox_prompt(qb_s, kb, vb, f, jnp.swapaxes(f, 1, 2), b, t, att)
    o_a = _dsa_prompt(qi_s, qa_s, misc, ka, va, b, t, min(TOPK_MAX, t // 4), att)
    x = _outproj(grp, layer, o_a.reshape(b * t, -1), o_b.reshape(b * t, -1), w["w_out_a_slots"], w["w_out_b"], x,
                 _row_tile(grp.n, FFN_ROW_TILE))
    return x, (kb, vb, ka, va, misc)


def _even_sample(grp, layer, x, w, st):
    db, t = grp.nb, grp.t
    n = grp.n
    assert 8 % t == 0, t
    att = st["att_layer"]
    outs = _inproj(grp, layer, x, w["norm_mix"], w["w_in_plain"], w["groups_plain"], w["b_f"], st["rope"], n,
                   att, st["n_att"], st["att_prev"], cum_t=t)
    qa, qi, qb = outs[:3]
    states = outs[3:]
    kb, vb, ka, va, misc = (a[att] for a in states)
    ki, wi = misc[:, :D_IDX], misc[:, MISC_WI:MISC_WI + H_IDX]
    cq = misc[:, MISC_CQ:MISC_CQ + H_B]
    page_table = st["page_table"]
    npg = page_table.shape[1]
    ps = st["caches_t"][0].shape[-1]

    def batch_major(a):
        return a.reshape(t, db, -1).transpose(1, 0, 2)

    def head_major(a, nh):
        return a.reshape(t, db, nh, -1).transpose(1, 2, 0, 3)

    eye_b = jnp.eye(H_B, dtype=qb.dtype)
    qbd = (head_major(qb, H_B)[:, :, :, None, :] * eye_b[None, :, None, :, None]).reshape(db, H_B * t, -1)
    grp_hot = (jnp.arange(H_A)[:, None] // (H_A // KV_A) == jnp.arange(KV_A)[None, :]).astype(qa.dtype)
    qabd = (head_major(qa, H_A)[:, :, :, None, :] * grp_hot[None, :, None, :, None]).reshape(db, H_A * t, -1)
    qi32 = head_major(qi, H_IDX).reshape(db, H_IDX * t, D_IDX)
    wi32 = head_major(wi, H_IDX).reshape(db, H_IDX * t, 1)
    cq32 = head_major(cq, H_B).reshape(db, H_B * t, 1)
    cqt = jnp.pad(batch_major(cq).transpose(0, 2, 1), ((0, 0), (0, 0), (0, ps - t)))
    pad8 = lambda a: jnp.pad(batch_major(a), ((0, 0), (0, 8 - t), (0, 0)))
    o_a, o_b = _attn_sample(att, page_table, st["caches_t"], qbd, qabd, qi32, wi32, cq32, cqt,
                            pad8(kb), pad8(vb), pad8(ka), pad8(va), pad8(ki),
                            min(TOPK_MAX, (npg * ps + t) // 4), t)
    token_major = lambda o: o.transpose(1, 0, 2).reshape(n, -1).astype(_MM)
    x = _outproj(grp, layer, token_major(o_a), token_major(o_b), w["w_out_a_slots"], w["w_out_b"], x,
                 _row_tile(n, FFN_ROW_TILE))
    return x, states


def _odd_prompt(grp, layer, x, w, st):
    tm = _row_tile(grp.t, ROW_TILE)
    u = _pw1(grp, layer, x, w["norm_mix"], w["w_pw1"], w["b_pw1"], _row_tile(grp.n, ROW_TILE))
    x = _conv_prompt(grp, layer, u, x, w["w_dw"], w["b_dw"], w["ln_g"], w["ln_b"], w["w_pw2"], w["b_pw2"], tm)
    nb, t = grp.nb, grp.t
    state = u.reshape(nb, t, -1)[:, t - (CONV_WIDTH - 1):]
    return x, state


def _odd_sample(grp, layer, x, w, st):
    db, t, d = grp.nb, grp.t, grp.d
    u = _pw1(grp, layer, x, w["norm_mix"], w["w_pw1"], w["b_pw1"], _row_tile(grp.n, ROW_TILE))
    buf = st["state_conv"][st["conv_layer"]]
    u_t = u.reshape(t, db, -1)
    x_t = _conv_sample(jnp.swapaxes(buf, 0, 1), u_t, x.reshape(t, db, d), st["mod_batch"][layer], w["w_dw"],
                       w["b_dw"], w["ln_g"], w["ln_b"], w["w_pw2"], w["b_pw2"])
    state = jnp.concatenate([buf.astype(F32), jnp.swapaxes(u_t, 0, 1)], axis=1)[:, t:]
    return x_t.reshape(db * t, d), state


def _trunk(grp, x, even_fn, odd_fn, layer_w, st, norm_final):
    depth = len(layer_w)
    att, conv = None, []
    for i in range(depth):
        w = layer_w[i]
        if i % 2 == 0:
            x, att = even_fn(grp, i, x, w, dict(st, att_layer=i // 2, n_att=(depth + 1) // 2, att_prev=att))
        else:
            x, s = odd_fn(grp, i, x, w, dict(st, conv_layer=i // 2))
            conv.append(s)
        tm = _row_tile(grp.n, FFN_ROW_TILE)
        x = _ffn(grp, i, x, w["norm_ffn"], w["w_ffn_in"], w["w_ffn_out"], norm_final, i == depth - 1, tm,
                 w["tf"])
    return x, att, conv


def kernel(x_prompt, x_sample, cache_a_k, cache_a_v, cache_a_idx_k, cache_b_k, cache_b_v, cache_b_logf,
           state_conv, page_table, c_prompt, c_sample, w_in_att, b_fgate, w_out_att, w_pw1, b_pw1, w_dw,
           b_dw, ln_conv_g, ln_conv_b, w_pw2, b_pw2, w_ada, b_ada, norm_mix, norm_ffn, w_ffn_in,
           w_ffn_out, norm_final):
    bp, sp, d = x_prompt.shape
    db, ts, _ = x_sample.shape
    depth = w_ada.shape[0]
    npg, ps = page_table.shape[1], cache_a_idx_k.shape[2]
    past = npg * ps

    mod = _ada(jnp.concatenate([c_prompt, c_sample], axis=0), w_ada, b_ada)
    mod_p = mod[:, :bp].reshape(depth, bp, 1, 6 * d)
    mod_b = mod[:, bp:]
    grp_p = _Group(bp, sp, d, mod_p, batch_major=True)
    grp_s = _Group(db, ts, d, mod_b.reshape(depth, 1, db, 6 * d), batch_major=False)

    ff = w_ffn_out.shape[1]
    tf = ff // 2 if ff % (2 * LANES) == 0 else ff
    half_a = H_A * HEAD_DIM
    layer_w = []
    w_ffn_in_mm, w_ffn_out_mm = w_ffn_in.astype(_MM), w_ffn_out.astype(_MM)
    for i in range(depth):
        w = {"norm_mix": norm_mix[i], "norm_ffn": norm_ffn[i], "w_ffn_in": w_ffn_in_mm, "w_ffn_out": w_ffn_out_mm,
             "tf": tf}
        l = i // 2
        if i % 2 == 0:
            w["w_in_slots"], w["groups_slots"] = _pack_w_in(w_in_att[l], True)
            w["w_in_plain"], w["groups_plain"] = _pack_w_in(w_in_att[l], False)
            w["b_f"] = b_fgate[l]
            wo = w_out_att[l]
            wa = wo[:half_a].reshape(H_A, HEAD_DIM, d)
            zero = jnp.zeros((HEAD_DIM, d), wo.dtype)
            parts = []
            for hh in range(H_A):
                parts += [wa[hh], zero] if hh // (H_A // KV_A) == 0 else [zero, wa[hh]]
            w["w_out_a_slots"] = jnp.concatenate(parts, axis=0).astype(_MM)
            w["w_out_a"] = wo[:half_a].astype(_MM)
            w["w_out_b"] = wo[half_a:].astype(_MM)
        else:
            w.update(w_pw1=w_pw1[l].astype(_MM), b_pw1=b_pw1[l], w_dw=w_dw[l], b_dw=b_dw[l], ln_g=ln_conv_g[l],
                     ln_b=ln_conv_b[l], w_pw2=w_pw2[l].astype(_MM), b_pw2=b_pw2[l])
        layer_w.append(w)

    st_p = {"rope": _rope_tables(jnp.arange(sp, dtype=I32).astype(F32))}
    pos_s = (past + jnp.arange(db * ts, dtype=I32) // db).astype(F32)
    pos_last5 = lambda a: jnp.transpose(a, (0, 1, 3, 4, 2))
    pos_last4 = lambda a: jnp.transpose(a, (0, 1, 3, 2))
    caches_t = (pos_last5(cache_b_k), pos_last5(cache_b_v), pos_last5(cache_a_k), pos_last5(cache_a_v),
                pos_last4(cache_a_idx_k), pos_last4(cache_b_logf))
    st_s = {"rope": _rope_tables(pos_s), "page_table": page_table, "caches_t": caches_t,
            "state_conv": state_conv, "mod_batch": mod_b}

    y_p, att_p, conv_p = _trunk(grp_p, x_prompt.reshape(bp * sp, d), _even_prompt, _odd_prompt, layer_w, st_p,
                                norm_final)
    y_s, att_s, conv_s = _trunk(grp_s, jnp.swapaxes(x_sample, 0, 1).reshape(db * ts, d), _even_sample, _odd_sample,
                                layer_w, st_s, norm_final)
    y_s = jnp.swapaxes(y_s.reshape(ts, db, d), 0, 1)

    def states(att, rows_to_batch):
        kb, vb, ka, va, misc = (rows_to_batch(a) for a in att)
        lead = kb.shape[:3]
        return (ka.reshape(*lead, KV_A, HEAD_DIM), va.reshape(*lead, KV_A, HEAD_DIM), misc[..., :D_IDX],
                kb.reshape(*lead, H_B, HEAD_DIM), vb.reshape(*lead, H_B, HEAD_DIM),
                misc[..., MISC_FB:MISC_FB + H_B])

    n_att = (depth + 1) // 2
    out_p = states(att_p, lambda a: a.reshape(n_att, bp, sp, -1))
    out_s = states(att_s, lambda a: jnp.swapaxes(a.reshape(n_att, ts, db, -1), 1, 2))
    return (y_p.reshape(bp, sp, d), y_s, *out_p, jnp.stack(conv_p), *out_s, jnp.stack(conv_s))
```

```python
import functools

import jax
import jax.numpy as jnp
from jax import lax
from jax.experimental import pallas as pl
from jax.experimental.pallas import tpu as pltpu

F32 = jnp.float32
I32 = jnp.int32
_MM = jnp.bfloat16

HEAD_DIM = 64
H_A = 8
KV_A = 2
H_IDX = 8
D_IDX = 64
H_B = 8
ROT_DIM = HEAD_DIM // 4
ROPE_THETA = 500000.0
TOPK_MAX = 256
CONV_WIDTH = 31
EPS = 1e-6
LANES = 128
INT_MIN = -(2 ** 31)
NEG_INF = float("-inf")
Q_SCALE = HEAD_DIM ** -0.5
VMEM_LIMIT = 56 * 1024 * 1024

ROW_TILE = 256
FFN_ROW_TILE = 512
FOX_TQ, FOX_TK = 512, 512
FOX_ROWS = 512
DSA_TQ = 256
DSA_CHUNK = 256
COUNT_ROWS = 32
DSA_BANDS = 8
PAGES_PER_STEP = 8
BATCHES_PER_STEP = 2

_SPLITS = (H_A * HEAD_DIM, KV_A * HEAD_DIM, KV_A * HEAD_DIM, H_IDX * D_IDX, D_IDX, H_IDX,
           H_B * HEAD_DIM, H_B * HEAD_DIM, H_B * HEAD_DIM, H_B)
_NAMES = ("qa", "ka", "va", "qi", "ki", "wi", "qb", "kb", "vb", "fb")
_OFF = {}
_o = 0
for _n, _w in zip(_NAMES, _SPLITS):
    _OFF[_n] = (_o, _w)
    _o += _w
MISC_WI = D_IDX
MISC_FB = D_IDX + H_IDX
MISC_CQ = MISC_FB + H_B

SDS = jax.ShapeDtypeStruct
BS = pl.BlockSpec


def _cp(*sem):
    return pltpu.CompilerParams(dimension_semantics=sem, vmem_limit_bytes=VMEM_LIMIT)


def _dot(a, b):
    return jnp.dot(a, b, preferred_element_type=F32)


def _dot_nt(a, b):
    return lax.dot_general(a, b, (((1,), (1,)), ((), ())), preferred_element_type=F32)


def _sigmoid(x):
    return 1.0 / (1.0 + jnp.exp(-x))


def _silu(x):
    return x * _sigmoid(x)


def _rms(x, g):
    return x * lax.rsqrt(jnp.mean(x * x, axis=-1, keepdims=True) + EPS) * g


def _modulate(x, g, shift, scale):
    return _rms(x, g) * (1.0 + scale) + shift


def _ada_kernel(c_ref, w_ref, b_ref, o_ref):
    a = _silu(c_ref[...]).astype(_MM)
    o_ref[0] = _dot(a, w_ref[0].astype(_MM)) + b_ref[0]


def _ada(c_all, w_ada, b_ada):
    depth, d, d6 = w_ada.shape
    r = c_all.shape[0]
    tn = d6 // 4
    return pl.pallas_call(
        _ada_kernel,
        out_shape=SDS((depth, r, d6), F32),
        grid=(depth, d6 // tn),
        in_specs=[BS((r, d), lambda l, j: (0, 0)),
                  BS((1, d, tn), lambda l, j: (l, 0, j)),
                  BS((1, 1, tn), lambda l, j: (l, 0, j))],
        out_specs=BS((1, r, tn), lambda l, j: (l, 0, j)),
        compiler_params=_cp("arbitrary", "arbitrary"),
        name="ada_mod",
    )(c_all, w_ada, b_ada.reshape(depth, 1, d6))


class _Group:
    def __init__(self, nb, t, d, mod, batch_major):
        self.nb, self.t, self.d = nb, t, d
        self.n = nb * t
        self.mod = mod
        self.batch_major = batch_major

    def mod_arg(self, layer, tm, chunk):
        d = self.d
        if self.batch_major:
            tpb = self.t // tm
            return self.mod[layer], BS((1, 1, d), lambda i, *_: (i // tpb, 0, chunk))
        assert tm % self.nb == 0, (tm, self.nb)
        return self.mod[layer], BS((1, self.nb, d), lambda i, *_: (0, 0, chunk))


def _rows(v, n):
    r = v.shape[0]
    return v if r in (1, n) else jnp.concatenate([v] * (n // r), axis=0)


def _rope_tables(pos):
    half = ROT_DIM // 2
    inv = ROPE_THETA ** (-jnp.arange(half, dtype=F32) * 2.0 / ROT_DIM)
    ang = pos[:, None] * inv[None, :]
    cos, sin = jnp.cos(ang), jnp.sin(ang)
    n = pos.shape[0]
    one = jnp.ones((n, HEAD_DIM - ROT_DIM), F32)
    zero = jnp.zeros((n, HEAD_DIM - ROT_DIM), F32)
    z8 = jnp.zeros((n, half), F32)
    c = jnp.concatenate([cos, cos, one], axis=1)
    s1 = jnp.concatenate([-sin, z8, zero], axis=1)
    s2 = jnp.concatenate([z8, sin, zero], axis=1)
    rep = LANES // HEAD_DIM
    return jnp.tile(c, (1, rep)), jnp.tile(s1, (1, rep)), jnp.tile(s2, (1, rep))


def _rope(y, c, s1, s2):
    w = y.shape[1]
    rep = w // LANES
    if rep > 1:
        c, s1, s2 = (jnp.concatenate([t] * rep, axis=1) for t in (c, s1, s2))
    half = ROT_DIM // 2
    return y * c + pltpu.roll(y, w - half, 1) * s1 + pltpu.roll(y, half, 1) * s2


def _log_sigmoid(x):
    return jnp.minimum(x, 0.0) - jnp.log(1.0 + jnp.exp(-jnp.abs(x)))


STATE_GROUPS = ("kb", "vb", "ka", "va", "misc")


def _inproj_kernel(x_ref, g_ref, sh_ref, sc_ref, w_ref, c_ref, s1_ref, s2_ref, bf_ref, *refs, groups, cum_t,
                   n_prev):
    out_refs = refs[n_prev:]
    x = x_ref[...]
    n_rows = x.shape[0]
    h = _modulate(x, g_ref[...], _rows(sh_ref[0], n_rows), _rows(sc_ref[0], n_rows)).astype(_MM)
    c, s1, s2 = c_ref[...], s1_ref[...], s2_ref[...]
    off = 0
    for (name, width, rope, scale), o_ref in zip(groups, out_refs):
        y = _dot(h, w_ref[:, off:off + width])
        off += width
        if name == "misc":
            lane = lax.broadcasted_iota(I32, y.shape, 1)
            yr = _rope(y, c, s1, s2)
            lf = _log_sigmoid(y + bf_ref[...])
            in_fb = (lane >= MISC_FB) & (lane < MISC_FB + H_B)
            if cum_t:
                stride = n_rows // cum_t
                tok = lax.broadcasted_iota(I32, y.shape, 0) // stride
                cum = lf
                for k in range(1, cum_t):
                    cum = cum + jnp.where(tok >= k, pltpu.roll(lf, k * stride, 0), 0.0)
                y = jnp.where((lane >= MISC_CQ) & (lane < MISC_CQ + H_B), pltpu.roll(cum, H_B, 1), y)
            y = jnp.where(lane < MISC_WI, yr, jnp.where(in_fb, lf, y))
        elif rope:
            y = _rope(y, c, s1, s2)
        if scale != 1.0:
            y = y * scale
        if name in STATE_GROUPS:
            o_ref[0] = y
        else:
            o_ref[...] = y.astype(o_ref.dtype)


def _pack_w_in(w_in, slots):
    d = w_in.shape[0]

    def cols(name):
        o, w = _OFF[name]
        return w_in[:, o:o + w]

    def slot(name, place):
        src = cols(name).reshape(d, -1, HEAD_DIM)
        zero = jnp.zeros((d, HEAD_DIM), w_in.dtype)
        parts = []
        for hh in range(src.shape[1]):
            parts += [src[:, hh], zero] if place(hh) == 0 else [zero, src[:, hh]]
        return jnp.concatenate(parts, axis=1)

    misc = jnp.concatenate([cols("ki"), cols("wi"), cols("fb"),
                            jnp.zeros((d, LANES - D_IDX - H_IDX - H_B), w_in.dtype)], axis=1)
    if slots:
        qa = slot("qa", lambda hh: hh // (H_A // KV_A))
        qi = slot("qi", lambda hh: 0)
        qb = slot("qb", lambda hh: hh % 2)
    else:
        qa, qi, qb = cols("qa"), cols("qi"), cols("qb")
    parts = [qa, qi, qb, cols("kb"), cols("vb"), cols("ka"), cols("va"), misc]
    groups = (("qa", qa.shape[1], True, Q_SCALE), ("qi", qi.shape[1], True, D_IDX ** -0.5),
              ("qb", qb.shape[1], False, Q_SCALE), ("kb", H_B * HEAD_DIM, False, 1.0),
              ("vb", H_B * HEAD_DIM, False, 1.0), ("ka", KV_A * HEAD_DIM, True, 1.0),
              ("va", KV_A * HEAD_DIM, False, 1.0), ("misc", LANES, False, 1.0))
    return jnp.concatenate(parts, axis=1).astype(_MM), groups


def _inproj(grp, layer, x, norm_g, w_packed, groups, b_f, tables, tm, att, n_att, prev, cum_t=0):
    n, d = x.shape
    nt = n // tm
    sh_arr, sh_spec = grp.mod_arg(layer, tm, 0)
    sc_arr, sc_spec = grp.mod_arg(layer, tm, 1)
    c, s1, s2 = tables
    tr = c.shape[0] // tm
    tspec = BS((tm, LANES), lambda i: (i % tr, 0))
    bf = jnp.zeros((1, LANES), F32).at[0, MISC_FB:MISC_FB + H_B].set(b_f)
    out_shape, out_specs = [], []
    for name, w, _, _ in groups:
        if name in STATE_GROUPS:
            out_shape.append(SDS((n_att, n, w), F32))
            out_specs.append(BS((1, tm, w), lambda i: (att, i, 0)))
        else:
            out_shape.append(SDS((n, w), _MM))
            out_specs.append(BS((tm, w), lambda i: (i, 0)))
    nc = w_packed.shape[1]
    in_specs = [BS((tm, d), lambda i: (i, 0)), BS((1, d), lambda i: (0, 0)), sh_spec, sc_spec,
                BS((d, nc), lambda i: (0, 0)), tspec, tspec, tspec, BS((1, LANES), lambda i: (0, 0))]
    args = [x, norm_g.reshape(1, d), sh_arr, sc_arr, w_packed, c, s1, s2, bf]
    aliases = {}
    if prev is not None:
        state_out = [k for k, g in enumerate(groups) if g[0] in STATE_GROUPS]
        for arr, k in zip(prev, state_out):
            aliases[len(args)] = k
            in_specs.append(BS(memory_space=pl.ANY))
            args.append(arr)
    return pl.pallas_call(
        functools.partial(_inproj_kernel, groups=groups, cum_t=cum_t, n_prev=len(aliases)),
        out_shape=out_shape,
        grid=(nt,),
        in_specs=in_specs,
        out_specs=out_specs,
        input_output_aliases=aliases,
        compiler_params=_cp("arbitrary"),
        name="even_inproj",
    )(*args)


def _tri_lower(n):
    r = lax.broadcasted_iota(I32, (n, n), 0)
    c = lax.broadcasted_iota(I32, (n, n), 1)
    return jnp.where(r >= c, 1.0, 0.0).astype(F32)


def _dot_f32(a, b):
    return jnp.dot(a, b, preferred_element_type=F32, precision=lax.Precision.HIGHEST)


def _cumsum_p_kernel(lf_ref, f_ref, *, tc):
    t = lf_ref.shape[1]
    tri = _tri_lower(tc)
    carry = jnp.zeros((1, lf_ref.shape[2]), F32)
    for c in range(t // tc):
        fc = _dot_f32(tri, lf_ref[0, c * tc:(c + 1) * tc, :]) + carry
        f_ref[0, c * tc:(c + 1) * tc, :] = fc
        carry = fc[tc - 1:tc, :]


def _cumsum_prompt(logf):
    b, t, hb = logf.shape
    tc = min(256, t)
    return pl.pallas_call(
        functools.partial(_cumsum_p_kernel, tc=tc),
        out_shape=SDS((b, t, hb), F32),
        grid=(b,),
        in_specs=[BS((1, t, hb), lambda i: (i, 0, 0))],
        out_specs=BS((1, t, hb), lambda i: (i, 0, 0)),
        compiler_params=_cp("arbitrary"),
        name="cumsum_prompt",
    )(logf)


def _fox_p_kernel(q_ref, k_ref, v_ref, fq_ref, fk_ref, o_ref, m_sc, l_sc, acc_sc, fq_sc, *, tq, tk):
    i = pl.program_id(1)
    j = pl.program_id(2)
    nk = pl.num_programs(2)

    @pl.when(j == 0)
    def _():
        m_sc[...] = jnp.full(m_sc.shape, NEG_INF, F32)
        l_sc[...] = jnp.zeros(l_sc.shape, F32)
        acc_sc[...] = jnp.zeros(acc_sc.shape, F32)
        fq = fq_ref[0]
        for hh in range(H_B):
            fq_sc[hh] = jnp.broadcast_to(fq[:, hh:hh + 1], (tq, LANES))

    def step(masked):
        k = k_ref[0, 0].astype(_MM)
        v = v_ref[0, 0].astype(_MM)
        fk = fk_ref[0]
        nr = min(FOX_ROWS, tq)
        lane = lax.broadcasted_iota(I32, (nr, LANES), 1)
        low = lane < HEAD_DIM
        for rc in range(tq // nr):
            rs = slice(rc * nr, (rc + 1) * nr)
            if masked:
                rows = i * tq + rc * nr + lax.broadcasted_iota(I32, (nr, tk), 0)
                cols = j * tk + lax.broadcasted_iota(I32, (nr, tk), 1)
                causal = cols <= rows
            for p in range(H_B // 2):
                kp = k[:, p * LANES:(p + 1) * LANES]
                vp = v[:, p * LANES:(p + 1) * LANES]
                alphas, pvs = [], []
                for e in range(2):
                    hh = 2 * p + e
                    z = _dot_nt(q_ref[0, rs, hh * LANES:(hh + 1) * LANES], kp) - fk[hh:hh + 1, :]
                    if masked:
                        z = jnp.where(causal, z, NEG_INF)
                    fq = fq_sc[hh, rs, :]
                    m_prev = m_sc[hh, rs, :]
                    m_new = jnp.maximum(m_prev, fq + jnp.max(z, axis=1, keepdims=True))
                    alpha = jnp.exp(m_prev - m_new)
                    pe = jnp.exp(z + jnp.concatenate([fq - m_new] * (tk // LANES), axis=1))
                    l_sc[hh, rs, :] = alpha * l_sc[hh, rs, :] + jnp.sum(pe, axis=1, keepdims=True)
                    m_sc[hh, rs, :] = m_new
                    alphas.append(alpha)
                    pvs.append(_dot(pe.astype(_MM), vp))
                acc_sc[p, rs, :] = (jnp.where(low, alphas[0], alphas[1]) * acc_sc[p, rs, :]
                                    + jnp.where(low, pvs[0], pvs[1]))

    visible = (j + 1) * tk <= i * tq + 1

    @pl.when(visible)
    def _():
        step(False)

    @pl.when(jnp.logical_not(visible) & (j * tk < (i + 1) * tq))
    def _():
        step(True)

    @pl.when(j == nk - 1)
    def _():
        lane = lax.broadcasted_iota(I32, (tq, LANES), 1)
        low = lane < HEAD_DIM
        for p in range(H_B // 2):
            linv = jnp.where(low, 1.0 / l_sc[2 * p], 1.0 / l_sc[2 * p + 1])
            o_ref[0, :, p * LANES:(p + 1) * LANES] = (acc_sc[p] * linv).astype(o_ref.dtype)


def _fox_prompt(qb_s, kb, vb, f, ft, b, t, att):
    tq, tk = min(FOX_TQ, t), min(FOX_TK, t)
    nq, nk = t // tq, t // tk
    dq = qb_s.shape[1]
    n_att, _, dk = kb.shape
    kmap = lambda bb, i, j: (att, bb, jnp.minimum(j, ((i + 1) * tq - 1) // tk), 0)
    return pl.pallas_call(
        functools.partial(_fox_p_kernel, tq=tq, tk=tk),
        out_shape=SDS((b, t, dk), _MM),
        grid=(b, nq, nk),
        in_specs=[BS((1, tq, dq), lambda bb, i, j: (bb, i, 0)),
                  BS((1, 1, tk, dk), kmap), BS((1, 1, tk, dk), kmap),
                  BS((1, tq, H_B), lambda bb, i, j: (bb, i, 0)),
                  BS((1, H_B, tk), lambda bb, i, j: (bb, 0, jnp.minimum(j, ((i + 1) * tq - 1) // tk)))],
        out_specs=BS((1, tq, dk), lambda bb, i, j: (bb, i, 0)),
        scratch_shapes=[pltpu.VMEM((H_B, tq, LANES), F32), pltpu.VMEM((H_B, tq, LANES), F32),
                        pltpu.VMEM((H_B // 2, tq, LANES), F32), pltpu.VMEM((H_B, tq, LANES), F32)],
        compiler_params=_cp("arbitrary", "arbitrary", "arbitrary"),
        name="fox_prompt",
    )(qb_s.reshape(b, t, dq), kb.reshape(n_att, b, t, dk), vb.reshape(n_att, b, t, dk), f, ft)


def _order_key(score):
    bits = pltpu.bitcast(score + 0.0, I32)
    return jnp.where(bits < 0, bits ^ jnp.int32(0x7FFFFFFF), bits)


def _kth_largest(load, nchunks, k_f, bits_per_pass=1):
    rows, width = load(0).shape
    rc = min(rows, COUNT_ROWS)

    def count_ge(cand):
        accs = []
        for r0 in range(0, rows, rc):
            acc = None
            for c in range(nchunks):
                x = jnp.where(load(c, r0, rc) >= cand[r0:r0 + rc], 1.0, 0.0)
                for j in range(width // LANES):
                    piece = x[:, j * LANES:(j + 1) * LANES]
                    acc = piece if acc is None else acc + piece
            accs.append(acc)
        folded = accs[0] if len(accs) == 1 else jnp.concatenate(accs, axis=0)
        return jnp.sum(folded, axis=1, keepdims=True)

    def body(it, tau):
        shift = 32 - bits_per_pass * (it + 1)
        best = tau
        for digit in range(1, 2 ** bits_per_pass):
            cand = tau ^ jnp.left_shift(jnp.int32(digit), shift)
            best = jnp.where(count_ge(cand) >= k_f, cand, best)
        return best

    tau = lax.fori_loop(0, 32 // bits_per_pass, body, jnp.full((rows, 1), INT_MIN, I32))
    return tau, count_ge(tau)


def _select_with_ties(load, store, nchunks, width, tau, k_f):
    n_gt = None
    for c in range(nchunks):
        x = jnp.sum(jnp.where(load(c) > tau, 1.0, 0.0), axis=1, keepdims=True)
        n_gt = x if n_gt is None else n_gt + x
    need = k_f - n_gt
    r = lax.broadcasted_iota(I32, (width, width), 0)
    cc = lax.broadcasted_iota(I32, (width, width), 1)
    upper = jnp.where(r <= cc, 1.0, 0.0).astype(_MM)
    carry = jnp.zeros_like(need)
    for c in range(nchunks):
        key = load(c)
        eq = key == tau
        prefix = _dot(jnp.where(eq, 1.0, 0.0).astype(_MM), upper) + carry
        store(c, (key > tau) | (eq & (prefix <= need)))
        carry = prefix[:, width - 1:width]


def _dsa_p_kernel(qi_ref, qa_ref, mq_ref, mk_ref, ka_ref, va_ref, *refs, tq, lk, q0, k_top, cw, n_prev):
    o_ref, key_sc, bias_sc = refs[n_prev:]
    i = pl.program_id(1)
    nch = lk // cw
    qpos = q0 + i * tq + lax.broadcasted_iota(I32, (tq, 1), 0)
    wi = mq_ref[0, 0][:, MISC_WI:MISC_WI + H_IDX] * (H_IDX ** -0.5)
    for c in range(nch):
        kmat = mk_ref[0, 0, c * cw:(c + 1) * cw, :].astype(_MM)
        score = jnp.zeros((tq, cw), F32)
        for hh in range(H_IDX):
            s = _dot_nt(qi_ref[0, :, hh * LANES:(hh + 1) * LANES], kmat)
            score = score + jnp.maximum(s, 0.0) * wi[:, hh:hh + 1]
        kpos = c * cw + lax.broadcasted_iota(I32, (tq, cw), 1)
        key_sc[:, c * cw:(c + 1) * cw] = jnp.where(kpos <= qpos, _order_key(score), INT_MIN)

    load = lambda c, r0=0, nr=tq: key_sc[r0:r0 + nr, c * cw:(c + 1) * cw]
    k_f = jnp.minimum(k_top, qpos + 1).astype(F32)
    tau, cnt = _kth_largest(load, nch, k_f)
    has_tie = jnp.max(cnt - k_f) > 0.0

    def store(c, sel):
        bias_sc[:, c * cw:(c + 1) * cw] = jnp.where(sel, 0.0, NEG_INF)

    @pl.when(jnp.logical_not(has_tie))
    def _():
        for c in range(nch):
            store(c, load(c) >= tau)

    @pl.when(has_tie)
    def _():
        _select_with_ties(load, store, nch, cw, tau, k_f)

    ka = ka_ref[0, 0].astype(_MM)
    va = va_ref[0, 0].astype(_MM)
    bias = bias_sc[...]
    lane = lax.broadcasted_iota(I32, (tq, LANES), 1)
    for hh in range(H_A):
        grp = hh // (H_A // KV_A)
        lg = _dot_nt(qa_ref[0, :, hh * LANES:(hh + 1) * LANES], ka) + bias
        m = jnp.max(lg, axis=1, keepdims=True)
        pe = jnp.exp(lg - m)
        l = jnp.sum(pe, axis=1, keepdims=True)
        o = _dot(pe.astype(_MM), va) * (1.0 / l)
        o = jnp.where((lane >= grp * HEAD_DIM) & (lane < (grp + 1) * HEAD_DIM), o, 0.0)
        o_ref[0, :, hh * LANES:(hh + 1) * LANES] = o.astype(o_ref.dtype)


def _dsa_prompt(qi_s, qa_s, misc, ka, va, b, t, k_top, att):
    bands = min(DSA_BANDS, t // LANES)
    band = t // bands
    tq = min(DSA_TQ, band)
    cw = min(DSA_CHUNK, band)
    dq = qi_s.shape[1]
    n_att = misc.shape[0]
    qi3, qa3 = qi_s.reshape(b, t, dq), qa_s.reshape(b, t, dq)
    misc4, ka4, va4 = (a.reshape(n_att, b, t, LANES) for a in (misc, ka, va))
    kmap = lambda bb, i: (att, bb, 0, 0)
    out = None
    for c in range(bands):
        lk = (c + 1) * band
        q_first = c * (band // tq)
        qmap = lambda bb, i, q_first=q_first: (bb, q_first + i, 0)
        qmap4 = lambda bb, i, q_first=q_first: (att, bb, q_first + i, 0)
        in_specs = [BS((1, tq, dq), qmap), BS((1, tq, dq), qmap), BS((1, 1, tq, LANES), qmap4),
                    BS((1, 1, lk, LANES), kmap), BS((1, 1, lk, LANES), kmap), BS((1, 1, lk, LANES), kmap)]
        args = [qi3, qa3, misc4, misc4, ka4, va4]
        aliases = {}
        if out is not None:
            aliases = {len(args): 0}
            in_specs.append(BS(memory_space=pl.ANY))
            args.append(out)
        out = pl.pallas_call(
            functools.partial(_dsa_p_kernel, tq=tq, lk=lk, q0=c * band, k_top=k_top, cw=cw, n_prev=len(aliases)),
            out_shape=SDS((b, t, dq), _MM),
            grid=(b, band // tq),
            in_specs=in_specs,
            out_specs=BS((1, tq, dq), qmap),
            scratch_shapes=[pltpu.VMEM((tq, lk), I32), pltpu.VMEM((tq, lk), F32)],
            input_output_aliases=aliases,
            compiler_params=_cp("arbitrary", "arbitrary"),
            name="dsa_prompt",
        )(*args)
    return out


def _outproj_kernel(oa_ref, ob_ref, wa_ref, wb_ref, x_ref, gate_ref, o_ref):
    y = _dot(oa_ref[...], wa_ref[...]) + _dot(ob_ref[...], wb_ref[...])
    o_ref[...] = x_ref[...] + _rows(gate_ref[0], y.shape[0]) * y


def _outproj(grp, layer, oa, ob, wa, wb, x, tm):
    n, d = x.shape
    g_arr, g_spec = grp.mod_arg(layer, tm, 2)
    da, db = oa.shape[1], ob.shape[1]
    return pl.pallas_call(
        _outproj_kernel,
        out_shape=SDS((n, d), F32),
        grid=(n // tm,),
        in_specs=[BS((tm, da), lambda i: (i, 0)), BS((tm, db), lambda i: (i, 0)),
                  BS((da, d), lambda i: (0, 0)), BS((db, d), lambda i: (0, 0)),
                  BS((tm, d), lambda i: (i, 0)), g_spec],
        out_specs=BS((tm, d), lambda i: (i, 0)),
        compiler_params=_cp("arbitrary"),
        name="even_outproj",
    )(oa, ob, wa, wb, x, g_arr)


def _ffn_kernel(x_ref, g_ref, sh_ref, sc_ref, gate_ref, wi_ref, wo_ref, gf_ref, o_ref, *, final, tf):
    ff = wo_ref.shape[1]
    x = x_ref[...]
    n_rows = x.shape[0]
    h = _modulate(x, g_ref[...], _rows(sh_ref[0], n_rows), _rows(sc_ref[0], n_rows)).astype(_MM)
    acc = None
    for c in range(ff // tf):
        gate_part = _dot(h, wi_ref[0, :, c * tf:(c + 1) * tf])
        up_part = _dot(h, wi_ref[0, :, ff + c * tf:ff + (c + 1) * tf])
        a = (_silu(gate_part) * up_part).astype(_MM)
        part = _dot(a, wo_ref[0, c * tf:(c + 1) * tf, :])
        acc = part if acc is None else acc + part
    y = x + _rows(gate_ref[0], n_rows) * acc
    if final:
        y = _rms(y, gf_ref[...])
    o_ref[...] = y


def _ffn(grp, layer, x, norm_g, w_in, w_out, norm_final, final, tm, tf):
    n, d = x.shape
    ff = w_out.shape[1]
    sh_arr, sh_spec = grp.mod_arg(layer, tm, 3)
    sc_arr, sc_spec = grp.mod_arg(layer, tm, 4)
    g_arr, g_spec = grp.mod_arg(layer, tm, 5)
    resident = lambda shape: BS((1,) + shape, lambda i: (layer, 0, 0), pipeline_mode=pl.Buffered(1))
    return pl.pallas_call(
        functools.partial(_ffn_kernel, final=final, tf=tf),
        out_shape=SDS((n, d), F32),
        grid=(n // tm,),
        in_specs=[BS((tm, d), lambda i: (i, 0)), BS((1, d), lambda i: (0, 0)), sh_spec, sc_spec, g_spec,
                  resident((d, 2 * ff)), resident((ff, d)), BS((1, d), lambda i: (0, 0))],
        out_specs=BS((tm, d), lambda i: (i, 0)),
        compiler_params=_cp("arbitrary"),
        name="ffn",
    )(x, norm_g.reshape(1, d), sh_arr, sc_arr, g_arr, w_in, w_out, norm_final.reshape(1, d))


def _pw1_kernel(x_ref, g_ref, sh_ref, sc_ref, w_ref, b_ref, u_ref):
    x = x_ref[...]
    n_rows = x.shape[0]
    h = _modulate(x, g_ref[...], _rows(sh_ref[0], n_rows), _rows(sc_ref[0], n_rows)).astype(_MM)
    y = _dot(h, w_ref[...]) + b_ref[...]
    dc = y.shape[1] // 2
    u_ref[...] = y[:, :dc] * _sigmoid(y[:, dc:])


def _pw1(grp, layer, x, norm_g, w, bias, tm):
    n, d = x.shape
    dc2 = w.shape[1]
    sh_arr, sh_spec = grp.mod_arg(layer, tm, 0)
    sc_arr, sc_spec = grp.mod_arg(layer, tm, 1)
    return pl.pallas_call(
        _pw1_kernel,
        out_shape=SDS((n, dc2 // 2), F32),
        grid=(n // tm,),
        in_specs=[BS((tm, d), lambda i: (i, 0)), BS((1, d), lambda i: (0, 0)), sh_spec, sc_spec,
                  BS((d, dc2), lambda i: (0, 0)), BS((1, dc2), lambda i: (0, 0))],
        out_specs=BS((tm, dc2 // 2), lambda i: (i, 0)),
        compiler_params=_cp("arbitrary"),
        name="conv_pw1_glu",
    )(x, norm_g.reshape(1, d), sh_arr, sc_arr, w, bias.reshape(1, dc2))


def _ln_swish_pw2(z, lng, lnb, w2, b2):
    mu = jnp.mean(z, axis=-1, keepdims=True)
    zc = z - mu
    var = jnp.mean(zc * zc, axis=-1, keepdims=True)
    zn = zc * lax.rsqrt(var + EPS) * lng + lnb
    return _dot(_silu(zn).astype(_MM), w2) + b2


HALO = 32


SUBLANES = 8
CONV_ROWS = 32


def _conv_p_kernel(ucur_ref, uhalo_ref, wdw_ref, bdw_ref, lng_ref, lnb_ref, w2_ref, b2_ref, x_ref, gate_ref,
                   o_ref, full_sc, shift_sc, z_sc, *, tm):
    i = pl.program_id(1)
    dc = full_sc.shape[1]
    full_sc[0:HALO, :] = jnp.where(i > 0, uhalo_ref[0], 0.0)
    full_sc[HALO:HALO + tm, :] = ucur_ref[0]
    span = shift_sc.shape[1]
    for q in range(1, SUBLANES):
        shift_sc[q - 1] = full_sc[q:q + span, :]
    base = HALO - (CONV_WIDTH - 1)

    def chunk(c, carry):
        r0 = pl.multiple_of(c * CONV_ROWS, CONV_ROWS)
        groups = CONV_ROWS // SUBLANES
        z = jnp.zeros((groups, SUBLANES, dc), F32) + bdw_ref[...]
        for w in range(CONV_WIDTH):
            q, al = (base + w) % SUBLANES, ((base + w) // SUBLANES) * SUBLANES
            src = full_sc if q == 0 else shift_sc.at[q - 1]
            slab = src[pl.ds(r0 + al, CONV_ROWS), :].reshape(groups, SUBLANES, dc)
            z = z + slab * wdw_ref[w][None]
        z_sc[pl.ds(r0, CONV_ROWS), :] = z.reshape(CONV_ROWS, dc)
        return carry

    lax.fori_loop(0, tm // CONV_ROWS, chunk, 0)
    y = _ln_swish_pw2(z_sc[...], lng_ref[...], lnb_ref[...], w2_ref[...], b2_ref[...])
    o_ref[0] = x_ref[0] + gate_ref[0] * y


def _conv_prompt(grp, layer, u, x, wdw, bdw, lng, lnb, w2, b2, tm):
    b, t, d = grp.nb, grp.t, grp.d
    dc = u.shape[1]
    g_arr, _ = grp.mod_arg(layer, tm, 2)
    hb = tm // HALO
    wrep = jnp.broadcast_to(wdw[:, None, :], (CONV_WIDTH, SUBLANES, dc))
    vec = lambda bb, i: (0, 0)
    return pl.pallas_call(
        functools.partial(_conv_p_kernel, tm=tm),
        out_shape=SDS((b, t, d), F32),
        grid=(b, t // tm),
        in_specs=[BS((1, tm, dc), lambda bb, i: (bb, i, 0)),
                  BS((1, HALO, dc), lambda bb, i: (bb, jnp.maximum(i * hb - 1, 0), 0)),
                  BS((CONV_WIDTH, SUBLANES, dc), lambda bb, i: (0, 0, 0)),
                  BS((1, dc), vec), BS((1, dc), vec), BS((1, dc), vec),
                  BS((dc, d), vec), BS((1, d), vec),
                  BS((1, tm, d), lambda bb, i: (bb, i, 0)),
                  BS((1, 1, d), lambda bb, i: (bb, 0, 2))],
        out_specs=BS((1, tm, d), lambda bb, i: (bb, i, 0)),
        scratch_shapes=[pltpu.VMEM((HALO + tm, dc), F32),
                        pltpu.VMEM((SUBLANES - 1, HALO + tm - SUBLANES, dc), F32),
                        pltpu.VMEM((tm, dc), F32)],
        compiler_params=_cp("arbitrary", "arbitrary"),
        name="conv_prompt",
    )(u.reshape(b, t, dc), u.reshape(b, t, dc), wrep, bdw.reshape(1, dc), lng.reshape(1, dc), lnb.reshape(1, dc),
      w2, b2.reshape(1, d), x.reshape(b, t, d), g_arr).reshape(b * t, d)


def _conv_s_kernel(buf_ref, u_ref, wdw_ref, bdw_ref, lng_ref, lnb_ref, w2_ref, b2_ref, x_ref, gate_ref, o_ref):
    nbuf = buf_ref.shape[0]
    t = u_ref.shape[0]
    zs = []
    for tt in range(t):
        z = jnp.zeros(u_ref.shape[1:], F32) + bdw_ref[...]
        for w in range(CONV_WIDTH):
            src = tt + w
            row = buf_ref[src] if src < nbuf else u_ref[src - nbuf]
            z = z + row * wdw_ref[w:w + 1, :]
        zs.append(z)
    y = _ln_swish_pw2(jnp.concatenate(zs, axis=0), lng_ref[...], lnb_ref[...], w2_ref[...], b2_ref[...])
    bb = u_ref.shape[1]
    for tt in range(t):
        o_ref[tt] = x_ref[tt] + gate_ref[...] * y[tt * bb:(tt + 1) * bb, :]


def _conv_sample(buf_t, u_t, x_t, gate, wdw, bdw, lng, lnb, w2, b2):
    nbuf, db, dc = buf_t.shape
    t, _, d = x_t.shape
    bb = min(32, db)
    wpad = jnp.zeros((HALO, dc), F32).at[:CONV_WIDTH].set(wdw)
    vec = lambda j: (0, 0)
    return pl.pallas_call(
        _conv_s_kernel,
        out_shape=SDS((t, db, d), F32),
        grid=(db // bb,),
        in_specs=[BS((nbuf, bb, dc), lambda j: (0, j, 0)), BS((t, bb, dc), lambda j: (0, j, 0)),
                  BS((HALO, dc), vec), BS((1, dc), vec), BS((1, dc), vec), BS((1, dc), vec),
                  BS((dc, d), vec), BS((1, d), vec),
                  BS((t, bb, d), lambda j: (0, j, 0)), BS((bb, d), lambda j: (j, 2))],
        out_specs=BS((t, bb, d), lambda j: (0, j, 0)),
        compiler_params=_cp("arbitrary"),
        name="conv_sample",
    )(buf_t, u_t, wpad, bdw.reshape(1, dc), lng.reshape(1, dc), lnb.reshape(1, dc), w2, b2.reshape(1, d), x_t, gate)


def _per_head_rows(x, t):
    nh, w = x.shape
    row = lax.broadcasted_iota(I32, (nh * t, w), 0) // t
    out = jnp.zeros((nh * t, w), x.dtype)
    for hh in range(nh):
        out = jnp.where(row == hh, x[hh:hh + 1, :], out)
    return out


def _strict_lower(n):
    r = lax.broadcasted_iota(I32, (n, n), 0)
    c = lax.broadcasted_iota(I32, (n, n), 1)
    return jnp.where(r > c, 1.0, 0.0).astype(F32)


def _attn_s_kernel(pt_ref, qb_ref, qa_ref, qi_ref, wi_ref, cq_ref, cqt_ref, kbn_ref, vbn_ref, kan_ref, van_ref,
                   kin_ref, *rest, past, k_top, t, gp, nb):
    npage = 6 * gp * nb
    page_refs = rest[:npage]
    oa_ref, ob_ref = rest[npage:npage + 2]
    (m_sc, l_sc, acc_sc, r_sc, ka_sc, va_sc, key_sc, sel_sc,
     kbn_sc, vbn_sc, kan_sc, van_sc, kin_sc) = rest[npage + 2:]
    first = (pl.program_id(0) == 0) & (pl.program_id(1) == 0)
    s = pl.program_id(1)
    ns = ka_sc.shape[1]
    ps = kin_sc.shape[1]
    w = gp * ps
    rows = H_B * t
    nt8 = kbn_ref.shape[1]
    row_tok = lax.broadcasted_iota(I32, (rows, ps), 0) % t
    lane_r = lax.broadcasted_iota(I32, (rows, ps), 1)
    row8_tok = lax.broadcasted_iota(I32, (8, ps), 0) % t
    lane8 = lax.broadcasted_iota(I32, (8, ps), 1)
    chunk = ns - 1 - s

    def dup_scores(s32, bi):
        s32 = jnp.maximum(s32, 0.0) * (wi_ref[bi] * (H_IDX ** -0.5))
        sc = s32[0:t]
        for hh in range(1, H_IDX):
            sc = sc + s32[hh * t:(hh + 1) * t]
        return jnp.concatenate([sc] * (8 // t), axis=0)

    def key_rows(bi):
        return slice(8 * bi, 8 * (bi + 1))

    @pl.when(first)
    def _():
        for ref in (kbn_sc, vbn_sc, kan_sc, van_sc, kin_sc):
            ref[...] = jnp.zeros(ref.shape, F32)

    def new_rows(bi):
        kbn_sc[bi, 0:nt8, :] = kbn_ref[bi]
        vbn_sc[bi, 0:nt8, :] = vbn_ref[bi]
        kan_sc[bi, 0:nt8, :] = kan_ref[bi]
        van_sc[bi, 0:nt8, :] = van_ref[bi]
        kin_sc[bi, 0:nt8, :] = kin_ref[bi]
        r_sc[bi] = jnp.zeros(r_sc.shape[1:], F32)
        lg = _dot_nt(qb_ref[bi], kbn_sc[bi].astype(_MM)) + cq_ref[bi] - _per_head_rows(cqt_ref[bi], t)
        lg = jnp.where(lane_r <= row_tok, lg, NEG_INF)
        m = jnp.max(lg, axis=1, keepdims=True)
        pe = jnp.exp(lg - m)
        m_sc[bi] = m
        l_sc[bi] = jnp.sum(pe, axis=1, keepdims=True)
        acc_sc[bi] = _dot(pe.astype(_MM), vbn_sc[bi].astype(_MM))
        key_new = _order_key(dup_scores(_dot_nt(qi_ref[bi], kin_sc[bi].astype(_MM)), bi))
        key_new = jnp.where(lane8 <= row8_tok, key_new, INT_MIN)
        if gp > 1:
            key_new = jnp.concatenate([key_new, jnp.full((8, w - ps), INT_MIN, I32)], axis=1)
        key_sc[ns, key_rows(bi), :] = key_new

    @pl.when(s == 0)
    def _():
        for bi in range(nb):
            new_rows(bi)

    def pages(bi):
        refs = [page_refs[6 * (bi * gp + g):6 * (bi * gp + g) + 6] for g in range(gp)]
        cat = lambda parts: jnp.concatenate(parts, axis=1) if gp > 1 else parts[0]
        kt = cat([r[0][0, 0].reshape(H_B * HEAD_DIM, ps).astype(_MM) for r in refs])
        vt = cat([r[1][0, 0].reshape(H_B * HEAD_DIM, ps).astype(_MM) for r in refs])
        ka_sc[bi, chunk] = cat([r[2][0, 0].reshape(KV_A * HEAD_DIM, ps).astype(_MM) for r in refs])
        va_sc[bi, chunk] = cat([r[3][0, 0].reshape(KV_A * HEAD_DIM, ps).astype(_MM) for r in refs])
        kit = cat([r[4][0, 0].astype(_MM) for r in refs])
        lfts = [r[5][0, 0].astype(F32) for r in refs]

        suf_loc = _dot_f32(jnp.concatenate(lfts, axis=0), _strict_lower(ps))
        carry = r_sc[bi]
        sufs = [None] * gp
        for g in reversed(range(gp)):
            loc = suf_loc[g * H_B:(g + 1) * H_B]
            sufs[g] = loc + carry
            carry = carry + loc[:, 0:1] + lfts[g][:, 0:1]
        r_sc[bi] = carry

        lg = _dot(qb_ref[bi], kt) + (_per_head_rows(cat(sufs), t) + cq_ref[bi])
        m_prev = m_sc[bi]
        m_new = jnp.maximum(m_prev, jnp.max(lg, axis=1, keepdims=True))
        alpha = jnp.exp(m_prev - m_new)
        pe = jnp.exp(lg - m_new)
        l_sc[bi] = alpha * l_sc[bi] + jnp.sum(pe, axis=1, keepdims=True)
        acc_sc[bi] = alpha * acc_sc[bi] + _dot_nt(pe.astype(_MM), vt)
        m_sc[bi] = m_new
        key_sc[chunk, key_rows(bi), :] = _order_key(dup_scores(_dot(qi_ref[bi], kit), bi))

    for bi in range(nb):
        pages(bi)

    def fox_out(bi):
        o = acc_sc[bi] * (1.0 / l_sc[bi])
        lane_h = lax.broadcasted_iota(I32, o.shape, 1) // HEAD_DIM
        row_h = lax.broadcasted_iota(I32, o.shape, 0) // t
        o = jnp.where(lane_h == row_h, o, 0.0)
        ob = o[0:t]
        for hh in range(1, H_B):
            ob = ob + o[hh * t:(hh + 1) * t]
        ob_ref[bi] = ob

    def dsa_out(bi):
        nch = ns + 1
        qa = qa_ref[bi]
        lgs = []
        for c in range(nch):
            bias = sel_sc[c, key_rows(bi), :]
            if c < ns:
                prod = _dot(qa, ka_sc[bi, c])
            else:
                prod, bias = _dot_nt(qa, kan_sc[bi].astype(_MM)), bias[:, 0:ps]
            lgs.append(prod + jnp.concatenate([bias] * (rows // 8), axis=0))
        m = jnp.max(lgs[0], axis=1, keepdims=True)
        for c in range(1, nch):
            m = jnp.maximum(m, jnp.max(lgs[c], axis=1, keepdims=True))
        lsum = jnp.zeros((rows, 1), F32)
        out = jnp.zeros((rows, KV_A * HEAD_DIM), F32)
        for c in range(nch):
            pe = jnp.exp(lgs[c] - m)
            lsum = lsum + jnp.sum(pe, axis=1, keepdims=True)
            if c < ns:
                out = out + _dot_nt(pe.astype(_MM), va_sc[bi, c])
            else:
                out = out + _dot(pe.astype(_MM), van_sc[bi].astype(_MM))
        out = out * (1.0 / lsum)
        lane_j = lax.broadcasted_iota(I32, (t, KV_A * HEAD_DIM), 1) // HEAD_DIM
        for hh in range(H_A):
            piece = out[hh * t:(hh + 1) * t, :]
            oa_ref[bi, :, hh * LANES:(hh + 1) * LANES] = jnp.where(lane_j == hh // (H_A // KV_A), piece, 0.0)

    @pl.when(s == ns - 1)
    def _():
        for bi in range(nb):
            fox_out(bi)
        nch = ns + 1
        qpos = past + lax.broadcasted_iota(I32, (8 * nb, 1), 0) % t
        k_f = jnp.minimum(k_top, qpos + 1).astype(F32)
        load = lambda c, r0=0, nr=8 * nb: key_sc[c, r0:r0 + nr, :]
        tau, cnt = _kth_largest(load, nch, k_f, bits_per_pass=2)
        has_tie = jnp.max(cnt - k_f) > 0.0

        def store(c, sel):
            sel_sc[c] = jnp.where(sel, 0.0, NEG_INF)

        @pl.when(jnp.logical_not(has_tie))
        def _():
            for c in range(nch):
                store(c, load(c) >= tau)

        @pl.when(has_tie)
        def _():
            _select_with_ties(load, store, nch, w, tau, k_f)

        for bi in range(nb):
            dsa_out(bi)


def _attn_sample(layer, page_table, caches_t, qbd, qabd, qi32, wi32, cq32, cqt, kbn, vbn, kan, van, kin, k_top, t):
    kb_t, vb_t, ka_t, va_t, ki_t, lf_t = caches_t
    db, npg = page_table.shape
    ps = ki_t.shape[-1]
    rows = H_B * t
    gp = PAGES_PER_STEP if npg % PAGES_PER_STEP == 0 else 1
    nb = BATCHES_PER_STEP if db % BATCHES_PER_STEP == 0 else 1
    ns = npg // gp
    w = gp * ps
    const3 = lambda b, s, pt: (b, 0, 0)
    full = lambda a: BS((nb,) + a.shape[1:], const3)

    def page_specs(bi, g):
        def pg(b, s, pt):
            return pt[b * nb + bi, npg - (s + 1) * gp + g]
        five = lambda nh: BS((1, 1, nh, HEAD_DIM, ps), lambda b, s, pt: (layer, pg(b, s, pt), 0, 0, 0))
        return [five(H_B), five(H_B), five(KV_A), five(KV_A),
                BS((1, 1, D_IDX, ps), lambda b, s, pt: (layer, pg(b, s, pt), 0, 0)),
                BS((1, 1, H_B, ps), lambda b, s, pt: (layer, pg(b, s, pt), 0, 0))]

    small = [qbd, qabd, qi32, wi32, cq32, cqt, kbn, vbn, kan, van, kin]
    in_specs = [full(a) for a in small]
    pages = []
    for bi in range(nb):
        for g in range(gp):
            in_specs += page_specs(bi, g)
            pages += [kb_t, vb_t, ka_t, va_t, ki_t, lf_t]
    da = H_A * LANES
    db_ = H_B * HEAD_DIM
    dj = KV_A * HEAD_DIM
    return pl.pallas_call(
        functools.partial(_attn_s_kernel, past=npg * ps, k_top=k_top, t=t, gp=gp, nb=nb),
        out_shape=[SDS((db, t, da), F32), SDS((db, t, db_), F32)],
        grid_spec=pltpu.PrefetchScalarGridSpec(
            num_scalar_prefetch=1,
            grid=(db // nb, ns),
            in_specs=in_specs,
            out_specs=[BS((nb, t, da), const3), BS((nb, t, db_), const3)],
            scratch_shapes=[pltpu.VMEM((nb, rows, 1), F32), pltpu.VMEM((nb, rows, 1), F32),
                            pltpu.VMEM((nb, rows, db_), F32), pltpu.VMEM((nb, H_B, 1), F32),
                            pltpu.VMEM((nb, ns, dj, w), _MM), pltpu.VMEM((nb, ns, dj, w), _MM),
                            pltpu.VMEM((ns + 1, 8 * nb, w), I32), pltpu.VMEM((ns + 1, 8 * nb, w), F32),
                            pltpu.VMEM((nb, ps, db_), F32), pltpu.VMEM((nb, ps, db_), F32),
                            pltpu.VMEM((nb, ps, dj), F32), pltpu.VMEM((nb, ps, dj), F32),
                            pltpu.VMEM((nb, ps, D_IDX), F32)]),
        compiler_params=_cp("arbitrary", "arbitrary"),
        name="attn_sample",
    )(page_table, *small, *pages)


def _row_tile(n, pref):
    tm = min(pref, n)
    assert n % tm == 0, (n, tm)
    return tm


def _even_prompt(grp, layer, x, w, st):
    b, t = grp.nb, grp.t
    tm = _row_tile(t, ROW_TILE)
    att = st["att_layer"]
    outs = _inproj(grp, layer, x, w["norm_mix"], w["w_in_slots"], w["groups_slots"], w["b_f"], st["rope"], tm,
                   att, st["n_att"], st["att_prev"])
    qa_s, qi_s, qb_s, kb, vb, ka, va, misc = outs
    f = _cumsum_prompt(misc[att][:, MISC_FB:MISC_FB + H_B].reshape(b, t, H_B))
    o_b = _fox_prompt(qb_s, kb, vb, f, jnp.swapaxes(f, 1, 2), b, t, att)
    o_a = _dsa_prompt(qi_s, qa_s, misc, ka, va, b, t, min(TOPK_MAX, t // 4), att)
    x = _outproj(grp, layer, o_a.reshape(b * t, -1), o_b.reshape(b * t, -1), w["w_out_a_slots"], w["w_out_b"], x,
                 _row_tile(grp.n, FFN_ROW_TILE))
    return x, (kb, vb, ka, va, misc)


def _even_sample(grp, layer, x, w, st):
    db, t = grp.nb, grp.t
    n = grp.n
    assert 8 % t == 0, t
    att = st["att_layer"]
    outs = _inproj(grp, layer, x, w["norm_mix"], w["w_in_plain"], w["groups_plain"], w["b_f"], st["rope"], n,
                   att, st["n_att"], st["att_prev"], cum_t=t)
    qa, qi, qb = outs[:3]
    states = outs[3:]
    kb, vb, ka, va, misc = (a[att] for a in states)
    ki, wi = misc[:, :D_IDX], misc[:, MISC_WI:MISC_WI + H_IDX]
    cq = misc[:, MISC_CQ:MISC_CQ + H_B]
    page_table = st["page_table"]
    npg = page_table.shape[1]
    ps = st["caches_t"][0].shape[-1]

    def batch_major(a):
        return a.reshape(t, db, -1).transpose(1, 0, 2)

    def head_major(a, nh):
        return a.reshape(t, db, nh, -1).transpose(1, 2, 0, 3)

    eye_b = jnp.eye(H_B, dtype=qb.dtype)
    qbd = (head_major(qb, H_B)[:, :, :, None, :] * eye_b[None, :, None, :, None]).reshape(db, H_B * t, -1)
    grp_hot = (jnp.arange(H_A)[:, None] // (H_A // KV_A) == jnp.arange(KV_A)[None, :]).astype(qa.dtype)
    qabd = (head_major(qa, H_A)[:, :, :, None, :] * grp_hot[None, :, None, :, None]).reshape(db, H_A * t, -1)
    qi32 = head_major(qi, H_IDX).reshape(db, H_IDX * t, D_IDX)
    wi32 = head_major(wi, H_IDX).reshape(db, H_IDX * t, 1)
    cq32 = head_major(cq, H_B).reshape(db, H_B * t, 1)
    cqt = jnp.pad(batch_major(cq).transpose(0, 2, 1), ((0, 0), (0, 0), (0, ps - t)))
    pad8 = lambda a: jnp.pad(batch_major(a), ((0, 0), (0, 8 - t), (0, 0)))
    o_a, o_b = _attn_sample(att, page_table, st["caches_t"], qbd, qabd, qi32, wi32, cq32, cqt,
                            pad8(kb), pad8(vb), pad8(ka), pad8(va), pad8(ki),
                            min(TOPK_MAX, (npg * ps + t) // 4), t)
    token_major = lambda o: o.transpose(1, 0, 2).reshape(n, -1).astype(_MM)
    x = _outproj(grp, layer, token_major(o_a), token_major(o_b), w["w_out_a_slots"], w["w_out_b"], x,
                 _row_tile(n, FFN_ROW_TILE))
    return x, states


def _odd_prompt(grp, layer, x, w, st):
    tm = _row_tile(grp.t, ROW_TILE)
    u = _pw1(grp, layer, x, w["norm_mix"], w["w_pw1"], w["b_pw1"], _row_tile(grp.n, ROW_TILE))
    x = _conv_prompt(grp, layer, u, x, w["w_dw"], w["b_dw"], w["ln_g"], w["ln_b"], w["w_pw2"], w["b_pw2"], tm)
    nb, t = grp.nb, grp.t
    state = u.reshape(nb, t, -1)[:, t - (CONV_WIDTH - 1):]
    return x, state


def _odd_sample(grp, layer, x, w, st):
    db, t, d = grp.nb, grp.t, grp.d
    u = _pw1(grp, layer, x, w["norm_mix"], w["w_pw1"], w["b_pw1"], _row_tile(grp.n, ROW_TILE))
    buf = st["state_conv"][st["conv_layer"]]
    u_t = u.reshape(t, db, -1)
    x_t = _conv_sample(jnp.swapaxes(buf, 0, 1), u_t, x.reshape(t, db, d), st["mod_batch"][layer], w["w_dw"],
                       w["b_dw"], w["ln_g"], w["ln_b"], w["w_pw2"], w["b_pw2"])
    state = jnp.concatenate([buf.astype(F32), jnp.swapaxes(u_t, 0, 1)], axis=1)[:, t:]
    return x_t.reshape(db * t, d), state


def _trunk(grp, x, even_fn, odd_fn, layer_w, st, norm_final):
    depth = len(layer_w)
    att, conv = None, []
    for i in range(depth):
        w = layer_w[i]
        if i % 2 == 0:
            x, att = even_fn(grp, i, x, w, dict(st, att_layer=i // 2, n_att=(depth + 1) // 2, att_prev=att))
        else:
            x, s = odd_fn(grp, i, x, w, dict(st, conv_layer=i // 2))
            conv.append(s)
        tm = _row_tile(grp.n, FFN_ROW_TILE)
        x = _ffn(grp, i, x, w["norm_ffn"], w["w_ffn_in"], w["w_ffn_out"], norm_final, i == depth - 1, tm,
                 w["tf"])
    return x, att, conv


def kernel(x_prompt, x_sample, cache_a_k, cache_a_v, cache_a_idx_k, cache_b_k, cache_b_v, cache_b_logf,
           state_conv, page_table, c_prompt, c_sample, w_in_att, b_fgate, w_out_att, w_pw1, b_pw1, w_dw,
           b_dw, ln_conv_g, ln_conv_b, w_pw2, b_pw2, w_ada, b_ada, norm_mix, norm_ffn, w_ffn_in,
           w_ffn_out, norm_final):
    bp, sp, d = x_prompt.shape
    db, ts, _ = x_sample.shape
    depth = w_ada.shape[0]
    npg, ps = page_table.shape[1], cache_a_idx_k.shape[2]
    past = npg * ps

    mod = _ada(jnp.concatenate([c_prompt, c_sample], axis=0), w_ada, b_ada)
    mod_p = mod[:, :bp].reshape(depth, bp, 1, 6 * d)
    mod_b = mod[:, bp:]
    grp_p = _Group(bp, sp, d, mod_p, batch_major=True)
    grp_s = _Group(db, ts, d, mod_b.reshape(depth, 1, db, 6 * d), batch_major=False)

    ff = w_ffn_out.shape[1]
    tf = ff // 2 if ff % (2 * LANES) == 0 else ff
    half_a = H_A * HEAD_DIM
    layer_w = []
    w_ffn_in_mm, w_ffn_out_mm = w_ffn_in.astype(_MM), w_ffn_out.astype(_MM)
    for i in range(depth):
        w = {"norm_mix": norm_mix[i], "norm_ffn": norm_ffn[i], "w_ffn_in": w_ffn_in_mm, "w_ffn_out": w_ffn_out_mm,
             "tf": tf}
        l = i // 2
        if i % 2 == 0:
            w["w_in_slots"], w["groups_slots"] = _pack_w_in(w_in_att[l], True)
            w["w_in_plain"], w["groups_plain"] = _pack_w_in(w_in_att[l], False)
            w["b_f"] = b_fgate[l]
            wo = w_out_att[l]
            wa = wo[:half_a].reshape(H_A, HEAD_DIM, d)
            zero = jnp.zeros((HEAD_DIM, d), wo.dtype)
            parts = []
            for hh in range(H_A):
                parts += [wa[hh], zero] if hh // (H_A // KV_A) == 0 else [zero, wa[hh]]
            w["w_out_a_slots"] = jnp.concatenate(parts, axis=0).astype(_MM)
            w["w_out_b"] = wo[half_a:].astype(_MM)
        else:
            w.update(w_pw1=w_pw1[l].astype(_MM), b_pw1=b_pw1[l], w_dw=w_dw[l], b_dw=b_dw[l], ln_g=ln_conv_g[l],
                     ln_b=ln_conv_b[l], w_pw2=w_pw2[l].astype(_MM), b_pw2=b_pw2[l])
        layer_w.append(w)

    st_p = {"rope": _rope_tables(jnp.arange(sp, dtype=I32).astype(F32))}
    pos_s = (past + jnp.arange(db * ts, dtype=I32) // db).astype(F32)
    pos_last5 = lambda a: jnp.transpose(a, (0, 1, 3, 4, 2))
    pos_last4 = lambda a: jnp.transpose(a, (0, 1, 3, 2))
    caches_t = (pos_last5(cache_b_k), pos_last5(cache_b_v), pos_last5(cache_a_k), pos_last5(cache_a_v),
                pos_last4(cache_a_idx_k), pos_last4(cache_b_logf))
    st_s = {"rope": _rope_tables(pos_s), "page_table": page_table, "caches_t": caches_t,
            "state_conv": state_conv, "mod_batch": mod_b}

    y_p, att_p, conv_p = _trunk(grp_p, x_prompt.reshape(bp * sp, d), _even_prompt, _odd_prompt, layer_w, st_p,
                                norm_final)
    y_s, att_s, conv_s = _trunk(grp_s, jnp.swapaxes(x_sample, 0, 1).reshape(db * ts, d), _even_sample, _odd_sample,
                                layer_w, st_s, norm_final)
    y_s = jnp.swapaxes(y_s.reshape(ts, db, d), 0, 1)

    def states(att, rows_to_batch):
        kb, vb, ka, va, misc = (rows_to_batch(a) for a in att)
        lead = kb.shape[:3]
        return (ka.reshape(*lead, KV_A, HEAD_DIM), va.reshape(*lead, KV_A, HEAD_DIM), misc[..., :D_IDX],
                kb.reshape(*lead, H_B, HEAD_DIM), vb.reshape(*lead, H_B, HEAD_DIM),
                misc[..., MISC_FB:MISC_FB + H_B])

    n_att = (depth + 1) // 2
    out_p = states(att_p, lambda a: a.reshape(n_att, bp, sp, -1))
    out_s = states(att_s, lambda a: jnp.swapaxes(a.reshape(n_att, ts, db, -1), 1, 2))
    return (y_p.reshape(bp, sp, d), y_s, *out_p, jnp.stack(conv_p), *out_s, jnp.stack(conv_s))
```

```python
import functools

import jax
import jax.numpy as jnp
from jax import lax
from jax.experimental import pallas as pl
from jax.experimental.pallas import tpu as pltpu

F32 = jnp.float32
I32 = jnp.int32
_MM = jnp.bfloat16

HEAD_DIM = 64
H_A = 8
KV_A = 2
H_IDX = 8
D_IDX = 64
H_B = 8
ROT_DIM = HEAD_DIM // 4
ROPE_THETA = 500000.0
TOPK_MAX = 256
CONV_WIDTH = 31
EPS = 1e-6
LANES = 128
INT_MIN = -(2 ** 31)
NEG_INF = float("-inf")
Q_SCALE = HEAD_DIM ** -0.5
VMEM_LIMIT = 56 * 1024 * 1024

ROW_TILE = 256
FFN_ROW_TILE = 512
FOX_TQ, FOX_TK = 512, 512
FOX_ROWS = 512
DSA_TQ = 256
DSA_CHUNK = 256
COUNT_ROWS = 32
DSA_BANDS = 8
PAGES_PER_STEP = 8
BATCHES_PER_STEP = 2

_SPLITS = (H_A * HEAD_DIM, KV_A * HEAD_DIM, KV_A * HEAD_DIM, H_IDX * D_IDX, D_IDX, H_IDX,
           H_B * HEAD_DIM, H_B * HEAD_DIM, H_B * HEAD_DIM, H_B)
_NAMES = ("qa", "ka", "va", "qi", "ki", "wi", "qb", "kb", "vb", "fb")
_OFF = {}
_o = 0
for _n, _w in zip(_NAMES, _SPLITS):
    _OFF[_n] = (_o, _w)
    _o += _w
MISC_WI = D_IDX
MISC_FB = D_IDX + H_IDX
MISC_CQ = MISC_FB + H_B

SDS = jax.ShapeDtypeStruct
BS = pl.BlockSpec


def _cp(*sem):
    return pltpu.CompilerParams(dimension_semantics=sem, vmem_limit_bytes=VMEM_LIMIT)


def _dot(a, b):
    return jnp.dot(a, b, preferred_element_type=F32)


def _dot_nt(a, b):
    return lax.dot_general(a, b, (((1,), (1,)), ((), ())), preferred_element_type=F32)


def _sigmoid(x):
    return 1.0 / (1.0 + jnp.exp(-x))


def _silu(x):
    return x * _sigmoid(x)


def _rms(x, g):
    return x * lax.rsqrt(jnp.mean(x * x, axis=-1, keepdims=True) + EPS) * g


def _modulate(x, g, shift, scale):
    return _rms(x, g) * (1.0 + scale) + shift


def _ada_kernel(c_ref, w_ref, b_ref, o_ref):
    a = _silu(c_ref[...]).astype(_MM)
    o_ref[0] = _dot(a, w_ref[0].astype(_MM)) + b_ref[0]


def _ada(c_all, w_ada, b_ada):
    depth, d, d6 = w_ada.shape
    r = c_all.shape[0]
    tn = d6 // 4
    return pl.pallas_call(
        _ada_kernel,
        out_shape=SDS((depth, r, d6), F32),
        grid=(depth, d6 // tn),
        in_specs=[BS((r, d), lambda l, j: (0, 0)),
                  BS((1, d, tn), lambda l, j: (l, 0, j)),
                  BS((1, 1, tn), lambda l, j: (l, 0, j))],
        out_specs=BS((1, r, tn), lambda l, j: (l, 0, j)),
        compiler_params=_cp("arbitrary", "arbitrary"),
        name="ada_mod",
    )(c_all, w_ada, b_ada.reshape(depth, 1, d6))


class _Group:
    def __init__(self, nb, t, d, mod, batch_major):
        self.nb, self.t, self.d = nb, t, d
        self.n = nb * t
        self.mod = mod
        self.batch_major = batch_major

    def mod_arg(self, layer, tm, chunk):
        d = self.d
        if self.batch_major:
            tpb = self.t // tm
            return self.mod[layer], BS((1, 1, d), lambda i, *_: (i // tpb, 0, chunk))
        assert tm % self.nb == 0, (tm, self.nb)
        return self.mod[layer], BS((1, self.nb, d), lambda i, *_: (0, 0, chunk))


def _rows(v, n):
    r = v.shape[0]
    return v if r in (1, n) else jnp.concatenate([v] * (n // r), axis=0)


def _rope_tables(pos):
    half = ROT_DIM // 2
    inv = ROPE_THETA ** (-jnp.arange(half, dtype=F32) * 2.0 / ROT_DIM)
    ang = pos[:, None] * inv[None, :]
    cos, sin = jnp.cos(ang), jnp.sin(ang)
    n = pos.shape[0]
    one = jnp.ones((n, HEAD_DIM - ROT_DIM), F32)
    zero = jnp.zeros((n, HEAD_DIM - ROT_DIM), F32)
    z8 = jnp.zeros((n, half), F32)
    c = jnp.concatenate([cos, cos, one], axis=1)
    s1 = jnp.concatenate([-sin, z8, zero], axis=1)
    s2 = jnp.concatenate([z8, sin, zero], axis=1)
    rep = LANES // HEAD_DIM
    return jnp.tile(c, (1, rep)), jnp.tile(s1, (1, rep)), jnp.tile(s2, (1, rep))


def _rope(y, c, s1, s2):
    w = y.shape[1]
    rep = w // LANES
    if rep > 1:
        c, s1, s2 = (jnp.concatenate([t] * rep, axis=1) for t in (c, s1, s2))
    half = ROT_DIM // 2
    return y * c + pltpu.roll(y, w - half, 1) * s1 + pltpu.roll(y, half, 1) * s2


def _log_sigmoid(x):
    return jnp.minimum(x, 0.0) - jnp.log(1.0 + jnp.exp(-jnp.abs(x)))


STATE_GROUPS = ("kb", "vb", "ka", "va", "misc")


def _inproj_kernel(x_ref, g_ref, sh_ref, sc_ref, w_ref, c_ref, s1_ref, s2_ref, bf_ref, *refs, groups, cum_t,
                   n_prev):
    out_refs = refs[n_prev:]
    x = x_ref[...]
    n_rows = x.shape[0]
    h = _modulate(x, g_ref[...], _rows(sh_ref[0], n_rows), _rows(sc_ref[0], n_rows)).astype(_MM)
    c, s1, s2 = c_ref[...], s1_ref[...], s2_ref[...]
    off = 0
    for (name, width, rope, scale), o_ref in zip(groups, out_refs):
        y = _dot(h, w_ref[:, off:off + width])
        off += width
        if name == "misc":
            lane = lax.broadcasted_iota(I32, y.shape, 1)
            yr = _rope(y, c, s1, s2)
            lf = _log_sigmoid(y + bf_ref[...])
            in_fb = (lane >= MISC_FB) & (lane < MISC_FB + H_B)
            if cum_t:
                stride = n_rows // cum_t
                tok = lax.broadcasted_iota(I32, y.shape, 0) // stride
                cum = lf
                for k in range(1, cum_t):
                    cum = cum + jnp.where(tok >= k, pltpu.roll(lf, k * stride, 0), 0.0)
                y = jnp.where((lane >= MISC_CQ) & (lane < MISC_CQ + H_B), pltpu.roll(cum, H_B, 1), y)
            y = jnp.where(lane < MISC_WI, yr, jnp.where(in_fb, lf, y))
        elif rope:
            y = _rope(y, c, s1, s2)
        if scale != 1.0:
            y = y * scale
        if name in STATE_GROUPS:
            o_ref[0] = y
        else:
            o_ref[...] = y.astype(o_ref.dtype)


def _pack_w_in(w_in, slots):
    d = w_in.shape[0]

    def cols(name):
        o, w = _OFF[name]
        return w_in[:, o:o + w]

    def slot(name, place):
        src = cols(name).reshape(d, -1, HEAD_DIM)
        zero = jnp.zeros((d, HEAD_DIM), w_in.dtype)
        parts = []
        for hh in range(src.shape[1]):
            parts += [src[:, hh], zero] if place(hh) == 0 else [zero, src[:, hh]]
        return jnp.concatenate(parts, axis=1)

    misc = jnp.concatenate([cols("ki"), cols("wi"), cols("fb"),
                            jnp.zeros((d, LANES - D_IDX - H_IDX - H_B), w_in.dtype)], axis=1)
    if slots:
        qa = slot("qa", lambda hh: hh // (H_A // KV_A))
        qi = slot("qi", lambda hh: 0)
        qb = slot("qb", lambda hh: hh % 2)
    else:
        qa, qi, qb = cols("qa"), cols("qi"), cols("qb")
    parts = [qa, qi, qb, cols("kb"), cols("vb"), cols("ka"), cols("va"), misc]
    groups = (("qa", qa.shape[1], True, Q_SCALE), ("qi", qi.shape[1], True, D_IDX ** -0.5),
              ("qb", qb.shape[1], False, Q_SCALE), ("kb", H_B * HEAD_DIM, False, 1.0),
              ("vb", H_B * HEAD_DIM, False, 1.0), ("ka", KV_A * HEAD_DIM, True, 1.0),
              ("va", KV_A * HEAD_DIM, False, 1.0), ("misc", LANES, False, 1.0))
    return jnp.concatenate(parts, axis=1).astype(_MM), groups


def _inproj(grp, layer, x, norm_g, w_packed, groups, b_f, tables, tm, att, n_att, prev, cum_t=0):
    n, d = x.shape
    nt = n // tm
    sh_arr, sh_spec = grp.mod_arg(layer, tm, 0)
    sc_arr, sc_spec = grp.mod_arg(layer, tm, 1)
    c, s1, s2 = tables
    tr = c.shape[0] // tm
    tspec = BS((tm, LANES), lambda i: (i % tr, 0))
    bf = jnp.zeros((1, LANES), F32).at[0, MISC_FB:MISC_FB + H_B].set(b_f)
    out_shape, out_specs = [], []
    for name, w, _, _ in groups:
        if name in STATE_GROUPS:
            out_shape.append(SDS((n_att, n, w), F32))
            out_specs.append(BS((1, tm, w), lambda i: (att, i, 0)))
        else:
            out_shape.append(SDS((n, w), _MM))
            out_specs.append(BS((tm, w), lambda i: (i, 0)))
    nc = w_packed.shape[1]
    in_specs = [BS((tm, d), lambda i: (i, 0)), BS((1, d), lambda i: (0, 0)), sh_spec, sc_spec,
                BS((d, nc), lambda i: (0, 0)), tspec, tspec, tspec, BS((1, LANES), lambda i: (0, 0))]
    args = [x, norm_g.reshape(1, d), sh_arr, sc_arr, w_packed, c, s1, s2, bf]
    aliases = {}
    if prev is not None:
        state_out = [k for k, g in enumerate(groups) if g[0] in STATE_GROUPS]
        for arr, k in zip(prev, state_out):
            aliases[len(args)] = k
            in_specs.append(BS(memory_space=pl.ANY))
            args.append(arr)
    return pl.pallas_call(
        functools.partial(_inproj_kernel, groups=groups, cum_t=cum_t, n_prev=len(aliases)),
        out_shape=out_shape,
        grid=(nt,),
        in_specs=in_specs,
        out_specs=out_specs,
        input_output_aliases=aliases,
        compiler_params=_cp("arbitrary"),
        name="even_inproj",
    )(*args)


def _tri_lower(n):
    r = lax.broadcasted_iota(I32, (n, n), 0)
    c = lax.broadcasted_iota(I32, (n, n), 1)
    return jnp.where(r >= c, 1.0, 0.0).astype(F32)


def _dot_f32(a, b):
    return jnp.dot(a, b, preferred_element_type=F32, precision=lax.Precision.HIGHEST)


def _cumsum_p_kernel(lf_ref, f_ref, *, tc):
    t = lf_ref.shape[1]
    tri = _tri_lower(tc)
    carry = jnp.zeros((1, lf_ref.shape[2]), F32)
    for c in range(t // tc):
        fc = _dot_f32(tri, lf_ref[0, c * tc:(c + 1) * tc, :]) + carry
        f_ref[0, c * tc:(c + 1) * tc, :] = fc
        carry = fc[tc - 1:tc, :]


def _cumsum_prompt(logf):
    b, t, hb = logf.shape
    tc = min(256, t)
    return pl.pallas_call(
        functools.partial(_cumsum_p_kernel, tc=tc),
        out_shape=SDS((b, t, hb), F32),
        grid=(b,),
        in_specs=[BS((1, t, hb), lambda i: (i, 0, 0))],
        out_specs=BS((1, t, hb), lambda i: (i, 0, 0)),
        compiler_params=_cp("arbitrary"),
        name="cumsum_prompt",
    )(logf)


def _fox_p_kernel(q_ref, k_ref, v_ref, fq_ref, fk_ref, o_ref, m_sc, l_sc, acc_sc, fq_sc, *, tq, tk):
    i = pl.program_id(1)
    j = pl.program_id(2)
    nk = pl.num_programs(2)

    @pl.when(j == 0)
    def _():
        m_sc[...] = jnp.full(m_sc.shape, NEG_INF, F32)
        l_sc[...] = jnp.zeros(l_sc.shape, F32)
        acc_sc[...] = jnp.zeros(acc_sc.shape, F32)
        fq = fq_ref[0]
        for hh in range(H_B):
            fq_sc[hh] = jnp.broadcast_to(fq[:, hh:hh + 1], (tq, LANES))

    def step(masked):
        k = k_ref[0, 0].astype(_MM)
        v = v_ref[0, 0].astype(_MM)
        fk = fk_ref[0]
        nr = min(FOX_ROWS, tq)
        lane = lax.broadcasted_iota(I32, (nr, LANES), 1)
        low = lane < HEAD_DIM
        for rc in range(tq // nr):
            rs = slice(rc * nr, (rc + 1) * nr)
            if masked:
                rows = i * tq + rc * nr + lax.broadcasted_iota(I32, (nr, tk), 0)
                cols = j * tk + lax.broadcasted_iota(I32, (nr, tk), 1)
                causal = cols <= rows
            for p in range(H_B // 2):
                kp = k[:, p * LANES:(p + 1) * LANES]
                vp = v[:, p * LANES:(p + 1) * LANES]
                alphas, pvs = [], []
                for e in range(2):
                    hh = 2 * p + e
                    z = _dot_nt(q_ref[0, rs, hh * LANES:(hh + 1) * LANES], kp) - fk[hh:hh + 1, :]
                    if masked:
                        z = jnp.where(causal, z, NEG_INF)
                    fq = fq_sc[hh, rs, :]
                    m_prev = m_sc[hh, rs, :]
                    m_new = jnp.maximum(m_prev, fq + jnp.max(z, axis=1, keepdims=True))
                    alpha = jnp.exp(m_prev - m_new)
                    pe = jnp.exp(z + jnp.concatenate([fq - m_new] * (tk // LANES), axis=1))
                    l_sc[hh, rs, :] = alpha * l_sc[hh, rs, :] + jnp.sum(pe, axis=1, keepdims=True)
                    m_sc[hh, rs, :] = m_new
                    alphas.append(alpha)
                    pvs.append(_dot(pe.astype(_MM), vp))
                acc_sc[p, rs, :] = (jnp.where(low, alphas[0], alphas[1]) * acc_sc[p, rs, :]
                                    + jnp.where(low, pvs[0], pvs[1]))

    visible = (j + 1) * tk <= i * tq + 1

    @pl.when(visible)
    def _():
        step(False)

    @pl.when(jnp.logical_not(visible) & (j * tk < (i + 1) * tq))
    def _():
        step(True)

    @pl.when(j == nk - 1)
    def _():
        lane = lax.broadcasted_iota(I32, (tq, LANES), 1)
        low = lane < HEAD_DIM
        for p in range(H_B // 2):
            linv = jnp.where(low, 1.0 / l_sc[2 * p], 1.0 / l_sc[2 * p + 1])
            o_ref[0, :, p * LANES:(p + 1) * LANES] = (acc_sc[p] * linv).astype(o_ref.dtype)


def _fox_prompt(qb_s, kb, vb, f, ft, b, t, att):
    tq, tk = min(FOX_TQ, t), min(FOX_TK, t)
    nq, nk = t // tq, t // tk
    dq = qb_s.shape[1]
    n_att, _, dk = kb.shape
    kmap = lambda bb, i, j: (att, bb, jnp.minimum(j, ((i + 1) * tq - 1) // tk), 0)
    return pl.pallas_call(
        functools.partial(_fox_p_kernel, tq=tq, tk=tk),
        out_shape=SDS((b, t, dk), _MM),
        grid=(b, nq, nk),
        in_specs=[BS((1, tq, dq), lambda bb, i, j: (bb, i, 0)),
                  BS((1, 1, tk, dk), kmap), BS((1, 1, tk, dk), kmap),
                  BS((1, tq, H_B), lambda bb, i, j: (bb, i, 0)),
                  BS((1, H_B, tk), lambda bb, i, j: (bb, 0, jnp.minimum(j, ((i + 1) * tq - 1) // tk)))],
        out_specs=BS((1, tq, dk), lambda bb, i, j: (bb, i, 0)),
        scratch_shapes=[pltpu.VMEM((H_B, tq, LANES), F32), pltpu.VMEM((H_B, tq, LANES), F32),
                        pltpu.VMEM((H_B // 2, tq, LANES), F32), pltpu.VMEM((H_B, tq, LANES), F32)],
        compiler_params=_cp("arbitrary", "arbitrary", "arbitrary"),
        name="fox_prompt",
    )(qb_s.reshape(b, t, dq), kb.reshape(n_att, b, t, dk), vb.reshape(n_att, b, t, dk), f, ft)


def _order_key(score):
    bits = pltpu.bitcast(score + 0.0, I32)
    return jnp.where(bits < 0, bits ^ jnp.int32(0x7FFFFFFF), bits)


def _kth_largest(load, nchunks, k_f, bits_per_pass=1):
    rows, width = load(0).shape
    rc = min(rows, COUNT_ROWS)

    def count_ge(cand):
        accs = []
        for r0 in range(0, rows, rc):
            acc = None
            for c in range(nchunks):
                x = jnp.where(load(c, r0, rc) >= cand[r0:r0 + rc], 1.0, 0.0)
                for j in range(width // LANES):
                    piece = x[:, j * LANES:(j + 1) * LANES]
                    acc = piece if acc is None else acc + piece
            accs.append(acc)
        folded = accs[0] if len(accs) == 1 else jnp.concatenate(accs, axis=0)
        return jnp.sum(folded, axis=1, keepdims=True)

    def body(it, tau):
        shift = 32 - bits_per_pass * (it + 1)
        best = tau
        for digit in range(1, 2 ** bits_per_pass):
            cand = tau ^ jnp.left_shift(jnp.int32(digit), shift)
            best = jnp.where(count_ge(cand) >= k_f, cand, best)
        return best

    tau = lax.fori_loop(0, 32 // bits_per_pass, body, jnp.full((rows, 1), INT_MIN, I32))
    return tau, count_ge(tau)


def _select_with_ties(load, store, nchunks, width, tau, k_f):
    n_gt = None
    for c in range(nchunks):
        x = jnp.sum(jnp.where(load(c) > tau, 1.0, 0.0), axis=1, keepdims=True)
        n_gt = x if n_gt is None else n_gt + x
    need = k_f - n_gt
    r = lax.broadcasted_iota(I32, (width, width), 0)
    cc = lax.broadcasted_iota(I32, (width, width), 1)
    upper = jnp.where(r <= cc, 1.0, 0.0).astype(_MM)
    carry = jnp.zeros_like(need)
    for c in range(nchunks):
        key = load(c)
        eq = key == tau
        prefix = _dot(jnp.where(eq, 1.0, 0.0).astype(_MM), upper) + carry
        store(c, (key > tau) | (eq & (prefix <= need)))
        carry = prefix[:, width - 1:width]


def _dsa_p_kernel(qi_ref, qa_ref, mq_ref, mk_ref, ka_ref, va_ref, *refs, tq, lk, q0, k_top, cw, n_prev):
    o_ref, key_sc, bias_sc = refs[n_prev:]
    i = pl.program_id(1)
    nch = lk // cw
    qpos = q0 + i * tq + lax.broadcasted_iota(I32, (tq, 1), 0)
    wi = mq_ref[0, 0][:, MISC_WI:MISC_WI + H_IDX] * (H_IDX ** -0.5)
    for c in range(nch):
        kmat = mk_ref[0, 0, c * cw:(c + 1) * cw, :].astype(_MM)
        score = jnp.zeros((tq, cw), F32)
        for hh in range(H_IDX):
            s = _dot_nt(qi_ref[0, :, hh * LANES:(hh + 1) * LANES], kmat)
            score = score + jnp.maximum(s, 0.0) * wi[:, hh:hh + 1]
        kpos = c * cw + lax.broadcasted_iota(I32, (tq, cw), 1)
        key_sc[:, c * cw:(c + 1) * cw] = jnp.where(kpos <= qpos, _order_key(score), INT_MIN)

    load = lambda c, r0=0, nr=tq: key_sc[r0:r0 + nr, c * cw:(c + 1) * cw]
    k_f = jnp.minimum(k_top, qpos + 1).astype(F32)
    tau, cnt = _kth_largest(load, nch, k_f)
    has_tie = jnp.max(cnt - k_f) > 0.0

    def store(c, sel):
        bias_sc[:, c * cw:(c + 1) * cw] = jnp.where(sel, 0.0, NEG_INF)

    @pl.when(jnp.logical_not(has_tie))
    def _():
        for c in range(nch):
            store(c, load(c) >= tau)

    @pl.when(has_tie)
    def _():
        _select_with_ties(load, store, nch, cw, tau, k_f)

    ka = ka_ref[0, 0].astype(_MM)
    va = va_ref[0, 0].astype(_MM)
    bias = bias_sc[...]
    lane = lax.broadcasted_iota(I32, (tq, LANES), 1)
    for hh in range(H_A):
        grp = hh // (H_A // KV_A)
        lg = _dot_nt(qa_ref[0, :, hh * LANES:(hh + 1) * LANES], ka) + bias
        m = jnp.max(lg, axis=1, keepdims=True)
        pe = jnp.exp(lg - m)
        l = jnp.sum(pe, axis=1, keepdims=True)
        o = _dot(pe.astype(_MM), va) * (1.0 / l)
        o = jnp.where((lane >= grp * HEAD_DIM) & (lane < (grp + 1) * HEAD_DIM), o, 0.0)
        o_ref[0, :, hh * LANES:(hh + 1) * LANES] = o.astype(o_ref.dtype)


def _dsa_prompt(qi_s, qa_s, misc, ka, va, b, t, k_top, att):
    bands = min(DSA_BANDS, t // LANES)
    band = t // bands
    tq = min(DSA_TQ, band)
    cw = min(DSA_CHUNK, band)
    dq = qi_s.shape[1]
    n_att = misc.shape[0]
    qi3, qa3 = qi_s.reshape(b, t, dq), qa_s.reshape(b, t, dq)
    misc4, ka4, va4 = (a.reshape(n_att, b, t, LANES) for a in (misc, ka, va))
    kmap = lambda bb, i: (att, bb, 0, 0)
    out = None
    for c in range(bands):
        lk = (c + 1) * band
        q_first = c * (band // tq)
        qmap = lambda bb, i, q_first=q_first: (bb, q_first + i, 0)
        qmap4 = lambda bb, i, q_first=q_first: (att, bb, q_first + i, 0)
        in_specs = [BS((1, tq, dq), qmap), BS((1, tq, dq), qmap), BS((1, 1, tq, LANES), qmap4),
                    BS((1, 1, lk, LANES), kmap), BS((1, 1, lk, LANES), kmap), BS((1, 1, lk, LANES), kmap)]
        args = [qi3, qa3, misc4, misc4, ka4, va4]
        aliases = {}
        if out is not None:
            aliases = {len(args): 0}
            in_specs.append(BS(memory_space=pl.ANY))
            args.append(out)
        out = pl.pallas_call(
            functools.partial(_dsa_p_kernel, tq=tq, lk=lk, q0=c * band, k_top=k_top, cw=cw, n_prev=len(aliases)),
            out_shape=SDS((b, t, dq), _MM),
            grid=(b, band // tq),
            in_specs=in_specs,
            out_specs=BS((1, tq, dq), qmap),
            scratch_shapes=[pltpu.VMEM((tq, lk), I32), pltpu.VMEM((tq, lk), F32)],
            input_output_aliases=aliases,
            compiler_params=_cp("arbitrary", "arbitrary"),
            name="dsa_prompt",
        )(*args)
    return out


def _outproj_kernel(oa_ref, ob_ref, wa_ref, wb_ref, x_ref, gate_ref, o_ref):
    y = _dot(oa_ref[...], wa_ref[...]) + _dot(ob_ref[...], wb_ref[...])
    o_ref[...] = x_ref[...] + _rows(gate_ref[0], y.shape[0]) * y


def _outproj(grp, layer, oa, ob, wa, wb, x, tm):
    n, d = x.shape
    g_arr, g_spec = grp.mod_arg(layer, tm, 2)
    da, db = oa.shape[1], ob.shape[1]
    return pl.pallas_call(
        _outproj_kernel,
        out_shape=SDS((n, d), F32),
        grid=(n // tm,),
        in_specs=[BS((tm, da), lambda i: (i, 0)), BS((tm, db), lambda i: (i, 0)),
                  BS((da, d), lambda i: (0, 0)), BS((db, d), lambda i: (0, 0)),
                  BS((tm, d), lambda i: (i, 0)), g_spec],
        out_specs=BS((tm, d), lambda i: (i, 0)),
        compiler_params=_cp("arbitrary"),
        name="even_outproj",
    )(oa, ob, wa, wb, x, g_arr)


def _ffn_kernel(x_ref, g_ref, sh_ref, sc_ref, gate_ref, wi_ref, wo_ref, gf_ref, o_ref, *, final, tf):
    ff = wo_ref.shape[1]
    x = x_ref[...]
    n_rows = x.shape[0]
    h = _modulate(x, g_ref[...], _rows(sh_ref[0], n_rows), _rows(sc_ref[0], n_rows)).astype(_MM)
    acc = None
    for c in range(ff // tf):
        gate_part = _dot(h, wi_ref[0, :, c * tf:(c + 1) * tf])
        up_part = _dot(h, wi_ref[0, :, ff + c * tf:ff + (c + 1) * tf])
        a = (_silu(gate_part) * up_part).astype(_MM)
        part = _dot(a, wo_ref[0, c * tf:(c + 1) * tf, :])
        acc = part if acc is None else acc + part
    y = x + _rows(gate_ref[0], n_rows) * acc
    if final:
        y = _rms(y, gf_ref[...])
    o_ref[...] = y


def _ffn(grp, layer, x, norm_g, w_in, w_out, norm_final, final, tm, tf):
    n, d = x.shape
    ff = w_out.shape[1]
    sh_arr, sh_spec = grp.mod_arg(layer, tm, 3)
    sc_arr, sc_spec = grp.mod_arg(layer, tm, 4)
    g_arr, g_spec = grp.mod_arg(layer, tm, 5)
    resident = lambda shape: BS((1,) + shape, lambda i: (layer, 0, 0), pipeline_mode=pl.Buffered(1))
    return pl.pallas_call(
        functools.partial(_ffn_kernel, final=final, tf=tf),
        out_shape=SDS((n, d), F32),
        grid=(n // tm,),
        in_specs=[BS((tm, d), lambda i: (i, 0)), BS((1, d), lambda i: (0, 0)), sh_spec, sc_spec, g_spec,
                  resident((d, 2 * ff)), resident((ff, d)), BS((1, d), lambda i: (0, 0))],
        out_specs=BS((tm, d), lambda i: (i, 0)),
        compiler_params=_cp("arbitrary"),
        name="ffn",
    )(x, norm_g.reshape(1, d), sh_arr, sc_arr, g_arr, w_in, w_out, norm_final.reshape(1, d))


def _pw1_kernel(x_ref, g_ref, sh_ref, sc_ref, w_ref, b_ref, u_ref):
    x = x_ref[...]
    n_rows = x.shape[0]
    h = _modulate(x, g_ref[...], _rows(sh_ref[0], n_rows), _rows(sc_ref[0], n_rows)).astype(_MM)
    y = _dot(h, w_ref[...]) + b_ref[...]
    dc = y.shape[1] // 2
    u_ref[...] = y[:, :dc] * _sigmoid(y[:, dc:])


def _pw1(grp, layer, x, norm_g, w, bias, tm):
    n, d = x.shape
    dc2 = w.shape[1]
    sh_arr, sh_spec = grp.mod_arg(layer, tm, 0)
    sc_arr, sc_spec = grp.mod_arg(layer, tm, 1)
    return pl.pallas_call(
        _pw1_kernel,
        out_shape=SDS((n, dc2 // 2), F32),
        grid=(n // tm,),
        in_specs=[BS((tm, d), lambda i: (i, 0)), BS((1, d), lambda i: (0, 0)), sh_spec, sc_spec,
                  BS((d, dc2), lambda i: (0, 0)), BS((1, dc2), lambda i: (0, 0))],
        out_specs=BS((tm, dc2 // 2), lambda i: (i, 0)),
        compiler_params=_cp("arbitrary"),
        name="conv_pw1_glu",
    )(x, norm_g.reshape(1, d), sh_arr, sc_arr, w, bias.reshape(1, dc2))


def _ln_swish_pw2(z, lng, lnb, w2, b2):
    mu = jnp.mean(z, axis=-1, keepdims=True)
    zc = z - mu
    var = jnp.mean(zc * zc, axis=-1, keepdims=True)
    zn = zc * lax.rsqrt(var + EPS) * lng + lnb
    return _dot(_silu(zn).astype(_MM), w2) + b2


HALO = 32


SUBLANES = 8
CONV_ROWS = 32


def _conv_p_kernel(ucur_ref, uhalo_ref, wdw_ref, bdw_ref, lng_ref, lnb_ref, w2_ref, b2_ref, x_ref, gate_ref,
                   o_ref, full_sc, shift_sc, z_sc, *, tm):
    i = pl.program_id(1)
    dc = full_sc.shape[1]
    full_sc[0:HALO, :] = jnp.where(i > 0, uhalo_ref[0], 0.0)
    full_sc[HALO:HALO + tm, :] = ucur_ref[0]
    span = shift_sc.shape[1]
    for q in range(1, SUBLANES):
        shift_sc[q - 1] = full_sc[q:q + span, :]
    base = HALO - (CONV_WIDTH - 1)

    def chunk(c, carry):
        r0 = pl.multiple_of(c * CONV_ROWS, CONV_ROWS)
        groups = CONV_ROWS // SUBLANES
        z = jnp.zeros((groups, SUBLANES, dc), F32) + bdw_ref[...]
        for w in range(CONV_WIDTH):
            q, al = (base + w) % SUBLANES, ((base + w) // SUBLANES) * SUBLANES
            src = full_sc if q == 0 else shift_sc.at[q - 1]
            slab = src[pl.ds(r0 + al, CONV_ROWS), :].reshape(groups, SUBLANES, dc)
            z = z + slab * wdw_ref[w][None]
        z_sc[pl.ds(r0, CONV_ROWS), :] = z.reshape(CONV_ROWS, dc)
        return carry

    lax.fori_loop(0, tm // CONV_ROWS, chunk, 0)
    y = _ln_swish_pw2(z_sc[...], lng_ref[...], lnb_ref[...], w2_ref[...], b2_ref[...])
    o_ref[0] = x_ref[0] + gate_ref[0] * y


def _conv_prompt(grp, layer, u, x, wdw, bdw, lng, lnb, w2, b2, tm):
    b, t, d = grp.nb, grp.t, grp.d
    dc = u.shape[1]
    g_arr, _ = grp.mod_arg(layer, tm, 2)
    hb = tm // HALO
    wrep = jnp.broadcast_to(wdw[:, None, :], (CONV_WIDTH, SUBLANES, dc))
    vec = lambda bb, i: (0, 0)
    return pl.pallas_call(
        functools.partial(_conv_p_kernel, tm=tm),
        out_shape=SDS((b, t, d), F32),
        grid=(b, t // tm),
        in_specs=[BS((1, tm, dc), lambda bb, i: (bb, i, 0)),
                  BS((1, HALO, dc), lambda bb, i: (bb, jnp.maximum(i * hb - 1, 0), 0)),
                  BS((CONV_WIDTH, SUBLANES, dc), lambda bb, i: (0, 0, 0)),
                  BS((1, dc), vec), BS((1, dc), vec), BS((1, dc), vec),
                  BS((dc, d), vec), BS((1, d), vec),
                  BS((1, tm, d), lambda bb, i: (bb, i, 0)),
                  BS((1, 1, d), lambda bb, i: (bb, 0, 2))],
        out_specs=BS((1, tm, d), lambda bb, i: (bb, i, 0)),
        scratch_shapes=[pltpu.VMEM((HALO + tm, dc), F32),
                        pltpu.VMEM((SUBLANES - 1, HALO + tm - SUBLANES, dc), F32),
                        pltpu.VMEM((tm, dc), F32)],
        compiler_params=_cp("arbitrary", "arbitrary"),
        name="conv_prompt",
    )(u.reshape(b, t, dc), u.reshape(b, t, dc), wrep, bdw.reshape(1, dc), lng.reshape(1, dc), lnb.reshape(1, dc),
      w2, b2.reshape(1, d), x.reshape(b, t, d), g_arr).reshape(b * t, d)


def _conv_s_kernel(buf_ref, u_ref, wdw_ref, bdw_ref, lng_ref, lnb_ref, w2_ref, b2_ref, x_ref, gate_ref, o_ref):
    nbuf = buf_ref.shape[0]
    t = u_ref.shape[0]
    zs = []
    for tt in range(t):
        z = jnp.zeros(u_ref.shape[1:], F32) + bdw_ref[...]
        for w in range(CONV_WIDTH):
            src = tt + w
            row = buf_ref[src] if src < nbuf else u_ref[src - nbuf]
            z = z + row * wdw_ref[w:w + 1, :]
        zs.append(z)
    y = _ln_swish_pw2(jnp.concatenate(zs, axis=0), lng_ref[...], lnb_ref[...], w2_ref[...], b2_ref[...])
    bb = u_ref.shape[1]
    for tt in range(t):
        o_ref[tt] = x_ref[tt] + gate_ref[...] * y[tt * bb:(tt + 1) * bb, :]


def _conv_sample(buf_t, u_t, x_t, gate, wdw, bdw, lng, lnb, w2, b2):
    nbuf, db, dc = buf_t.shape
    t, _, d = x_t.shape
    bb = min(32, db)
    wpad = jnp.zeros((HALO, dc), F32).at[:CONV_WIDTH].set(wdw)
    vec = lambda j: (0, 0)
    return pl.pallas_call(
        _conv_s_kernel,
        out_shape=SDS((t, db, d), F32),
        grid=(db // bb,),
        in_specs=[BS((nbuf, bb, dc), lambda j: (0, j, 0)), BS((t, bb, dc), lambda j: (0, j, 0)),
                  BS((HALO, dc), vec), BS((1, dc), vec), BS((1, dc), vec), BS((1, dc), vec),
                  BS((dc, d), vec), BS((1, d), vec),
                  BS((t, bb, d), lambda j: (0, j, 0)), BS((bb, d), lambda j: (j, 2))],
        out_specs=BS((t, bb, d), lambda j: (0, j, 0)),
        compiler_params=_cp("arbitrary"),
        name="conv_sample",
    )(buf_t, u_t, wpad, bdw.reshape(1, dc), lng.reshape(1, dc), lnb.reshape(1, dc), w2, b2.reshape(1, d), x_t, gate)


def _per_head_rows(x, t):
    nh, w = x.shape
    row = lax.broadcasted_iota(I32, (nh * t, w), 0) // t
    out = jnp.zeros((nh * t, w), x.dtype)
    for hh in range(nh):
        out = jnp.where(row == hh, x[hh:hh + 1, :], out)
    return out


def _strict_lower(n):
    r = lax.broadcasted_iota(I32, (n, n), 0)
    c = lax.broadcasted_iota(I32, (n, n), 1)
    return jnp.where(r > c, 1.0, 0.0).astype(F32)


def _attn_s_kernel(pt_ref, qb_ref, qa_ref, qi_ref, wi_ref, cq_ref, cqt_ref, kbn_ref, vbn_ref, kan_ref, van_ref,
                   kin_ref, *rest, past, k_top, t, gp, nb):
    npage = 6 * gp * nb
    page_refs = rest[:npage]
    oa_ref, ob_ref = rest[npage:npage + 2]
    (m_sc, l_sc, acc_sc, r_sc, ka_sc, va_sc, key_sc, sel_sc,
     kbn_sc, vbn_sc, kan_sc, van_sc, kin_sc) = rest[npage + 2:]
    first = (pl.program_id(0) == 0) & (pl.program_id(1) == 0)
    s = pl.program_id(1)
    ns = ka_sc.shape[1]
    ps = kin_sc.shape[1]
    w = gp * ps
    rows = H_B * t
    nt8 = kbn_ref.shape[1]
    row_tok = lax.broadcasted_iota(I32, (rows, ps), 0) % t
    lane_r = lax.broadcasted_iota(I32, (rows, ps), 1)
    row8_tok = lax.broadcasted_iota(I32, (8, ps), 0) % t
    lane8 = lax.broadcasted_iota(I32, (8, ps), 1)
    chunk = ns - 1 - s

    def dup_scores(s32, bi):
        s32 = jnp.maximum(s32, 0.0) * (wi_ref[bi] * (H_IDX ** -0.5))
        sc = s32[0:t]
        for hh in range(1, H_IDX):
            sc = sc + s32[hh * t:(hh + 1) * t]
        return jnp.concatenate([sc] * (8 // t), axis=0)

    def key_rows(bi):
        return slice(8 * bi, 8 * (bi + 1))

    @pl.when(first)
    def _():
        for ref in (kbn_sc, vbn_sc, kan_sc, van_sc, kin_sc):
            ref[...] = jnp.zeros(ref.shape, F32)

    def new_rows(bi):
        kbn_sc[bi, 0:nt8, :] = kbn_ref[bi]
        vbn_sc[bi, 0:nt8, :] = vbn_ref[bi]
        kan_sc[bi, 0:nt8, :] = kan_ref[bi]
        van_sc[bi, 0:nt8, :] = van_ref[bi]
        kin_sc[bi, 0:nt8, :] = kin_ref[bi]
        r_sc[bi] = jnp.zeros(r_sc.shape[1:], F32)
        lg = _dot_nt(qb_ref[bi], kbn_sc[bi].astype(_MM)) + cq_ref[bi] - _per_head_rows(cqt_ref[bi], t)
        lg = jnp.where(lane_r <= row_tok, lg, NEG_INF)
        m = jnp.max(lg, axis=1, keepdims=True)
        pe = jnp.exp(lg - m)
        m_sc[bi] = m
        l_sc[bi] = jnp.sum(pe, axis=1, keepdims=True)
        acc_sc[bi] = _dot(pe.astype(_MM), vbn_sc[bi].astype(_MM))
        key_new = _order_key(dup_scores(_dot_nt(qi_ref[bi], kin_sc[bi].astype(_MM)), bi))
        key_new = jnp.where(lane8 <= row8_tok, key_new, INT_MIN)
        if gp > 1:
            key_new = jnp.concatenate([key_new, jnp.full((8, w - ps), INT_MIN, I32)], axis=1)
        key_sc[ns, key_rows(bi), :] = key_new

    @pl.when(s == 0)
    def _():
        for bi in range(nb):
            new_rows(bi)

    def pages(bi):
        refs = [page_refs[6 * (bi * gp + g):6 * (bi * gp + g) + 6] for g in range(gp)]
        cat = lambda parts: jnp.concatenate(parts, axis=1) if gp > 1 else parts[0]
        kt = cat([r[0][0, 0].reshape(H_B * HEAD_DIM, ps).astype(_MM) for r in refs])
        vt = cat([r[1][0, 0].reshape(H_B * HEAD_DIM, ps).astype(_MM) for r in refs])
        ka_sc[bi, chunk] = cat([r[2][0, 0].reshape(KV_A * HEAD_DIM, ps).astype(_MM) for r in refs])
        va_sc[bi, chunk] = cat([r[3][0, 0].reshape(KV_A * HEAD_DIM, ps).astype(_MM) for r in refs])
        kit = cat([r[4][0, 0].astype(_MM) for r in refs])
        lfts = [r[5][0, 0].astype(F32) for r in refs]

        suf_loc = _dot_f32(jnp.concatenate(lfts, axis=0), _strict_lower(ps))
        carry = r_sc[bi]
        sufs = [None] * gp
        for g in reversed(range(gp)):
            loc = suf_loc[g * H_B:(g + 1) * H_B]
            sufs[g] = loc + carry
            carry = carry + loc[:, 0:1] + lfts[g][:, 0:1]
        r_sc[bi] = carry

        lg = _dot(qb_ref[bi], kt) + (_per_head_rows(cat(sufs), t) + cq_ref[bi])
        m_prev = m_sc[bi]
        m_new = jnp.maximum(m_prev, jnp.max(lg, axis=1, keepdims=True))
        alpha = jnp.exp(m_prev - m_new)
        pe = jnp.exp(lg - m_new)
        l_sc[bi] = alpha * l_sc[bi] + jnp.sum(pe, axis=1, keepdims=True)
        acc_sc[bi] = alpha * acc_sc[bi] + _dot_nt(pe.astype(_MM), vt)
        m_sc[bi] = m_new
        key_sc[chunk, key_rows(bi), :] = _order_key(dup_scores(_dot(qi_ref[bi], kit), bi))

    for bi in range(nb):
        pages(bi)

    def fox_out(bi):
        o = acc_sc[bi] * (1.0 / l_sc[bi])
        lane_h = lax.broadcasted_iota(I32, o.shape, 1) // HEAD_DIM
        row_h = lax.broadcasted_iota(I32, o.shape, 0) // t
        o = jnp.where(lane_h == row_h, o, 0.0)
        ob = o[0:t]
        for hh in range(1, H_B):
            ob = ob + o[hh * t:(hh + 1) * t]
        ob_ref[bi] = ob

    def dsa_out(bi):
        nch = ns + 1
        qa = qa_ref[bi]
        lgs = []
        for c in range(nch):
            bias = sel_sc[c, key_rows(bi), :]
            if c < ns:
                prod = _dot(qa, ka_sc[bi, c])
            else:
                prod, bias = _dot_nt(qa, kan_sc[bi].astype(_MM)), bias[:, 0:ps]
            lgs.append(prod + jnp.concatenate([bias] * (rows // 8), axis=0))
        m = jnp.max(lgs[0], axis=1, keepdims=True)
        for c in range(1, nch):
            m = jnp.maximum(m, jnp.max(lgs[c], axis=1, keepdims=True))
        lsum = jnp.zeros((rows, 1), F32)
        out = jnp.zeros((rows, KV_A * HEAD_DIM), F32)
        for c in range(nch):
            pe = jnp.exp(lgs[c] - m)
            lsum = lsum + jnp.sum(pe, axis=1, keepdims=True)
            if c < ns:
                out = out + _dot_nt(pe.astype(_MM), va_sc[bi, c])
            else:
                out = out + _dot(pe.astype(_MM), van_sc[bi].astype(_MM))
        out = out * (1.0 / lsum)
        lane_j = lax.broadcasted_iota(I32, (t, KV_A * HEAD_DIM), 1) // HEAD_DIM
        for hh in range(H_A):
            piece = out[hh * t:(hh + 1) * t, :]
            oa_ref[bi, :, hh * LANES:(hh + 1) * LANES] = jnp.where(lane_j == hh // (H_A // KV_A), piece, 0.0)

    @pl.when(s == ns - 1)
    def _():
        for bi in range(nb):
            fox_out(bi)
        nch = ns + 1
        qpos = past + lax.broadcasted_iota(I32, (8 * nb, 1), 0) % t
        k_f = jnp.minimum(k_top, qpos + 1).astype(F32)
        load = lambda c, r0=0, nr=8 * nb: key_sc[c, r0:r0 + nr, :]
        tau, cnt = _kth_largest(load, nch, k_f, bits_per_pass=2)
        has_tie = jnp.max(cnt - k_f) > 0.0

        def store(c, sel):
            sel_sc[c] = jnp.where(sel, 0.0, NEG_INF)

        @pl.when(jnp.logical_not(has_tie))
        def _():
            for c in range(nch):
                store(c, load(c) >= tau)

        @pl.when(has_tie)
        def _():
            _select_with_ties(load, store, nch, w, tau, k_f)

        for bi in range(nb):
            dsa_out(bi)


def _attn_sample(layer, page_table, caches_t, qbd, qabd, qi32, wi32, cq32, cqt, kbn, vbn, kan, van, kin, k_top, t):
    kb_t, vb_t, ka_t, va_t, ki_t, lf_t = caches_t
    db, npg = page_table.shape
    ps = ki_t.shape[-1]
    rows = H_B * t
    gp = PAGES_PER_STEP if npg % PAGES_PER_STEP == 0 else 1
    nb = BATCHES_PER_STEP if db % BATCHES_PER_STEP == 0 else 1
    ns = npg // gp
    w = gp * ps
    const3 = lambda b, s, pt: (b, 0, 0)
    full = lambda a: BS((nb,) + a.shape[1:], const3)

    def page_specs(bi, g):
        def pg(b, s, pt):
            return pt[b * nb + bi, npg - (s + 1) * gp + g]
        five = lambda nh: BS((1, 1, nh, HEAD_DIM, ps), lambda b, s, pt: (layer, pg(b, s, pt), 0, 0, 0))
        return [five(H_B), five(H_B), five(KV_A), five(KV_A),
                BS((1, 1, D_IDX, ps), lambda b, s, pt: (layer, pg(b, s, pt), 0, 0)),
                BS((1, 1, H_B, ps), lambda b, s, pt: (layer, pg(b, s, pt), 0, 0))]

    small = [qbd, qabd, qi32, wi32, cq32, cqt, kbn, vbn, kan, van, kin]
    in_specs = [full(a) for a in small]
    pages = []
    for bi in range(nb):
        for g in range(gp):
            in_specs += page_specs(bi, g)
            pages += [kb_t, vb_t, ka_t, va_t, ki_t, lf_t]
    da = H_A * LANES
    db_ = H_B * HEAD_DIM
    dj = KV_A * HEAD_DIM
    return pl.pallas_call(
        functools.partial(_attn_s_kernel, past=npg * ps, k_top=k_top, t=t, gp=gp, nb=nb),
        out_shape=[SDS((db, t, da), F32), SDS((db, t, db_), F32)],
        grid_spec=pltpu.PrefetchScalarGridSpec(
            num_scalar_prefetch=1,
            grid=(db // nb, ns),
            in_specs=in_specs,
            out_specs=[BS((nb, t, da), const3), BS((nb, t, db_), const3)],
            scratch_shapes=[pltpu.VMEM((nb, rows, 1), F32), pltpu.VMEM((nb, rows, 1), F32),
                            pltpu.VMEM((nb, rows, db_), F32), pltpu.VMEM((nb, H_B, 1), F32),
                            pltpu.VMEM((nb, ns, dj, w), _MM), pltpu.VMEM((nb, ns, dj, w), _MM),
                            pltpu.VMEM((ns + 1, 8 * nb, w), I32), pltpu.VMEM((ns + 1, 8 * nb, w), F32),
                            pltpu.VMEM((nb, ps, db_), F32), pltpu.VMEM((nb, ps, db_), F32),
                            pltpu.VMEM((nb, ps, dj), F32), pltpu.VMEM((nb, ps, dj), F32),
                            pltpu.VMEM((nb, ps, D_IDX), F32)]),
        compiler_params=_cp("arbitrary", "arbitrary"),
        name="attn_sample",
    )(page_table, *small, *pages)


def _pos_minor_kernel(x_ref, o_ref):
    o_ref[0, 0] = x_ref[0].T


def _pos_minor(a, b, t):
    n_att, _, w = a.shape
    tm = _row_tile(t, FFN_ROW_TILE)
    tpb = t // tm
    return pl.pallas_call(
        _pos_minor_kernel,
        out_shape=SDS((n_att, b, w, t), a.dtype),
        grid=(n_att, b, tpb),
        in_specs=[BS((1, tm, w), lambda l, bb, i: (l, bb * tpb + i, 0))],
        out_specs=BS((1, 1, w, tm), lambda l, bb, i: (l, bb, 0, i)),
        compiler_params=_cp("arbitrary", "arbitrary", "arbitrary"),
        name="rows_to_pos_minor",
    )(a)


def _row_tile(n, pref):
    tm = min(pref, n)
    assert n % tm == 0, (n, tm)
    return tm


def _even_prompt(grp, layer, x, w, st):
    b, t = grp.nb, grp.t
    tm = _row_tile(t, ROW_TILE)
    att = st["att_layer"]
    outs = _inproj(grp, layer, x, w["norm_mix"], w["w_in_slots"], w["groups_slots"], w["b_f"], st["rope"], tm,
                   att, st["n_att"], st["att_prev"])
    qa_s, qi_s, qb_s, kb, vb, ka, va, misc = outs
    f = _cumsum_prompt(misc[att][:, MISC_FB:MISC_FB + H_B].reshape(b, t, H_B))
    o_b = _fox_prompt(qb_s, kb, vb, f, jnp.swapaxes(f, 1, 2), b, t, att)
    o_a = _dsa_prompt(qi_s, qa_s, misc, ka, va, b, t, min(TOPK_MAX, t // 4), att)
    x = _outproj(grp, layer, o_a.reshape(b * t, -1), o_b.reshape(b * t, -1), w["w_out_a_slots"], w["w_out_b"], x,
                 _row_tile(grp.n, FFN_ROW_TILE))
    return x, (kb, vb, ka, va, misc)


def _even_sample(grp, layer, x, w, st):
    db, t = grp.nb, grp.t
    n = grp.n
    assert 8 % t == 0, t
    att = st["att_layer"]
    outs = _inproj(grp, layer, x, w["norm_mix"], w["w_in_plain"], w["groups_plain"], w["b_f"], st["rope"], n,
                   att, st["n_att"], st["att_prev"], cum_t=t)
    qa, qi, qb = outs[:3]
    states = outs[3:]
    kb, vb, ka, va, misc = (a[att] for a in states)
    ki, wi = misc[:, :D_IDX], misc[:, MISC_WI:MISC_WI + H_IDX]
    cq = misc[:, MISC_CQ:MISC_CQ + H_B]
    page_table = st["page_table"]
    npg = page_table.shape[1]
    ps = st["caches_t"][0].shape[-1]

    def batch_major(a):
        return a.reshape(t, db, -1).transpose(1, 0, 2)

    def head_major(a, nh):
        return a.reshape(t, db, nh, -1).transpose(1, 2, 0, 3)

    eye_b = jnp.eye(H_B, dtype=qb.dtype)
    qbd = (head_major(qb, H_B)[:, :, :, None, :] * eye_b[None, :, None, :, None]).reshape(db, H_B * t, -1)
    grp_hot = (jnp.arange(H_A)[:, None] // (H_A // KV_A) == jnp.arange(KV_A)[None, :]).astype(qa.dtype)
    qabd = (head_major(qa, H_A)[:, :, :, None, :] * grp_hot[None, :, None, :, None]).reshape(db, H_A * t, -1)
    qi32 = head_major(qi, H_IDX).reshape(db, H_IDX * t, D_IDX)
    wi32 = head_major(wi, H_IDX).reshape(db, H_IDX * t, 1)
    cq32 = head_major(cq, H_B).reshape(db, H_B * t, 1)
    cqt = jnp.pad(batch_major(cq).transpose(0, 2, 1), ((0, 0), (0, 0), (0, ps - t)))
    pad8 = lambda a: jnp.pad(batch_major(a), ((0, 0), (0, 8 - t), (0, 0)))
    o_a, o_b = _attn_sample(att, page_table, st["caches_t"], qbd, qabd, qi32, wi32, cq32, cqt,
                            pad8(kb), pad8(vb), pad8(ka), pad8(va), pad8(ki),
                            min(TOPK_MAX, (npg * ps + t) // 4), t)
    token_major = lambda o: o.transpose(1, 0, 2).reshape(n, -1).astype(_MM)
    x = _outproj(grp, layer, token_major(o_a), token_major(o_b), w["w_out_a_slots"], w["w_out_b"], x,
                 _row_tile(n, FFN_ROW_TILE))
    return x, states


def _odd_prompt(grp, layer, x, w, st):
    tm = _row_tile(grp.t, ROW_TILE)
    u = _pw1(grp, layer, x, w["norm_mix"], w["w_pw1"], w["b_pw1"], _row_tile(grp.n, ROW_TILE))
    x = _conv_prompt(grp, layer, u, x, w["w_dw"], w["b_dw"], w["ln_g"], w["ln_b"], w["w_pw2"], w["b_pw2"], tm)
    nb, t = grp.nb, grp.t
    state = u.reshape(nb, t, -1)[:, t - (CONV_WIDTH - 1):]
    return x, state


def _odd_sample(grp, layer, x, w, st):
    db, t, d = grp.nb, grp.t, grp.d
    u = _pw1(grp, layer, x, w["norm_mix"], w["w_pw1"], w["b_pw1"], _row_tile(grp.n, ROW_TILE))
    buf = st["state_conv"][st["conv_layer"]]
    u_t = u.reshape(t, db, -1)
    x_t = _conv_sample(jnp.swapaxes(buf, 0, 1), u_t, x.reshape(t, db, d), st["mod_batch"][layer], w["w_dw"],
                       w["b_dw"], w["ln_g"], w["ln_b"], w["w_pw2"], w["b_pw2"])
    state = jnp.concatenate([buf.astype(F32), jnp.swapaxes(u_t, 0, 1)], axis=1)[:, t:]
    return x_t.reshape(db * t, d), state


def _trunk(grp, x, even_fn, odd_fn, layer_w, st, norm_final):
    depth = len(layer_w)
    att, conv = None, []
    for i in range(depth):
        w = layer_w[i]
        if i % 2 == 0:
            x, att = even_fn(grp, i, x, w, dict(st, att_layer=i // 2, n_att=(depth + 1) // 2, att_prev=att))
        else:
            x, s = odd_fn(grp, i, x, w, dict(st, conv_layer=i // 2))
            conv.append(s)
        tm = _row_tile(grp.n, FFN_ROW_TILE)
        x = _ffn(grp, i, x, w["norm_ffn"], w["w_ffn_in"], w["w_ffn_out"], norm_final, i == depth - 1, tm,
                 w["tf"])
    return x, att, conv


def kernel(x_prompt, x_sample, cache_a_k, cache_a_v, cache_a_idx_k, cache_b_k, cache_b_v, cache_b_logf,
           state_conv, page_table, c_prompt, c_sample, w_in_att, b_fgate, w_out_att, w_pw1, b_pw1, w_dw,
           b_dw, ln_conv_g, ln_conv_b, w_pw2, b_pw2, w_ada, b_ada, norm_mix, norm_ffn, w_ffn_in,
           w_ffn_out, norm_final):
    bp, sp, d = x_prompt.shape
    db, ts, _ = x_sample.shape
    depth = w_ada.shape[0]
    npg, ps = page_table.shape[1], cache_a_idx_k.shape[2]
    past = npg * ps

    mod = _ada(jnp.concatenate([c_prompt, c_sample], axis=0), w_ada, b_ada)
    mod_p = mod[:, :bp].reshape(depth, bp, 1, 6 * d)
    mod_b = mod[:, bp:]
    grp_p = _Group(bp, sp, d, mod_p, batch_major=True)
    grp_s = _Group(db, ts, d, mod_b.reshape(depth, 1, db, 6 * d), batch_major=False)

    ff = w_ffn_out.shape[1]
    tf = ff // 2 if ff % (2 * LANES) == 0 else ff
    half_a = H_A * HEAD_DIM
    layer_w = []
    w_ffn_in_mm, w_ffn_out_mm = w_ffn_in.astype(_MM), w_ffn_out.astype(_MM)
    for i in range(depth):
        w = {"norm_mix": norm_mix[i], "norm_ffn": norm_ffn[i], "w_ffn_in": w_ffn_in_mm, "w_ffn_out": w_ffn_out_mm,
             "tf": tf}
        l = i // 2
        if i % 2 == 0:
            w["w_in_slots"], w["groups_slots"] = _pack_w_in(w_in_att[l], True)
            w["w_in_plain"], w["groups_plain"] = _pack_w_in(w_in_att[l], False)
            w["b_f"] = b_fgate[l]
            wo = w_out_att[l]
            wa = wo[:half_a].reshape(H_A, HEAD_DIM, d)
            zero = jnp.zeros((HEAD_DIM, d), wo.dtype)
            parts = []
            for hh in range(H_A):
                parts += [wa[hh], zero] if hh // (H_A // KV_A) == 0 else [zero, wa[hh]]
            w["w_out_a_slots"] = jnp.concatenate(parts, axis=0).astype(_MM)
            w["w_out_b"] = wo[half_a:].astype(_MM)
        else:
            w.update(w_pw1=w_pw1[l].astype(_MM), b_pw1=b_pw1[l], w_dw=w_dw[l], b_dw=b_dw[l], ln_g=ln_conv_g[l],
                     ln_b=ln_conv_b[l], w_pw2=w_pw2[l].astype(_MM), b_pw2=b_pw2[l])
        layer_w.append(w)

    st_p = {"rope": _rope_tables(jnp.arange(sp, dtype=I32).astype(F32))}
    pos_s = (past + jnp.arange(db * ts, dtype=I32) // db).astype(F32)
    pos_last5 = lambda a: jnp.transpose(a, (0, 1, 3, 4, 2))
    pos_last4 = lambda a: jnp.transpose(a, (0, 1, 3, 2))
    caches_t = (pos_last5(cache_b_k), pos_last5(cache_b_v), pos_last5(cache_a_k), pos_last5(cache_a_v),
                pos_last4(cache_a_idx_k), pos_last4(cache_b_logf))
    st_s = {"rope": _rope_tables(pos_s), "page_table": page_table, "caches_t": caches_t,
            "state_conv": state_conv, "mod_batch": mod_b}

    y_p, att_p, conv_p = _trunk(grp_p, x_prompt.reshape(bp * sp, d), _even_prompt, _odd_prompt, layer_w, st_p,
                                norm_final)
    y_s, att_s, conv_s = _trunk(grp_s, jnp.swapaxes(x_sample, 0, 1).reshape(db * ts, d), _even_sample, _odd_sample,
                                layer_w, st_s, norm_final)
    y_s = jnp.swapaxes(y_s.reshape(ts, db, d), 0, 1)

    def states(att, rows_to_batch):
        kb, vb, ka, va, misc = (rows_to_batch(a) for a in att)
        lead = kb.shape[:3]
        return (ka.reshape(*lead, KV_A, HEAD_DIM), va.reshape(*lead, KV_A, HEAD_DIM), misc[..., :D_IDX],
                kb.reshape(*lead, H_B, HEAD_DIM), vb.reshape(*lead, H_B, HEAD_DIM),
                misc[..., MISC_FB:MISC_FB + H_B])

    n_att = (depth + 1) // 2

    def heads_last(a, nh):
        return a.reshape(n_att, bp, nh, -1, sp).transpose(0, 1, 4, 2, 3)

    kb_t, vb_t, ka_t, va_t, misc_t = (_pos_minor(a, bp, sp) for a in att_p)
    out_p = (heads_last(ka_t, KV_A), heads_last(va_t, KV_A), jnp.swapaxes(misc_t[:, :, :D_IDX], 2, 3),
             heads_last(kb_t, H_B), heads_last(vb_t, H_B), jnp.swapaxes(misc_t[:, :, MISC_FB:MISC_FB + H_B], 2, 3))
    out_s = states(att_s, lambda a: jnp.swapaxes(a.reshape(n_att, ts, db, -1), 1, 2))
    return (y_p.reshape(bp, sp, d), y_s, *out_p, jnp.stack(conv_p), *out_s, jnp.stack(conv_s))
```
